```python
import jax, jax.numpy as jnp
from jax import lax
import numpy as np

D_MODEL = 2048
BATCH = 2
SEQ = 8192
DEPTH = 4
DEC_BATCH = 16
DEC_SEQ = 32
PAST_LEN = 4096

CHUNK = 64
N_HEADS = 16
N_KV_HEADS = 4
HEAD_DIM = 64
Q_PER_KV = N_HEADS // N_KV_HEADS
ATTN_WIDTH = N_HEADS * HEAD_DIM
KV_WIDTH = N_KV_HEADS * HEAD_DIM
WINDOW = 128
WINDOW_CHUNKS = WINDOW // CHUNK
ROT_DIM = HEAD_DIM // 4
ROPE_THETA = 500000.0
GM_CHUNK = 128
GM_GROUPS = 8
GM_GROUP_DIM = 128
GM_WIDTH = GM_GROUPS * GM_GROUP_DIM
MIX_WIDTH = ATTN_WIDTH + GM_WIDTH
IN_WIDTH = ATTN_WIDTH + 2 * KV_WIDTH + 2 * GM_WIDTH
IN_SPLITS = (ATTN_WIDTH, ATTN_WIDTH + KV_WIDTH, ATTN_WIDTH + 2 * KV_WIDTH,
             ATTN_WIDTH + 2 * KV_WIDTH + GM_WIDTH)
N_EXPERTS = 16
N_EXPERT_GROUPS = 4
EXPERTS_PER_GROUP = N_EXPERTS // N_EXPERT_GROUPS
TOP_K = 2
D_EXPERT = 512
EXPERT_BLOCK = 128
DN_ALPHA = (2 * DEPTH) ** 0.25
DN_BETA = (8 * DEPTH) ** -0.25
NORM_EPS = 1e-5
NEG_INF = -1e30

kernel_name = "hymba_swa_sink_gmlp_grouped_moe_stream_step"


def _layer_norm(x, g, b):
    xf = x.astype(jnp.float32)
    xc = xf - xf.mean(-1, keepdims=True)
    var = (xc * xc).mean(-1, keepdims=True)
    y = xc * lax.rsqrt(var + NORM_EPS) * g.astype(jnp.float32) + b.astype(jnp.float32)
    return y.astype(x.dtype)


def _rms_norm(x, g):
    xf = x.astype(jnp.float32)
    y = xf * lax.rsqrt((xf * xf).mean(-1, keepdims=True) + NORM_EPS) * g.astype(jnp.float32)
    return y.astype(x.dtype)


def _partial_rope(x, pos):
    half = ROT_DIM // 2
    inv_freq = jnp.power(ROPE_THETA, -2.0 * jnp.arange(half, dtype=jnp.float32) / ROT_DIM)
    ang = pos.astype(jnp.float32)[:, None] * inv_freq[None, :]
    cos, sin = jnp.cos(ang)[:, None, :], jnp.sin(ang)[:, None, :]
    xr = x[..., :ROT_DIM].astype(jnp.float32)
    x1, x2 = xr[..., :half], xr[..., half:]
    rot = jnp.concatenate([x1 * cos - x2 * sin, x2 * cos + x1 * sin], -1).astype(x.dtype)
    return jnp.concatenate([rot, x[..., ROT_DIM:]], -1)


def _sink_softmax(scores, sink):
    m = jnp.maximum(scores.max(-1, keepdims=True), sink)
    e = jnp.exp(scores - m)
    return e / (e.sum(-1, keepdims=True) + jnp.exp(sink - m))


def _project_in(x, w_in, pos):
    B, S, _ = x.shape
    h = jnp.einsum('bsd,de->bse', x, w_in)
    q, k, v, gu, gv = jnp.split(h, IN_SPLITS, axis=-1)
    q = _partial_rope(q.reshape(B, S, N_HEADS, HEAD_DIM), pos)
    k = _partial_rope(k.reshape(B, S, N_KV_HEADS, HEAD_DIM), pos)
    v = v.reshape(B, S, N_KV_HEADS, HEAD_DIM)
    return q, k, v, jax.nn.gelu(gu, approximate=False), jax.nn.gelu(gv, approximate=False)


def _swa_prompt(q, k, v, sinks):
    B, S = q.shape[:2]
    n_c = S // CHUNK
    pad = WINDOW_CHUNKS * CHUNK
    band = (WINDOW_CHUNKS + 1) * CHUNK

    def band_blocks(t):
        tp = jnp.pad(t, ((0, 0), (pad, 0), (0, 0), (0, 0)))
        tp = tp.reshape(B, n_c + WINDOW_CHUNKS, CHUNK, N_KV_HEADS, HEAD_DIM)
        return jnp.concatenate([tp[:, i:i + n_c] for i in range(WINDOW_CHUNKS + 1)], axis=2)

    kb, vb = band_blocks(k), band_blocks(v)
    qb = q.reshape(B, n_c, CHUNK, N_KV_HEADS, Q_PER_KV, HEAD_DIM)
    s = jnp.einsum('bcqkgd,bcjkd->bckgqj', qb, kb,
                   preferred_element_type=jnp.float32) * (HEAD_DIM ** -0.5)
    key_pos = jnp.arange(n_c)[:, None] * CHUNK + jnp.arange(band)[None, :] - pad
    s = jnp.where((key_pos >= 0)[None, :, None, None, None, :], s, NEG_INF)
    sink = sinks.astype(jnp.float32).reshape(1, 1, N_KV_HEADS, Q_PER_KV, 1, 1)
    p = _sink_softmax(s, sink)
    o = jnp.einsum('bckgqj,bcjkd->bcqkgd', p.astype(v.dtype), vb)
    return o.reshape(B, S, ATTN_WIDTH)


def _swa_sample(q, k_all, v_all, sinks):
    B, S = q.shape[:2]
    qh = q.reshape(B, S, N_KV_HEADS, Q_PER_KV, HEAD_DIM)
    s = jnp.einsum('bqkgd,bjkd->bkgqj', qh, k_all,
                   preferred_element_type=jnp.float32) * (HEAD_DIM ** -0.5)
    sink = sinks.astype(jnp.float32).reshape(1, N_KV_HEADS, Q_PER_KV, 1, 1)
    p = _sink_softmax(s, sink)
    o = jnp.einsum('bkgqj,bjkd->bqkgd', p.astype(v_all.dtype), v_all)
    return o.reshape(B, S, ATTN_WIDTH)


def _gm_masked(ws):
    c = jnp.arange(GM_CHUNK) // CHUNK
    return jnp.where(c[:, None] >= c[None, :], ws, 0.0)


def _gmlp_prompt(gu, gv, ln_g, ln_b, ws, bs):
    B, S, _ = gu.shape
    vn = _layer_norm(gv.reshape(B, S, GM_GROUPS, GM_GROUP_DIM), ln_g, ln_b)
    vc = vn.reshape(B, S // GM_CHUNK, GM_CHUNK, GM_GROUPS, GM_GROUP_DIM)
    mix = jnp.einsum('gij,bnjgc->bnigc', _gm_masked(ws), vc) + bs.T[:, :, None]
    return gu * mix.reshape(B, S, GM_WIDTH)


def _gmlp_sample(gu, gv, ln_g, ln_b, ws, bs):
    B, S, _ = gu.shape
    vn = _layer_norm(gv.reshape(B, S, GM_GROUPS, GM_GROUP_DIM), ln_g, ln_b)
    w = _gm_masked(ws)[:, :S, :S]
    mix = jnp.einsum('gij,bjgc->bigc', w, vn) + bs[:, :S].T[:, :, None]
    return gu * mix.reshape(B, S, GM_WIDTH), vn


def _merge(attn, gm, g_attn, g_gm, w_o):
    h = jnp.concatenate([_rms_norm(attn, g_attn), _rms_norm(gm, g_gm)], -1)
    return jnp.einsum('bsm,md->bsd', h, w_o)


def _route(xt, router_w, router_b):
    T = xt.shape[0]
    scores = jax.nn.sigmoid(jnp.einsum('td,de->te', xt, router_w,
                                       preferred_element_type=jnp.float32))
    sel = (scores + router_b.astype(jnp.float32)).reshape(T, N_EXPERT_GROUPS, EXPERTS_PER_GROUP)
    group_score = lax.top_k(sel, TOP_K)[0].sum(-1)
    best = jnp.argmax(group_score, axis=-1)
    sel_in = jnp.take_along_axis(sel, best[:, None, None], axis=1)[:, 0]
    _, local = lax.top_k(sel_in, TOP_K)
    idx = best[:, None] * EXPERTS_PER_GROUP + local
    gate = jnp.take_along_axis(scores, idx, axis=-1)
    return idx, gate / gate.sum(-1, keepdims=True)


def _moe(x, router_w, router_b, w_gate, w_up, w_down):
    shp = x.shape
    xt = x.reshape(-1, D_MODEL)
    T = xt.shape[0]
    idx, gate = _route(xt, router_w, router_b)
    A = T * TOP_K
    flat_e = idx.reshape(A)
    order = jnp.argsort(flat_e)
    e_sorted = flat_e[order]
    tok_sorted = order // TOP_K
    counts = jnp.bincount(flat_e, length=N_EXPERTS)
    padded = (counts + EXPERT_BLOCK - 1) // EXPERT_BLOCK * EXPERT_BLOCK
    start = jnp.cumsum(counts) - counts
    pend = jnp.cumsum(padded)
    pstart = pend - padded
    dest = pstart[e_sorted] + (jnp.arange(A) - start[e_sorted])
    n_blocks = -(-A // EXPERT_BLOCK) + N_EXPERTS
    rows = jnp.zeros((n_blocks * EXPERT_BLOCK, D_MODEL), xt.dtype).at[dest].set(xt[tok_sorted])
    block_expert = jnp.clip(jnp.searchsorted(pend, jnp.arange(n_blocks) * EXPERT_BLOCK,
                                             side='right'), 0, N_EXPERTS - 1)

    def run_block(args):
        xb, e = args
        h = jax.nn.silu(xb @ w_gate[e]) * (xb @ w_up[e])
        return h @ w_down[e]

    out_rows = lax.map(run_block, (rows.reshape(n_blocks, EXPERT_BLOCK, D_MODEL), block_expert))
    out_rows = out_rows.reshape(n_blocks * EXPERT_BLOCK, D_MODEL)
    contrib = out_rows[dest] * gate.reshape(A)[order][:, None].astype(xt.dtype)
    y = jnp.zeros((T, D_MODEL), xt.dtype).at[tok_sorted].add(contrib)
    return y.reshape(shp)


def setup_inputs(seed: int = 0) -> dict:
    key = jax.random.key(seed)
    ks = jax.random.split(key, 24)
    f32 = jnp.float32
    nrm = lambda k, shape, s: jax.random.normal(k, shape, f32) * s
    v_cols = jnp.ones((IN_WIDTH,), f32).at[ATTN_WIDTH + KV_WIDTH:ATTN_WIDTH + 2 * KV_WIDTH].set(DN_BETA)
    return {
        "x_prompt": nrm(ks[0], (BATCH, SEQ, D_MODEL), 1.0),
        "x_sample": nrm(ks[1], (DEC_BATCH, DEC_SEQ, D_MODEL), 1.0),
        "cache_k": nrm(ks[2], (DEPTH, DEC_BATCH, WINDOW, N_KV_HEADS, HEAD_DIM), 1.0),
        "cache_v": nrm(ks[3], (DEPTH, DEC_BATCH, WINDOW, N_KV_HEADS, HEAD_DIM), 1.0),
        "w_in": nrm(ks[4], (DEPTH, D_MODEL, IN_WIDTH), D_MODEL ** -0.5) * v_cols,
        "w_o": nrm(ks[5], (DEPTH, MIX_WIDTH, D_MODEL), MIX_WIDTH ** -0.5 * DN_BETA),
        "attn_sinks": nrm(ks[6], (DEPTH, N_HEADS), 0.5),
        "attn_out_g": 1.0 + nrm(ks[7], (DEPTH, ATTN_WIDTH), 0.02),
        "gm_out_g": 1.0 + nrm(ks[8], (DEPTH, GM_WIDTH), 0.02),
        "gm_ln_g": 1.0 + nrm(ks[9], (DEPTH, GM_GROUPS, GM_GROUP_DIM), 0.02),
        "gm_ln_b": nrm(ks[10], (DEPTH, GM_GROUPS, GM_GROUP_DIM), 0.02),
        "gm_ws": nrm(ks[11], (DEPTH, GM_GROUPS, GM_CHUNK, GM_CHUNK), GM_CHUNK ** -0.5),
        "gm_bs": 1.0 + nrm(ks[12], (DEPTH, GM_GROUPS, GM_CHUNK), 0.02),
        "ln1_g": 1.0 + nrm(ks[13], (DEPTH, D_MODEL), 0.02),
        "ln1_b": nrm(ks[14], (DEPTH, D_MODEL), 0.02),
        "ln2_g": 1.0 + nrm(ks[15], (DEPTH, D_MODEL), 0.02),
        "ln2_b": nrm(ks[16], (DEPTH, D_MODEL), 0.02),
        "router_w": nrm(ks[17], (D_MODEL, N_EXPERTS), D_MODEL ** -0.5),
        "router_b": nrm(ks[18], (N_EXPERTS,), 0.01),
        "w_gate": nrm(ks[19], (DEPTH, N_EXPERTS, D_MODEL, D_EXPERT), D_MODEL ** -0.5),
        "w_up": nrm(ks[20], (DEPTH, N_EXPERTS, D_MODEL, D_EXPERT), D_MODEL ** -0.5),
        "w_down": nrm(ks[21], (DEPTH, N_EXPERTS, D_EXPERT, D_MODEL), D_EXPERT ** -0.5 * DN_BETA),
    }


def reference(x_prompt, x_sample, cache_k, cache_v, w_in, w_o, attn_sinks, attn_out_g, gm_out_g,
              gm_ln_g, gm_ln_b, gm_ws, gm_bs, ln1_g, ln1_b, ln2_g, ln2_b, router_w, router_b,
              w_gate, w_up, w_down):
    pos_p = jnp.arange(x_prompt.shape[1], dtype=jnp.int32)
    pos_s = PAST_LEN + jnp.arange(x_sample.shape[1], dtype=jnp.int32)
    xp, xs = x_prompt, x_sample
    kp_list, vp_list, ks_list, vs_list, gv_list = [], [], [], [], []
    for l in range(DEPTH):
        q, k, v, gu, gv = _project_in(xp, w_in[l], pos_p)
        attn = _swa_prompt(q, k, v, attn_sinks[l])
        gm = _gmlp_prompt(gu, gv, gm_ln_g[l], gm_ln_b[l], gm_ws[l], gm_bs[l])
        xp = _layer_norm(DN_ALPHA * xp + _merge(attn, gm, attn_out_g[l], gm_out_g[l], w_o[l]),
                         ln1_g[l], ln1_b[l])
        xp = _layer_norm(DN_ALPHA * xp + _moe(xp, router_w, router_b, w_gate[l], w_up[l], w_down[l]),
                         ln2_g[l], ln2_b[l])
        kp_list.append(k[:, -WINDOW:])
        vp_list.append(v[:, -WINDOW:])
        q, k, v, gu, gv = _project_in(xs, w_in[l], pos_s)
        k_all = jnp.concatenate([cache_k[l], k], axis=1)
        v_all = jnp.concatenate([cache_v[l], v], axis=1)
        attn = _swa_sample(q, k_all, v_all, attn_sinks[l])
        gm, vn = _gmlp_sample(gu, gv, gm_ln_g[l], gm_ln_b[l], gm_ws[l], gm_bs[l])
        xs = _layer_norm(DN_ALPHA * xs + _merge(attn, gm, attn_out_g[l], gm_out_g[l], w_o[l]),
                         ln1_g[l], ln1_b[l])
        xs = _layer_norm(DN_ALPHA * xs + _moe(xs, router_w, router_b, w_gate[l], w_up[l], w_down[l]),
                         ln2_g[l], ln2_b[l])
        ks_list.append(k_all[:, -WINDOW:])
        vs_list.append(v_all[:, -WINDOW:])
        gv_list.append(vn)
    new_k_prompt = jnp.stack(kp_list)
    new_v_prompt = jnp.stack(vp_list)
    new_k_sample = jnp.stack(ks_list)
    new_v_sample = jnp.stack(vs_list)
    gm_v_sample = jnp.stack(gv_list)
    return (xp, xs, new_k_prompt, new_v_prompt, new_k_sample, new_v_sample, gm_v_sample)
```

```python
import functools

import jax
import jax.numpy as jnp
import numpy as np
from jax import lax
from jax.experimental import pallas as pl
from jax.experimental.pallas import tpu as pltpu

F32 = jnp.float32
BF16 = jnp.bfloat16

CHUNK = 64
N_HEADS = 16
N_KV_HEADS = 4
HEAD_DIM = 64
Q_PER_KV = N_HEADS // N_KV_HEADS
ATTN_WIDTH = N_HEADS * HEAD_DIM
KV_WIDTH = N_KV_HEADS * HEAD_DIM
WINDOW = 128
ROT_DIM = HEAD_DIM // 4
ROPE_THETA = 500000.0
GM_CHUNK = 128
GM_GROUPS = 8
GM_GROUP_DIM = 128
GM_WIDTH = GM_GROUPS * GM_GROUP_DIM
N_EXPERTS = 16
N_EXPERT_GROUPS = 4
EXPERTS_PER_GROUP = N_EXPERTS // N_EXPERT_GROUPS
NORM_EPS = 1e-5
NEG_INF = -1e30
PAST_LEN = 4096

LANES = 128
KEY_BLOCK = 256
Q_BLOCK = 128
EXPERT_ROWS = 256
VMEM_LIMIT = 56 * 1024 * 1024


def _cparams(sem):
    return pltpu.CompilerParams(dimension_semantics=sem, vmem_limit_bytes=VMEM_LIMIT)


def _resident(shape, index_map):
    return pl.BlockSpec(shape, index_map, pipeline_mode=pl.Buffered(1))


def _gelu(a):
    return 0.5 * a * (1.0 + lax.erf(a * np.float32(0.7071067811865476)))


def _pair_expand(t):
    r = pltpu.roll(t, HEAD_DIM, 1)
    lane = lax.broadcasted_iota(jnp.int32, t.shape, 1)
    lo = lane < HEAD_DIM
    return jnp.where(lo, t, r), jnp.where(lo, r, t)


def _expand_kv(a):
    a0, b0 = _pair_expand(a[:, 0:LANES])
    a1, b1 = _pair_expand(a[:, LANES:2 * LANES])
    return jnp.concatenate([a0, b0, a1, b1], axis=1)


def _inproj_body(x_ref, w_ref, rope_ref, lng_ref, lnb_ref,
                 q_ref, k_ref, v_ref, kp_ref, vp_ref, gu_ref, vn_ref):
    xb = x_ref[...].astype(BF16)
    cs = rope_ref[:, 0:LANES]
    s1 = rope_ref[:, LANES:2 * LANES]
    s2 = rope_ref[:, 2 * LANES:3 * LANES]

    def rope(a):
        n = a.shape[1] // LANES
        width = a.shape[1]
        c = jnp.concatenate([cs] * n, axis=1)
        m1 = jnp.concatenate([s1] * n, axis=1)
        m2 = jnp.concatenate([s2] * n, axis=1)
        return a * c + pltpu.roll(a, width - ROT_DIM // 2, 1) * m1 + pltpu.roll(a, ROT_DIM // 2, 1) * m2

    def proj(c0, c1):
        return jnp.dot(xb, w_ref[:, c0:c1], preferred_element_type=F32)

    half = ATTN_WIDTH // 2
    for h in range(2):
        q = rope(proj(h * half, (h + 1) * half)) * np.float32(HEAD_DIM ** -0.5)
        q_ref[:, h * half:(h + 1) * half] = q.astype(BF16)

    kv = proj(ATTN_WIDTH, ATTN_WIDTH + 2 * KV_WIDTH)
    k = rope(kv[:, 0:KV_WIDTH])
    v = kv[:, KV_WIDTH:2 * KV_WIDTH]
    k_ref[...] = k
    v_ref[...] = v
    kp_ref[...] = _expand_kv(k).astype(BF16)
    vp_ref[...] = _expand_kv(v).astype(BF16)

    base = ATTN_WIDTH + 2 * KV_WIDTH
    for h in range(2):
        gu_ref[:, h * half:(h + 1) * half] = _gelu(proj(base + h * half, base + (h + 1) * half))

    base = base + GM_WIDTH
    for h in range(2):
        gv = _gelu(proj(base + h * half, base + (h + 1) * half))
        for g in range(half // GM_GROUP_DIM):
            t = gv[:, g * GM_GROUP_DIM:(g + 1) * GM_GROUP_DIM]
            col = h * half + g * GM_GROUP_DIM
            tc = t - jnp.mean(t, axis=-1, keepdims=True)
            var = jnp.mean(tc * tc, axis=-1, keepdims=True)
            y = tc * lax.rsqrt(var + NORM_EPS) * lng_ref[:, col:col + GM_GROUP_DIM] \
                + lnb_ref[:, col:col + GM_GROUP_DIM]
            vn_ref[:, col:col + GM_GROUP_DIM] = y


def _inproj(x, w, rope_tab, lng, lnb, tm):
    t_all, d = x.shape
    n_in = w.shape[1]
    row = lambda i: (i, 0)
    const = lambda i: (0, 0)
    out_shape = (
        jax.ShapeDtypeStruct((t_all, ATTN_WIDTH), BF16),
        jax.ShapeDtypeStruct((t_all, KV_WIDTH), F32),
        jax.ShapeDtypeStruct((t_all, KV_WIDTH), F32),
        jax.ShapeDtypeStruct((t_all, 2 * KV_WIDTH), BF16),
        jax.ShapeDtypeStruct((t_all, 2 * KV_WIDTH), BF16),
        jax.ShapeDtypeStruct((t_all, GM_WIDTH), F32),
        jax.ShapeDtypeStruct((t_all, GM_WIDTH), F32),
    )
    return pl.pallas_call(
        _inproj_body,
        grid=(t_all // tm,),
        in_specs=[
            pl.BlockSpec((tm, d), row),
            _resident((d, n_in), const),
            pl.BlockSpec((tm, 3 * LANES), row),
            _resident((1, GM_WIDTH), const),
            _resident((1, GM_WIDTH), const),
        ],
        out_specs=(
            pl.BlockSpec((tm, ATTN_WIDTH), row),
            pl.BlockSpec((tm, KV_WIDTH), row),
            pl.BlockSpec((tm, KV_WIDTH), row),
            pl.BlockSpec((tm, 2 * KV_WIDTH), row),
            pl.BlockSpec((tm, 2 * KV_WIDTH), row),
            pl.BlockSpec((tm, GM_WIDTH), row),
            pl.BlockSpec((tm, GM_WIDTH), row),
        ),
        out_shape=out_shape,
        compiler_params=_cparams(("parallel",)),
        name="inproj",
    )(x, w, rope_tab, lng, lnb)


def _attn_group(qg, kexp, vexp, bias, sinks):
    nk = kexp.shape[0]
    seg = lax.broadcasted_iota(jnp.int32, kexp.shape, 1) // HEAD_DIM
    zero = jnp.zeros_like(kexp)
    kbd = jnp.concatenate([jnp.where(seg == j, kexp, zero) for j in range(Q_PER_KV)], axis=0)
    vbd = jnp.concatenate([jnp.where(seg == j, vexp, zero) for j in range(Q_PER_KV)], axis=0)
    s = lax.dot_general(qg, kbd, (((1,), (1,)), ((), ())), preferred_element_type=F32)
    es, rs = [], []
    for j in range(Q_PER_KV):
        sj = s[:, j * nk:(j + 1) * nk] + bias
        m = jnp.maximum(jnp.max(sj, axis=-1, keepdims=True), sinks[j])
        e = jnp.exp(sj - m)
        den = jnp.sum(e, axis=-1, keepdims=True) + jnp.exp(sinks[j] - m)
        es.append(e.astype(BF16))
        rs.append(1.0 / den)
    p = jnp.concatenate(es, axis=1)
    o = jnp.dot(p, vbd, preferred_element_type=F32)
    oseg = lax.broadcasted_iota(jnp.int32, o.shape, 1) // HEAD_DIM
    r = jnp.where(oseg == 0, rs[0], jnp.where(oseg == 1, rs[1], jnp.where(oseg == 2, rs[2], rs[3])))
    return o * r


def _attn_rows(q, kbuf, vbuf, bias, sink_ref, g_row):
    outs = []
    for h in range(N_KV_HEADS):
        kt = kbuf[:, h * LANES:(h + 1) * LANES]
        vt = vbuf[:, h * LANES:(h + 1) * LANES]
        kexp = jnp.concatenate([kt, kt], axis=1)
        vexp = jnp.concatenate([vt, vt], axis=1)
        sinks = [sink_ref[h * Q_PER_KV + j] for j in range(Q_PER_KV)]
        width = Q_PER_KV * HEAD_DIM
        outs.append(_attn_group(q[:, h * width:(h + 1) * width], kexp, vexp, bias, sinks))
    o = jnp.concatenate(outs, axis=1)
    ms = jnp.mean(o * o, axis=-1, keepdims=True)
    return o * lax.rsqrt(ms + NORM_EPS) * g_row


def _attn_prompt_body(sink_ref, q_ref, kc_ref, kprev_ref, vc_ref, vprev_ref, g_ref, o_ref,
                      kbuf, vbuf, *, tq):
    i = pl.program_id(1)
    kbuf[0:WINDOW, :] = kprev_ref[...]
    kbuf[WINDOW:WINDOW + tq, :] = kc_ref[...]
    vbuf[0:WINDOW, :] = vprev_ref[...]
    vbuf[WINDOW:WINDOW + tq, :] = vc_ref[...]
    rq = lax.broadcasted_iota(jnp.int32, (Q_BLOCK, KEY_BLOCK), 0) // CHUNK
    ck = lax.broadcasted_iota(jnp.int32, (Q_BLOCK, KEY_BLOCK), 1) // CHUNK
    rel = ck - rq
    band = (rel >= 0) & (rel <= WINDOW // CHUNK)
    g_row = g_ref[...]

    def sub(s, carry):
        r0 = pl.multiple_of(s * Q_BLOCK, Q_BLOCK)
        first = jnp.logical_and(i == 0, s == 0)
        kmin = jnp.where(first, WINDOW // CHUNK, 0)
        ok = jnp.logical_and(band, ck >= kmin)
        bias = jnp.where(ok, np.float32(0.0), np.float32(NEG_INF))
        q = q_ref[pl.ds(r0, Q_BLOCK), :]
        kb = kbuf[pl.ds(r0, KEY_BLOCK), :]
        vb = vbuf[pl.ds(r0, KEY_BLOCK), :]
        o_ref[pl.ds(r0, Q_BLOCK), :] = _attn_rows(q, kb, vb, bias, sink_ref, g_row).astype(BF16)
        return carry

    lax.fori_loop(0, tq // Q_BLOCK, sub, 0)


def _attn_prompt(sinks, q, kp, vp, g_attn, batch, seq, tq):
    nq = seq // tq
    per = tq // WINDOW
    cur = lambda b, i: (b * nq + i, 0)
    prev = lambda b, i: (b * (seq // WINDOW) + jnp.maximum(i * per - 1, 0), 0)
    kernel = functools.partial(_attn_prompt_body, tq=tq)
    return pl.pallas_call(
        kernel,
        grid=(batch, nq),
        in_specs=[
            pl.BlockSpec(memory_space=pltpu.SMEM),
            pl.BlockSpec((tq, ATTN_WIDTH), cur),
            pl.BlockSpec((tq, 2 * KV_WIDTH), cur),
            pl.BlockSpec((WINDOW, 2 * KV_WIDTH), prev),
            pl.BlockSpec((tq, 2 * KV_WIDTH), cur),
            pl.BlockSpec((WINDOW, 2 * KV_WIDTH), prev),
            pl.BlockSpec((1, ATTN_WIDTH), lambda b, i: (0, 0)),
        ],
        out_specs=pl.BlockSpec((tq, ATTN_WIDTH), cur),
        out_shape=jax.ShapeDtypeStruct((batch * seq, ATTN_WIDTH), BF16),
        scratch_shapes=[pltpu.VMEM((WINDOW + tq, 2 * KV_WIDTH), BF16),
                        pltpu.VMEM((WINDOW + tq, 2 * KV_WIDTH), BF16)],
        compiler_params=_cparams(("parallel", "parallel")),
        name="attn_prompt",
    )(sinks, q, kp, kp, vp, vp, g_attn)


def _attn_sample_body(sink_ref, q_ref, ck_ref, cv_ref, kn_ref, vn_ref, g_ref, o_ref, *, ds):
    pad = jnp.zeros((KEY_BLOCK - WINDOW - ds, KV_WIDTH), F32)
    k_all = jnp.concatenate([ck_ref[0], kn_ref[...], pad], axis=0)
    v_all = jnp.concatenate([cv_ref[0], vn_ref[...], pad], axis=0)
    kb = _expand_kv(k_all).astype(BF16)
    vb = _expand_kv(v_all).astype(BF16)
    col = lax.broadcasted_iota(jnp.int32, (ds, KEY_BLOCK), 1)
    bias = jnp.where(col < WINDOW + ds, np.float32(0.0), np.float32(NEG_INF))
    o_ref[...] = _attn_rows(q_ref[...], kb, vb, bias, sink_ref, g_ref[...]).astype(BF16)


def _attn_sample(sinks, q, k32, v32, cache_k, cache_v, g_attn, row0, dbatch, ds):
    off = row0 // ds
    new = lambda b: (off + b, 0)
    kernel = functools.partial(_attn_sample_body, ds=ds)
    return pl.pallas_call(
        kernel,
        grid=(dbatch,),
        in_specs=[
            pl.BlockSpec(memory_space=pltpu.SMEM),
            pl.BlockSpec((ds, ATTN_WIDTH), new),
            pl.BlockSpec((1, WINDOW, KV_WIDTH), lambda b: (b, 0, 0)),
            pl.BlockSpec((1, WINDOW, KV_WIDTH), lambda b: (b, 0, 0)),
            pl.BlockSpec((ds, KV_WIDTH), new),
            pl.BlockSpec((ds, KV_WIDTH), new),
            pl.BlockSpec((1, ATTN_WIDTH), lambda b: (0, 0)),
        ],
        out_specs=pl.BlockSpec((ds, ATTN_WIDTH), lambda b: (b, 0)),
        out_shape=jax.ShapeDtypeStruct((dbatch * ds, ATTN_WIDTH), BF16),
        compiler_params=_cparams(("parallel",)),
        name="attn_sample",
    )(sinks, q, cache_k, cache_v, k32, v32, g_attn)


def _gmlp_body(gu_ref, vn_ref, ws_ref, bias_ref, g_ref, o_ref, gm_scr, *, rows, nchunk):
    ci = lax.broadcasted_iota(jnp.int32, (GM_CHUNK, GM_CHUNK), 0) // CHUNK
    cj = lax.broadcasted_iota(jnp.int32, (GM_CHUNK, GM_CHUNK), 1) // CHUNK
    visible = ci >= cj
    for g in range(GM_GROUPS):
        lo, hi = g * GM_GROUP_DIM, (g + 1) * GM_GROUP_DIM
        w = jnp.where(visible, ws_ref[g], np.float32(0.0))[0:rows, :].astype(BF16)
        pieces = []
        for n in range(nchunk):
            piece = vn_ref[n * rows:(n + 1) * rows, lo:hi]
            if rows < GM_CHUNK:
                piece = jnp.concatenate([piece, jnp.zeros((GM_CHUNK - rows, GM_GROUP_DIM), F32)], axis=0)
            pieces.append(piece.astype(BF16))
        rhs = jnp.concatenate(pieces, axis=1)
        mix = jnp.dot(w, rhs, preferred_element_type=F32)
        b = bias_ref[0:rows, lo:hi]
        for n in range(nchunk):
            gm_scr[n * rows:(n + 1) * rows, lo:hi] = \
                gu_ref[n * rows:(n + 1) * rows, lo:hi] * (mix[:, n * GM_GROUP_DIM:(n + 1) * GM_GROUP_DIM] + b)
    gm = gm_scr[...]
    ms = jnp.mean(gm * gm, axis=-1, keepdims=True)
    o_ref[...] = (gm * lax.rsqrt(ms + NORM_EPS) * g_ref[...]).astype(BF16)


def _gmlp(gu, vn, ws, bias_full, g_gm, row0, n_rows, rows, nchunk):
    tile = rows * nchunk
    off = row0 // tile
    kernel = functools.partial(_gmlp_body, rows=rows, nchunk=nchunk)
    return pl.pallas_call(
        kernel,
        grid=(n_rows // tile,),
        in_specs=[
            pl.BlockSpec((tile, GM_WIDTH), lambda i: (off + i, 0)),
            pl.BlockSpec((tile, GM_WIDTH), lambda i: (off + i, 0)),
            pl.BlockSpec((GM_GROUPS, GM_CHUNK, GM_CHUNK), lambda i: (0, 0, 0)),
            pl.BlockSpec((GM_CHUNK, GM_WIDTH), lambda i: (0, 0)),
            pl.BlockSpec((1, GM_WIDTH), lambda i: (0, 0)),
        ],
        out_specs=pl.BlockSpec((tile, GM_WIDTH), lambda i: (i, 0)),
        out_shape=jax.ShapeDtypeStruct((n_rows, GM_WIDTH), BF16),
        scratch_shapes=[pltpu.VMEM((tile, GM_WIDTH), F32)],
        compiler_params=_cparams(("parallel",)),
        name="gmlp",
    )(gu, vn, ws, bias_full, g_gm)


def _layer_norm_rows(z, g, b):
    zc = z - jnp.mean(z, axis=-1, keepdims=True)
    var = jnp.mean(zc * zc, axis=-1, keepdims=True)
    return zc * lax.rsqrt(var + NORM_EPS) * g + b


def _route_rows(lg, rb):
    sc = 1.0 / (1.0 + jnp.exp(-lg))
    sel = sc + rb
    rows = [sel[e:e + 1, :] for e in range(N_EXPERTS)]
    scr = [sc[e:e + 1, :] for e in range(N_EXPERTS)]
    gs = []
    for g in range(N_EXPERT_GROUPS):
        a, b, c, d = rows[g * EXPERTS_PER_GROUP:(g + 1) * EXPERTS_PER_GROUP]
        hi1, lo1 = jnp.maximum(a, b), jnp.minimum(a, b)
        hi2, lo2 = jnp.maximum(c, d), jnp.minimum(c, d)
        top1 = jnp.maximum(hi1, hi2)
        top2 = jnp.maximum(jnp.minimum(hi1, hi2), jnp.maximum(lo1, lo2))
        gs.append(top1 + top2)
    gmax = functools.reduce(jnp.maximum, gs)

    def first_hits(vals, target):
        hits, found = [], None
        for v in vals:
            eq = v == target
            if found is None:
                hits.append(eq)
                found = eq
            else:
                hits.append(jnp.logical_and(eq, jnp.logical_not(found)))
                found = jnp.logical_or(found, eq)
        return hits

    gsel = first_hits(gs, gmax)
    neg = np.float32(-np.inf)
    sel_in = [jnp.where(gsel[e // EXPERTS_PER_GROUP], rows[e], neg) for e in range(N_EXPERTS)]
    oh0 = first_hits(sel_in, functools.reduce(jnp.maximum, sel_in))
    sel_2 = [jnp.where(oh0[e], neg, sel_in[e]) for e in range(N_EXPERTS)]
    oh1 = first_hits(sel_2, functools.reduce(jnp.maximum, sel_2))
    zero = np.float32(0.0)
    g0 = functools.reduce(jnp.add, [jnp.where(oh0[e], scr[e], zero) for e in range(N_EXPERTS)])
    g1 = functools.reduce(jnp.add, [jnp.where(oh1[e], scr[e], zero) for e in range(N_EXPERTS)])
    tot = g0 + g1
    return oh0, oh1, g0 / tot, g1 / tot


def _outproj_body(an_ref, gn_ref, x_ref, wo_ref, g1_ref, b1_ref, rwt_ref, rb_ref,
                  x1_ref, mi_ref, mf_ref, cnt_ref, carry, *, alpha):
    i = pl.program_id(0)
    tm = x_ref.shape[0]

    @pl.when(i == 0)
    def _():
        carry[...] = jnp.zeros_like(carry)

    y = jnp.dot(an_ref[...], wo_ref[0:ATTN_WIDTH, :], preferred_element_type=F32)
    y = y + jnp.dot(gn_ref[...], wo_ref[ATTN_WIDTH:ATTN_WIDTH + GM_WIDTH, :], preferred_element_type=F32)
    x1 = _layer_norm_rows(np.float32(alpha) * x_ref[...] + y, g1_ref[...], b1_ref[...])
    x1_ref[...] = x1

    lg = lax.dot_general(rwt_ref[...], x1.astype(BF16), (((1,), (1,)), ((), ())),
                         preferred_element_type=F32)
    rb = jnp.concatenate([rb_ref[...]] * (tm // LANES), axis=1)
    oh0, oh1, gate0, gate1 = _route_rows(lg, rb)

    one, zero = np.float32(1.0), np.float32(0.0)
    chosen = jnp.concatenate(
        [jnp.where(jnp.logical_or(oh0[e], oh1[e]), one, zero) for e in range(N_EXPERTS)], axis=0)
    s_idx = lax.broadcasted_iota(jnp.int32, (tm, tm), 0)
    t_idx = lax.broadcasted_iota(jnp.int32, (tm, tm), 1)
    before = jnp.where(s_idx < t_idx, one, zero).astype(BF16)
    base = carry[...]
    rank = jnp.dot(chosen.astype(BF16), before, preferred_element_type=F32) \
        + jnp.concatenate([base] * (tm // LANES), axis=1)
    carry[...] = base + jnp.sum(chosen, axis=1, keepdims=True)
    cnt_ref[...] = carry[...]

    def pick(oh, vals):
        return functools.reduce(jnp.add, [jnp.where(oh[e], vals[e], zero) for e in range(N_EXPERTS)])

    rank_rows = [rank[e:e + 1, :] for e in range(N_EXPERTS)]
    ids = [np.float32(e) for e in range(N_EXPERTS)]
    zrow = jnp.zeros((1, tm), F32)
    meta_i = jnp.concatenate([pick(oh0, ids), pick(oh1, ids), pick(oh0, rank_rows), pick(oh1, rank_rows),
                              zrow, zrow, zrow, zrow], axis=0)
    mi_ref[...] = meta_i.astype(jnp.int32)
    mf_ref[...] = jnp.concatenate([gate0, gate1, zrow, zrow, zrow, zrow, zrow, zrow], axis=0)


def _outproj(attn_n, gm_n, x, wo, g1, b1, rwt, rb, alpha, tm):
    t_all, d = x.shape
    row = lambda i: (i, 0)
    col = lambda i: (0, i)
    const = lambda i: (0, 0)
    kernel = functools.partial(_outproj_body, alpha=alpha)
    return pl.pallas_call(
        kernel,
        grid=(t_all // tm,),
        in_specs=[
            pl.BlockSpec((tm, ATTN_WIDTH), row),
            pl.BlockSpec((tm, GM_WIDTH), row),
            pl.BlockSpec((tm, d), row),
            _resident((ATTN_WIDTH + GM_WIDTH, d), const),
            _resident((1, d), const),
            _resident((1, d), const),
            _resident((N_EXPERTS, d), const),
            _resident((N_EXPERTS, LANES), const),
        ],
        out_specs=(
            pl.BlockSpec((tm, d), row),
            pl.BlockSpec((8, tm), col),
            pl.BlockSpec((8, tm), col),
            pl.BlockSpec((N_EXPERTS, LANES), const),
        ),
        out_shape=(
            jax.ShapeDtypeStruct((t_all, d), F32),
            jax.ShapeDtypeStruct((8, t_all), jnp.int32),
            jax.ShapeDtypeStruct((8, t_all), F32),
            jax.ShapeDtypeStruct((N_EXPERTS, LANES), F32),
        ),
        scratch_shapes=[pltpu.VMEM((N_EXPERTS, LANES), F32)],
        compiler_params=_cparams(("arbitrary",)),
        name="outproj_router",
    )(attn_n, gm_n, x, wo, g1, b1, rwt, rb)


def _gather_rows(idx_ref, src_hbm, dst, sem, n):
    def body(r, c):
        pltpu.make_async_copy(src_hbm.at[pl.ds(idx_ref[0, 0, r], 1), :], dst.at[pl.ds(r, 1), :], sem).start()
        return c
    lax.fori_loop(0, n, body, 0)


def _wait_rows(src_hbm, dst, sem, n):
    def body(r, c):
        pltpu.make_async_copy(src_hbm.at[pl.ds(0, 1), :], dst.at[pl.ds(r, 1), :], sem).wait()
        return c
    lax.fori_loop(0, n, body, 0)


def _expert_body(be_ref, nu_ref, src0_ref, srcn_ref, x_hbm, wg_ref, wu_ref, wd_ref, o_ref, xbuf, sem):
    j = pl.program_id(0)
    n_used = nu_ref[0]
    slot = j % 2

    @pl.when(j == 0)
    def _():
        _gather_rows(src0_ref, x_hbm, xbuf.at[0], sem.at[0], EXPERT_ROWS)

    @pl.when(j + 1 < n_used)
    def _():
        _gather_rows(srcn_ref, x_hbm, xbuf.at[1 - slot], sem.at[1 - slot], EXPERT_ROWS)

    @pl.when(j < n_used)
    def _():
        _wait_rows(x_hbm, xbuf.at[slot], sem.at[slot], EXPERT_ROWS)
        xb = xbuf[slot].astype(BF16)
        gate = jnp.dot(xb, wg_ref[0], preferred_element_type=F32)
        up = jnp.dot(xb, wu_ref[0], preferred_element_type=F32)
        h = gate * (1.0 / (1.0 + jnp.exp(-gate))) * up
        o_ref[...] = jnp.dot(h.astype(BF16), wd_ref[0], preferred_element_type=F32)

    @pl.when(j >= n_used)
    def _():
        o_ref[...] = jnp.zeros_like(o_ref)


def _experts(block_expert, n_used, src_tok, x1, wg, wu, wd, n_blocks):
    d = x1.shape[1]
    de = wg.shape[2]
    src3 = src_tok.reshape(n_blocks, 1, EXPERT_ROWS)
    grid_spec = pltpu.PrefetchScalarGridSpec(
        num_scalar_prefetch=2,
        grid=(n_blocks,),
        in_specs=[
            pl.BlockSpec((1, 1, EXPERT_ROWS), lambda j, be, nu: (0, 0, 0), memory_space=pltpu.SMEM),
            pl.BlockSpec((1, 1, EXPERT_ROWS), lambda j, be, nu: (jnp.minimum(j + 1, n_blocks - 1), 0, 0),
                         memory_space=pltpu.SMEM),
            pl.BlockSpec(memory_space=pl.ANY),
            pl.BlockSpec((1, d, de), lambda j, be, nu: (be[j], 0, 0)),
            pl.BlockSpec((1, d, de), lambda j, be, nu: (be[j], 0, 0)),
            pl.BlockSpec((1, de, d), lambda j, be, nu: (be[j], 0, 0)),
        ],
        out_specs=pl.BlockSpec((EXPERT_ROWS, d), lambda j, be, nu: (j, 0)),
        scratch_shapes=[pltpu.VMEM((2, EXPERT_ROWS, d), F32), pltpu.SemaphoreType.DMA((2,))],
    )
    return pl.pallas_call(
        _expert_body,
        grid_spec=grid_spec,
        out_shape=jax.ShapeDtypeStruct((n_blocks * EXPERT_ROWS, d), F32),
        compiler_params=_cparams(("arbitrary",)),
        name="experts",
    )(block_expert, n_used, src3, src3, x1, wg, wu, wd)


def _combine_body(d0_ref, dn_ref, x1_ref, gt_ref, rows_hbm, g2_ref, b2_ref, x2_ref, rbuf, sem, *, alpha):
    i = pl.program_id(0)
    n = pl.num_programs(0)
    tm = x1_ref.shape[0]
    slot = i % 2

    @pl.when(i == 0)
    def _():
        _gather_rows(d0_ref, rows_hbm, rbuf.at[0], sem.at[0], 2 * tm)

    @pl.when(i + 1 < n)
    def _():
        _gather_rows(dn_ref, rows_hbm, rbuf.at[1 - slot], sem.at[1 - slot], 2 * tm)

    _wait_rows(rows_hbm, rbuf.at[slot], sem.at[slot], 2 * tm)
    g0 = gt_ref[:, 0:1]
    g1 = gt_ref[:, 1:2]
    y = rbuf[slot, 0:tm, :] * g0 + rbuf[slot, tm:2 * tm, :] * g1
    x2_ref[...] = _layer_norm_rows(np.float32(alpha) * x1_ref[...] + y, g2_ref[...], b2_ref[...])


def _combine(dest_tiles, x1, gates_t, out_rows, g2, b2, alpha, tm):
    t_all, d = x1.shape
    n_tiles = t_all // tm
    kernel = functools.partial(_combine_body, alpha=alpha)
    return pl.pallas_call(
        kernel,
        grid=(n_tiles,),
        in_specs=[
            pl.BlockSpec((1, 1, 2 * tm), lambda i: (0, 0, 0), memory_space=pltpu.SMEM),
            pl.BlockSpec((1, 1, 2 * tm), lambda i: (jnp.minimum(i + 1, n_tiles - 1), 0, 0),
                         memory_space=pltpu.SMEM),
            pl.BlockSpec((tm, d), lambda i: (i, 0)),
            pl.BlockSpec((tm, 2), lambda i: (i, 0)),
            pl.BlockSpec(memory_space=pl.ANY),
            _resident((1, d), lambda i: (0, 0)),
            _resident((1, d), lambda i: (0, 0)),
        ],
        out_specs=pl.BlockSpec((tm, d), lambda i: (i, 0)),
        out_shape=jax.ShapeDtypeStruct((t_all, d), F32),
        scratch_shapes=[pltpu.VMEM((2, 2 * tm, d), F32), pltpu.SemaphoreType.DMA((2,))],
        compiler_params=_cparams(("arbitrary",)),
        name="combine_ln",
    )(dest_tiles, dest_tiles, x1, gates_t, out_rows, g2, b2)


def _rope_table(pos):
    half = ROT_DIM // 2
    inv_freq = jnp.power(ROPE_THETA, -2.0 * jnp.arange(half, dtype=F32) / ROT_DIM)
    ang = pos.astype(F32)[:, None] * inv_freq[None, :]
    cos, sin = jnp.cos(ang), jnp.sin(ang)
    n = pos.shape[0]
    ones = jnp.ones((n, HEAD_DIM - ROT_DIM), F32)
    zeros = jnp.zeros((n, HEAD_DIM - ROT_DIM), F32)
    zh = jnp.zeros((n, half), F32)
    c = jnp.concatenate([cos, cos, ones], axis=1)
    s1 = jnp.concatenate([-sin, zh, zeros], axis=1)
    s2 = jnp.concatenate([zh, sin, zeros], axis=1)
    rep = LANES // HEAD_DIM
    return jnp.concatenate([jnp.tile(c, (1, rep)), jnp.tile(s1, (1, rep)), jnp.tile(s2, (1, rep))], axis=1)


def _dispatch_plan(meta_i, counts, n_blocks, tm):
    t_all = meta_i.shape[1]
    cnt = counts[:, 0].astype(jnp.int32)
    padded = (cnt + EXPERT_ROWS - 1) // EXPERT_ROWS * EXPERT_ROWS
    pend = jnp.cumsum(padded)
    pstart = pend - padded
    dest = pstart[meta_i[0:2]] + meta_i[2:4]
    n_used = (pend[-1] // EXPERT_ROWS).astype(jnp.int32)
    blk = jnp.arange(n_blocks, dtype=jnp.int32)
    blk = jnp.minimum(blk, n_used - 1)
    block_expert = jnp.clip(jnp.searchsorted(pend, blk * EXPERT_ROWS, side='right'),
                            0, N_EXPERTS - 1).astype(jnp.int32)
    tok = jnp.broadcast_to(jnp.arange(t_all, dtype=jnp.int32)[None, :], (2, t_all))
    src_tok = jnp.zeros((n_blocks * EXPERT_ROWS,), jnp.int32).at[dest.reshape(-1)].set(tok.reshape(-1))
    dest_tiles = dest.reshape(2, t_all // tm, tm).transpose(1, 0, 2).reshape(t_all // tm, 1, 2 * tm)
    return block_expert, n_used.reshape(1), src_tok, dest_tiles


def kernel(x_prompt, x_sample, cache_k, cache_v, w_in, w_o, attn_sinks, attn_out_g, gm_out_g,
           gm_ln_g, gm_ln_b, gm_ws, gm_bs, ln1_g, ln1_b, ln2_g, ln2_b, router_w, router_b,
           w_gate, w_up, w_down):
    depth = w_in.shape[0]
    batch, seq, d = x_prompt.shape
    dbatch, ds, _ = x_sample.shape
    alpha = (2 * depth) ** 0.25
    t_p, t_s = batch * seq, dbatch * ds
    t_all = t_p + t_s
    tm = 256 if (t_p % 256 == 0 and t_s % 256 == 0) else 128
    tq = 512 if seq % 512 == 0 else seq
    assert seq % GM_CHUNK == 0 and seq % tq == 0 and t_all % tm == 0 and ds <= CHUNK
    gm_tile_p = 4 * GM_CHUNK if seq % (4 * GM_CHUNK) == 0 else GM_CHUNK
    assert t_p % (dbatch * ds) == 0
    n_blocks = -(-2 * t_all // EXPERT_ROWS) + N_EXPERTS

    x = jnp.concatenate([x_prompt.reshape(t_p, d), x_sample.reshape(t_s, d)], axis=0)
    rope_tab = jnp.concatenate(
        [jnp.tile(_rope_table(jnp.arange(seq)), (batch, 1)),
         jnp.tile(_rope_table(PAST_LEN + jnp.arange(ds)), (dbatch, 1))], axis=0)
    rwt = router_w.T.astype(BF16)
    rb = jnp.broadcast_to(router_b.astype(F32)[:, None], (N_EXPERTS, LANES))
    cache_k = cache_k.reshape(depth, dbatch, WINDOW, KV_WIDTH)
    cache_v = cache_v.reshape(depth, dbatch, WINDOW, KV_WIDTH)

    kp_out, vp_out, ks_out, vs_out, gv_out = [], [], [], [], []
    for l in range(depth):
        q, k32, v32, kp, vp, gu, vn = _inproj(
            x, w_in[l].astype(BF16), rope_tab,
            gm_ln_g[l].reshape(1, GM_WIDTH), gm_ln_b[l].reshape(1, GM_WIDTH), tm)
        g_attn = attn_out_g[l].reshape(1, ATTN_WIDTH)
        g_gm = gm_out_g[l].reshape(1, GM_WIDTH)
        attn_p = _attn_prompt(attn_sinks[l], q, kp, vp, g_attn, batch, seq, tq)
        attn_s = _attn_sample(attn_sinks[l], q, k32, v32, cache_k[l], cache_v[l], g_attn, t_p, dbatch, ds)
        bias_full = jnp.repeat(gm_bs[l].T, GM_GROUP_DIM, axis=1)
        gm_p = _gmlp(gu, vn, gm_ws[l], bias_full, g_gm, 0, t_p, GM_CHUNK, gm_tile_p // GM_CHUNK)
        gm_s = _gmlp(gu, vn, gm_ws[l], bias_full, g_gm, t_p, t_s, ds, dbatch)
        attn_n = jnp.concatenate([attn_p, attn_s], axis=0)
        gm_n = jnp.concatenate([gm_p, gm_s], axis=0)

        x1, meta_i, meta_f, counts = _outproj(
            attn_n, gm_n, x, w_o[l].astype(BF16), ln1_g[l].reshape(1, d), ln1_b[l].reshape(1, d),
            rwt, rb, alpha, tm)
        block_expert, n_used, src_tok, dest_tiles = _dispatch_plan(meta_i, counts, n_blocks, tm)
        out_rows = _experts(block_expert, n_used, src_tok, x1, w_gate[l].astype(BF16),
                            w_up[l].astype(BF16), w_down[l].astype(BF16), n_blocks)
        x = _combine(dest_tiles, x1, meta_f[0:2].T, out_rows, ln2_g[l].reshape(1, d),
                     ln2_b[l].reshape(1, d), alpha, tm)

        k_p = k32[:t_p].reshape(batch, seq, N_KV_HEADS, HEAD_DIM)
        v_p = v32[:t_p].reshape(batch, seq, N_KV_HEADS, HEAD_DIM)
        kp_out.append(k_p[:, seq - WINDOW:])
        vp_out.append(v_p[:, seq - WINDOW:])
        k_s = k32[t_p:].reshape(dbatch, ds, KV_WIDTH)
        v_s = v32[t_p:].reshape(dbatch, ds, KV_WIDTH)
        ks_out.append(jnp.concatenate([cache_k[l][:, ds:], k_s], axis=1)
                      .reshape(dbatch, WINDOW, N_KV_HEADS, HEAD_DIM))
        vs_out.append(jnp.concatenate([cache_v[l][:, ds:], v_s], axis=1)
                      .reshape(dbatch, WINDOW, N_KV_HEADS, HEAD_DIM))
        gv_out.append(vn[t_p:].reshape(dbatch, ds, GM_GROUPS, GM_GROUP_DIM))

    y_prompt = x[:t_p].reshape(batch, seq, d)
    y_sample = x[t_p:].reshape(dbatch, ds, d)
    return (y_prompt, y_sample, jnp.stack(kp_out), jnp.stack(vp_out), jnp.stack(ks_out),
            jnp.stack(vs_out), jnp.stack(gv_out))
```

```python
import functools

import jax
import jax.numpy as jnp
import numpy as np
from jax import lax
from jax.experimental import pallas as pl
from jax.experimental.pallas import tpu as pltpu

F32 = jnp.float32
BF16 = jnp.bfloat16

CHUNK = 64
N_HEADS = 16
N_KV_HEADS = 4
HEAD_DIM = 64
Q_PER_KV = N_HEADS // N_KV_HEADS
ATTN_WIDTH = N_HEADS * HEAD_DIM
KV_WIDTH = N_KV_HEADS * HEAD_DIM
WINDOW = 128
ROT_DIM = HEAD_DIM // 4
ROPE_THETA = 500000.0
GM_CHUNK = 128
GM_GROUPS = 8
GM_GROUP_DIM = 128
GM_WIDTH = GM_GROUPS * GM_GROUP_DIM
N_EXPERTS = 16
N_EXPERT_GROUPS = 4
EXPERTS_PER_GROUP = N_EXPERTS // N_EXPERT_GROUPS
NORM_EPS = 1e-5
NEG_INF = -1e30
PAST_LEN = 4096

LANES = 128
KEY_BLOCK = 256
Q_BLOCK = 128
EXPERT_ROWS = 256
VMEM_LIMIT = 56 * 1024 * 1024


def _cparams(sem):
    return pltpu.CompilerParams(dimension_semantics=sem, vmem_limit_bytes=VMEM_LIMIT)


def _resident(shape, index_map):
    return pl.BlockSpec(shape, index_map, pipeline_mode=pl.Buffered(1))


def _gelu(a):
    return 0.5 * a * (1.0 + lax.erf(a * np.float32(0.7071067811865476)))


def _pair_expand(t):
    r = pltpu.roll(t, HEAD_DIM, 1)
    lane = lax.broadcasted_iota(jnp.int32, t.shape, 1)
    lo = lane < HEAD_DIM
    return jnp.where(lo, t, r), jnp.where(lo, r, t)


def _expand_kv(a):
    a0, b0 = _pair_expand(a[:, 0:LANES])
    a1, b1 = _pair_expand(a[:, LANES:2 * LANES])
    return jnp.concatenate([a0, b0, a1, b1], axis=1)


def _inproj_body(x_ref, w_ref, rope_ref, lng_ref, lnb_ref,
                 q_ref, k_ref, v_ref, kp_ref, vp_ref, gu_ref, vn_ref):
    xb = x_ref[...].astype(BF16)
    cs = rope_ref[:, 0:LANES]
    s1 = rope_ref[:, LANES:2 * LANES]
    s2 = rope_ref[:, 2 * LANES:3 * LANES]

    def rope(a):
        n = a.shape[1] // LANES
        width = a.shape[1]
        c = jnp.concatenate([cs] * n, axis=1)
        m1 = jnp.concatenate([s1] * n, axis=1)
        m2 = jnp.concatenate([s2] * n, axis=1)
        return a * c + pltpu.roll(a, width - ROT_DIM // 2, 1) * m1 + pltpu.roll(a, ROT_DIM // 2, 1) * m2

    def proj(c0, c1):
        return jnp.dot(xb, w_ref[:, c0:c1], preferred_element_type=F32)

    half = ATTN_WIDTH // 2
    for h in range(2):
        q = rope(proj(h * half, (h + 1) * half)) * np.float32(HEAD_DIM ** -0.5)
        q_ref[:, h * half:(h + 1) * half] = q.astype(BF16)

    kv = proj(ATTN_WIDTH, ATTN_WIDTH + 2 * KV_WIDTH)
    k = rope(kv[:, 0:KV_WIDTH])
    v = kv[:, KV_WIDTH:2 * KV_WIDTH]
    k_ref[...] = k
    v_ref[...] = v
    kp_ref[...] = _expand_kv(k).astype(BF16)
    vp_ref[...] = _expand_kv(v).astype(BF16)

    base = ATTN_WIDTH + 2 * KV_WIDTH
    for h in range(2):
        gu_ref[:, h * half:(h + 1) * half] = _gelu(proj(base + h * half, base + (h + 1) * half))

    base = base + GM_WIDTH
    for h in range(2):
        gv = _gelu(proj(base + h * half, base + (h + 1) * half))
        for g in range(half // GM_GROUP_DIM):
            t = gv[:, g * GM_GROUP_DIM:(g + 1) * GM_GROUP_DIM]
            col = h * half + g * GM_GROUP_DIM
            tc = t - jnp.mean(t, axis=-1, keepdims=True)
            var = jnp.mean(tc * tc, axis=-1, keepdims=True)
            y = tc * lax.rsqrt(var + NORM_EPS) * lng_ref[:, col:col + GM_GROUP_DIM] \
                + lnb_ref[:, col:col + GM_GROUP_DIM]
            vn_ref[:, col:col + GM_GROUP_DIM] = y


def _inproj(x, w, rope_tab, lng, lnb, tm):
    t_all, d = x.shape
    n_in = w.shape[1]
    row = lambda i: (i, 0)
    const = lambda i: (0, 0)
    out_shape = (
        jax.ShapeDtypeStruct((t_all, ATTN_WIDTH), BF16),
        jax.ShapeDtypeStruct((t_all, KV_WIDTH), F32),
        jax.ShapeDtypeStruct((t_all, KV_WIDTH), F32),
        jax.ShapeDtypeStruct((t_all, 2 * KV_WIDTH), BF16),
        jax.ShapeDtypeStruct((t_all, 2 * KV_WIDTH), BF16),
        jax.ShapeDtypeStruct((t_all, GM_WIDTH), F32),
        jax.ShapeDtypeStruct((t_all, GM_WIDTH), F32),
    )
    return pl.pallas_call(
        _inproj_body,
        grid=(t_all // tm,),
        in_specs=[
            pl.BlockSpec((tm, d), row),
            _resident((d, n_in), const),
            pl.BlockSpec((tm, 3 * LANES), row),
            _resident((1, GM_WIDTH), const),
            _resident((1, GM_WIDTH), const),
        ],
        out_specs=(
            pl.BlockSpec((tm, ATTN_WIDTH), row),
            pl.BlockSpec((tm, KV_WIDTH), row),
            pl.BlockSpec((tm, KV_WIDTH), row),
            pl.BlockSpec((tm, 2 * KV_WIDTH), row),
            pl.BlockSpec((tm, 2 * KV_WIDTH), row),
            pl.BlockSpec((tm, GM_WIDTH), row),
            pl.BlockSpec((tm, GM_WIDTH), row),
        ),
        out_shape=out_shape,
        compiler_params=_cparams(("parallel",)),
        name="inproj",
    )(x, w, rope_tab, lng, lnb)


def _attn_group(qg, kexp, vexp, bias, sinks):
    nk = kexp.shape[0]
    seg = lax.broadcasted_iota(jnp.int32, kexp.shape, 1) // HEAD_DIM
    zero = jnp.zeros_like(kexp)
    kbd = jnp.concatenate([jnp.where(seg == j, kexp, zero) for j in range(Q_PER_KV)], axis=0)
    vbd = jnp.concatenate([jnp.where(seg == j, vexp, zero) for j in range(Q_PER_KV)], axis=0)
    s = lax.dot_general(qg, kbd, (((1,), (1,)), ((), ())), preferred_element_type=F32)
    es, rs = [], []
    for j in range(Q_PER_KV):
        sj = s[:, j * nk:(j + 1) * nk] + bias
        m = jnp.maximum(jnp.max(sj, axis=-1, keepdims=True), sinks[j])
        e = jnp.exp(sj - m)
        den = jnp.sum(e, axis=-1, keepdims=True) + jnp.exp(sinks[j] - m)
        es.append(e.astype(BF16))
        rs.append(1.0 / den)
    p = jnp.concatenate(es, axis=1)
    o = jnp.dot(p, vbd, preferred_element_type=F32)
    oseg = lax.broadcasted_iota(jnp.int32, o.shape, 1) // HEAD_DIM
    r = jnp.where(oseg == 0, rs[0], jnp.where(oseg == 1, rs[1], jnp.where(oseg == 2, rs[2], rs[3])))
    return o * r


def _attn_rows(q, kbuf, vbuf, bias, sink_ref, g_row):
    outs = []
    for h in range(N_KV_HEADS):
        kt = kbuf[:, h * LANES:(h + 1) * LANES]
        vt = vbuf[:, h * LANES:(h + 1) * LANES]
        kexp = jnp.concatenate([kt, kt], axis=1)
        vexp = jnp.concatenate([vt, vt], axis=1)
        sinks = [sink_ref[h * Q_PER_KV + j] for j in range(Q_PER_KV)]
        width = Q_PER_KV * HEAD_DIM
        outs.append(_attn_group(q[:, h * width:(h + 1) * width], kexp, vexp, bias, sinks))
    o = jnp.concatenate(outs, axis=1)
    ms = jnp.mean(o * o, axis=-1, keepdims=True)
    return o * lax.rsqrt(ms + NORM_EPS) * g_row


def _attn_prompt_body(sink_ref, q_ref, kc_ref, kprev_ref, vc_ref, vprev_ref, g_ref, o_ref,
                      kbuf, vbuf, *, tq):
    i = pl.program_id(1)
    kbuf[0:WINDOW, :] = kprev_ref[...]
    kbuf[WINDOW:WINDOW + tq, :] = kc_ref[...]
    vbuf[0:WINDOW, :] = vprev_ref[...]
    vbuf[WINDOW:WINDOW + tq, :] = vc_ref[...]
    rq = lax.broadcasted_iota(jnp.int32, (Q_BLOCK, KEY_BLOCK), 0) // CHUNK
    ck = lax.broadcasted_iota(jnp.int32, (Q_BLOCK, KEY_BLOCK), 1) // CHUNK
    rel = ck - rq
    band = (rel >= 0) & (rel <= WINDOW // CHUNK)
    g_row = g_ref[...]

    def sub(s, carry):
        r0 = pl.multiple_of(s * Q_BLOCK, Q_BLOCK)
        first = jnp.logical_and(i == 0, s == 0)
        kmin = jnp.where(first, WINDOW // CHUNK, 0)
        ok = jnp.logical_and(band, ck >= kmin)
        bias = jnp.where(ok, np.float32(0.0), np.float32(NEG_INF))
        q = q_ref[pl.ds(r0, Q_BLOCK), :]
        kb = kbuf[pl.ds(r0, KEY_BLOCK), :]
        vb = vbuf[pl.ds(r0, KEY_BLOCK), :]
        o_ref[pl.ds(r0, Q_BLOCK), :] = _attn_rows(q, kb, vb, bias, sink_ref, g_row).astype(BF16)
        return carry

    lax.fori_loop(0, tq // Q_BLOCK, sub, 0)


def _attn_prompt(sinks, q, kp, vp, g_attn, batch, seq, tq):
    nq = seq // tq
    per = tq // WINDOW
    cur = lambda b, i: (b * nq + i, 0)
    prev = lambda b, i: (b * (seq // WINDOW) + jnp.maximum(i * per - 1, 0), 0)
    kernel = functools.partial(_attn_prompt_body, tq=tq)
    return pl.pallas_call(
        kernel,
        grid=(batch, nq),
        in_specs=[
            pl.BlockSpec(memory_space=pltpu.SMEM),
            pl.BlockSpec((tq, ATTN_WIDTH), cur),
            pl.BlockSpec((tq, 2 * KV_WIDTH), cur),
            pl.BlockSpec((WINDOW, 2 * KV_WIDTH), prev),
            pl.BlockSpec((tq, 2 * KV_WIDTH), cur),
            pl.BlockSpec((WINDOW, 2 * KV_WIDTH), prev),
            pl.BlockSpec((1, ATTN_WIDTH), lambda b, i: (0, 0)),
        ],
        out_specs=pl.BlockSpec((tq, ATTN_WIDTH), cur),
        out_shape=jax.ShapeDtypeStruct((batch * seq, ATTN_WIDTH), BF16),
        scratch_shapes=[pltpu.VMEM((WINDOW + tq, 2 * KV_WIDTH), BF16),
                        pltpu.VMEM((WINDOW + tq, 2 * KV_WIDTH), BF16)],
        compiler_params=_cparams(("parallel", "parallel")),
        name="attn_prompt",
    )(sinks, q, kp, kp, vp, vp, g_attn)


def _attn_sample_body(sink_ref, q_ref, ck_ref, cv_ref, kn_ref, vn_ref, g_ref, o_ref, *, ds):
    pad = jnp.zeros((KEY_BLOCK - WINDOW - ds, KV_WIDTH), F32)
    k_all = jnp.concatenate([ck_ref[0], kn_ref[...], pad], axis=0)
    v_all = jnp.concatenate([cv_ref[0], vn_ref[...], pad], axis=0)
    kb = _expand_kv(k_all).astype(BF16)
    vb = _expand_kv(v_all).astype(BF16)
    col = lax.broadcasted_iota(jnp.int32, (ds, KEY_BLOCK), 1)
    bias = jnp.where(col < WINDOW + ds, np.float32(0.0), np.float32(NEG_INF))
    o_ref[...] = _attn_rows(q_ref[...], kb, vb, bias, sink_ref, g_ref[...]).astype(BF16)


def _attn_sample(sinks, q, k32, v32, cache_k, cache_v, g_attn, row0, dbatch, ds):
    off = row0 // ds
    new = lambda b: (off + b, 0)
    kernel = functools.partial(_attn_sample_body, ds=ds)
    return pl.pallas_call(
        kernel,
        grid=(dbatch,),
        in_specs=[
            pl.BlockSpec(memory_space=pltpu.SMEM),
            pl.BlockSpec((ds, ATTN_WIDTH), new),
            pl.BlockSpec((1, WINDOW, KV_WIDTH), lambda b: (b, 0, 0)),
            pl.BlockSpec((1, WINDOW, KV_WIDTH), lambda b: (b, 0, 0)),
            pl.BlockSpec((ds, KV_WIDTH), new),
            pl.BlockSpec((ds, KV_WIDTH), new),
            pl.BlockSpec((1, ATTN_WIDTH), lambda b: (0, 0)),
        ],
        out_specs=pl.BlockSpec((ds, ATTN_WIDTH), lambda b: (b, 0)),
        out_shape=jax.ShapeDtypeStruct((dbatch * ds, ATTN_WIDTH), BF16),
        compiler_params=_cparams(("parallel",)),
        name="attn_sample",
    )(sinks, q, cache_k, cache_v, k32, v32, g_attn)


def _gmlp_body(gu_ref, vn_ref, ws_ref, bias_ref, g_ref, o_ref, gm_scr, *, rows, nchunk):
    ci = lax.broadcasted_iota(jnp.int32, (GM_CHUNK, GM_CHUNK), 0) // CHUNK
    cj = lax.broadcasted_iota(jnp.int32, (GM_CHUNK, GM_CHUNK), 1) // CHUNK
    visible = ci >= cj
    for g in range(GM_GROUPS):
        lo, hi = g * GM_GROUP_DIM, (g + 1) * GM_GROUP_DIM
        w = jnp.where(visible, ws_ref[g], np.float32(0.0))[0:rows, :].astype(BF16)
        pieces = []
        for n in range(nchunk):
            piece = vn_ref[n * rows:(n + 1) * rows, lo:hi]
            if rows < GM_CHUNK:
                piece = jnp.concatenate([piece, jnp.zeros((GM_CHUNK - rows, GM_GROUP_DIM), F32)], axis=0)
            pieces.append(piece.astype(BF16))
        rhs = jnp.concatenate(pieces, axis=1)
        mix = jnp.dot(w, rhs, preferred_element_type=F32)
        b = bias_ref[0:rows, lo:hi]
        for n in range(nchunk):
            gm_scr[n * rows:(n + 1) * rows, lo:hi] = \
                gu_ref[n * rows:(n + 1) * rows, lo:hi] * (mix[:, n * GM_GROUP_DIM:(n + 1) * GM_GROUP_DIM] + b)
    gm = gm_scr[...]
    ms = jnp.mean(gm * gm, axis=-1, keepdims=True)
    o_ref[...] = (gm * lax.rsqrt(ms + NORM_EPS) * g_ref[...]).astype(BF16)


def _gmlp(gu, vn, ws, bias_full, g_gm, row0, n_rows, rows, nchunk):
    tile = rows * nchunk
    off = row0 // tile
    kernel = functools.partial(_gmlp_body, rows=rows, nchunk=nchunk)
    return pl.pallas_call(
        kernel,
        grid=(n_rows // tile,),
        in_specs=[
            pl.BlockSpec((tile, GM_WIDTH), lambda i: (off + i, 0)),
            pl.BlockSpec((tile, GM_WIDTH), lambda i: (off + i, 0)),
            pl.BlockSpec((GM_GROUPS, GM_CHUNK, GM_CHUNK), lambda i: (0, 0, 0)),
            pl.BlockSpec((GM_CHUNK, GM_WIDTH), lambda i: (0, 0)),
            pl.BlockSpec((1, GM_WIDTH), lambda i: (0, 0)),
        ],
        out_specs=pl.BlockSpec((tile, GM_WIDTH), lambda i: (i, 0)),
        out_shape=jax.ShapeDtypeStruct((n_rows, GM_WIDTH), BF16),
        scratch_shapes=[pltpu.VMEM((tile, GM_WIDTH), F32)],
        compiler_params=_cparams(("parallel",)),
        name="gmlp",
    )(gu, vn, ws, bias_full, g_gm)


def _layer_norm_rows(z, g, b):
    zc = z - jnp.mean(z, axis=-1, keepdims=True)
    var = jnp.mean(zc * zc, axis=-1, keepdims=True)
    return zc * lax.rsqrt(var + NORM_EPS) * g + b


def _tree_sum(items):
    while len(items) > 1:
        items = [a + b for a, b in zip(items[0::2], items[1::2])] + (items[-1:] if len(items) % 2 else [])
    return items[0]


def _first_max(items):
    while len(items) > 1:
        nxt = [tuple(jnp.where(a[0] >= b[0], x, y) for x, y in zip(a, b))
               for a, b in zip(items[0::2], items[1::2])]
        items = nxt + (items[-1:] if len(items) % 2 else [])
    return items[0]


def _route_rows(lg, rb):
    sc = 1.0 / (1.0 + jnp.exp(-lg))
    sel = sc + rb
    rows = [sel[e:e + 1, :] for e in range(N_EXPERTS)]
    scr = [sc[e:e + 1, :] for e in range(N_EXPERTS)]
    groups = []
    for g in range(N_EXPERT_GROUPS):
        a, b, c, d = rows[g * EXPERTS_PER_GROUP:(g + 1) * EXPERTS_PER_GROUP]
        hi1, lo1 = jnp.maximum(a, b), jnp.minimum(a, b)
        hi2, lo2 = jnp.maximum(c, d), jnp.minimum(c, d)
        top1 = jnp.maximum(hi1, hi2)
        top2 = jnp.maximum(jnp.minimum(hi1, hi2), jnp.maximum(lo1, lo2))
        members = tuple(rows[g * EXPERTS_PER_GROUP + j] for j in range(EXPERTS_PER_GROUP))
        scores = tuple(scr[g * EXPERTS_PER_GROUP + j] for j in range(EXPERTS_PER_GROUP))
        groups.append((top1 + top2, jnp.full_like(top1, g * EXPERTS_PER_GROUP)) + members + scores)
    best = _first_max(groups)
    first_id = best[1]
    cand = [(best[2 + j], best[2 + EXPERTS_PER_GROUP + j], first_id + np.float32(j))
            for j in range(EXPERTS_PER_GROUP)]
    _, g0, idx0 = _first_max(cand)
    neg = np.float32(-np.inf)
    cand2 = [(jnp.where(c[2] == idx0, neg, c[0]), c[1], c[2]) for c in cand]
    _, g1, idx1 = _first_max(cand2)
    tot = g0 + g1
    return idx0, idx1, g0 / tot, g1 / tot


def _outproj_body(an_ref, gn_ref, x_ref, wo_ref, g1_ref, b1_ref, rwt_ref, rb_ref,
                  x1_ref, mi_ref, mf_ref, cnt_ref, carry, *, alpha):
    i = pl.program_id(0)
    tm = x_ref.shape[0]

    @pl.when(i == 0)
    def _():
        carry[...] = jnp.zeros_like(carry)

    y = jnp.dot(an_ref[...], wo_ref[0:ATTN_WIDTH, :], preferred_element_type=F32)
    y = y + jnp.dot(gn_ref[...], wo_ref[ATTN_WIDTH:ATTN_WIDTH + GM_WIDTH, :], preferred_element_type=F32)
    x1 = _layer_norm_rows(np.float32(alpha) * x_ref[...] + y, g1_ref[...], b1_ref[...])
    x1_ref[...] = x1

    lg = lax.dot_general(rwt_ref[...], x1.astype(BF16), (((1,), (1,)), ((), ())),
                         preferred_element_type=F32)
    rb = jnp.concatenate([rb_ref[...]] * (tm // LANES), axis=1)
    idx0, idx1, gate0, gate1 = _route_rows(lg, rb)

    one, zero = np.float32(1.0), np.float32(0.0)
    hit0 = [idx0 == np.float32(e) for e in range(N_EXPERTS)]
    hit1 = [idx1 == np.float32(e) for e in range(N_EXPERTS)]
    chosen = jnp.concatenate(
        [jnp.where(jnp.logical_or(hit0[e], hit1[e]), one, zero) for e in range(N_EXPERTS)], axis=0)
    s_idx = lax.broadcasted_iota(jnp.int32, (tm, tm), 0)
    t_idx = lax.broadcasted_iota(jnp.int32, (tm, tm), 1)
    before = jnp.where(s_idx < t_idx, one, zero).astype(BF16)
    base = carry[...]
    rank = jnp.dot(chosen.astype(BF16), before, preferred_element_type=F32) \
        + jnp.concatenate([base] * (tm // LANES), axis=1)
    carry[...] = base + jnp.sum(chosen, axis=1, keepdims=True)
    cnt_ref[...] = carry[...]

    rank_rows = [rank[e:e + 1, :] for e in range(N_EXPERTS)]
    rank0 = _tree_sum([jnp.where(hit0[e], rank_rows[e], zero) for e in range(N_EXPERTS)])
    rank1 = _tree_sum([jnp.where(hit1[e], rank_rows[e], zero) for e in range(N_EXPERTS)])
    zr = jnp.zeros((1, tm), F32)
    mi_ref[...] = jnp.concatenate([idx0, idx1, rank0, rank1, zr, zr, zr, zr], axis=0).astype(jnp.int32)
    mf_ref[...] = jnp.concatenate([gate0, gate1, zr, zr, zr, zr, zr, zr], axis=0)


def _outproj(attn_n, gm_n, x, wo, g1, b1, rwt, rb, alpha, tm):
    t_all, d = x.shape
    row = lambda i: (i, 0)
    col = lambda i: (0, i)
    const = lambda i: (0, 0)
    kernel = functools.partial(_outproj_body, alpha=alpha)
    return pl.pallas_call(
        kernel,
        grid=(t_all // tm,),
        in_specs=[
            pl.BlockSpec((tm, ATTN_WIDTH), row),
            pl.BlockSpec((tm, GM_WIDTH), row),
            pl.BlockSpec((tm, d), row),
            _resident((ATTN_WIDTH + GM_WIDTH, d), const),
            _resident((1, d), const),
            _resident((1, d), const),
            _resident((N_EXPERTS, d), const),
            _resident((N_EXPERTS, LANES), const),
        ],
        out_specs=(
            pl.BlockSpec((tm, d), row),
            pl.BlockSpec((8, tm), col),
            pl.BlockSpec((8, tm), col),
            pl.BlockSpec((N_EXPERTS, LANES), const),
        ),
        out_shape=(
            jax.ShapeDtypeStruct((t_all, d), F32),
            jax.ShapeDtypeStruct((8, t_all), jnp.int32),
            jax.ShapeDtypeStruct((8, t_all), F32),
            jax.ShapeDtypeStruct((N_EXPERTS, LANES), F32),
        ),
        scratch_shapes=[pltpu.VMEM((N_EXPERTS, LANES), F32)],
        compiler_params=_cparams(("arbitrary",)),
        name="outproj_router",
    )(attn_n, gm_n, x, wo, g1, b1, rwt, rb)


def _dispatch_body(pz_ref, pe_ref, nu_ref, dest_ref, x1_hbm, rows_hbm, zbuf, sem, zsem, *, tb, n_blocks):
    i = pl.program_id(0)
    n = pl.num_programs(0)
    slot = i % 2
    base = i * tb

    def issue(r, c):
        for k in (2, 3):
            pltpu.make_async_copy(x1_hbm.at[pl.ds(base + r, 1), :],
                                  rows_hbm.at[pl.ds(dest_ref[k, r], 1), :], sem.at[slot]).start()
        return c

    lax.fori_loop(0, tb, issue, 0, unroll=8)

    def wait_tile(s):
        for _ in range(2):
            pltpu.make_async_copy(x1_hbm.at[pl.ds(0, tb), :], rows_hbm.at[pl.ds(0, tb), :], sem.at[s]).wait()

    @pl.when(i > 0)
    def _():
        wait_tile(1 - slot)

    @pl.when(i == n - 1)
    def _():
        wait_tile(slot)
        zbuf[...] = jnp.zeros_like(zbuf)

        def row_copy(r):
            return pltpu.make_async_copy(zbuf.at[pl.ds(0, 1), :], rows_hbm.at[pl.ds(r, 1), :], zsem.at[0])

        def block_copy(j):
            r0 = pl.multiple_of(j * EXPERT_ROWS, EXPERT_ROWS)
            return pltpu.make_async_copy(zbuf, rows_hbm.at[pl.ds(r0, EXPERT_ROWS), :], zsem.at[0])

        for e in range(N_EXPERTS):
            lax.fori_loop(pz_ref[e], pe_ref[e], lambda r, c: (row_copy(r).start(), c)[1], 0)
        lax.fori_loop(nu_ref[0], n_blocks, lambda j, c: (block_copy(j).start(), c)[1], 0)
        for e in range(N_EXPERTS):
            lax.fori_loop(pz_ref[e], pe_ref[e], lambda r, c: (row_copy(r).wait(), c)[1], 0)
        lax.fori_loop(nu_ref[0], n_blocks, lambda j, c: (block_copy(j).wait(), c)[1], 0)


def _dispatch(pad_lo, pad_hi, n_used, dest, x1, n_blocks, tb):
    t_all, d = x1.shape
    kernel = functools.partial(_dispatch_body, tb=tb, n_blocks=n_blocks)
    grid_spec = pltpu.PrefetchScalarGridSpec(
        num_scalar_prefetch=3,
        grid=(t_all // tb,),
        in_specs=[
            pl.BlockSpec((8, tb), lambda i, pz, pe, nu: (0, i), memory_space=pltpu.SMEM),
            pl.BlockSpec(memory_space=pl.ANY),
        ],
        out_specs=pl.BlockSpec(memory_space=pl.ANY),
        scratch_shapes=[pltpu.VMEM((EXPERT_ROWS, d), F32), pltpu.SemaphoreType.DMA((2,)),
                        pltpu.SemaphoreType.DMA((1,))],
    )
    return pl.pallas_call(
        kernel,
        grid_spec=grid_spec,
        out_shape=jax.ShapeDtypeStruct((n_blocks * EXPERT_ROWS, d), F32),
        compiler_params=_cparams(("arbitrary",)),
        name="dispatch",
    )(pad_lo, pad_hi, n_used, dest, x1)


def _expert_body(be_ref, nu_ref, x_ref, wg_ref, wu_ref, wd_ref, o_ref):
    j = pl.program_id(0)

    @pl.when(j < nu_ref[0])
    def _():
        xb = x_ref[...].astype(BF16)
        gate = jnp.dot(xb, wg_ref[0], preferred_element_type=F32)
        up = jnp.dot(xb, wu_ref[0], preferred_element_type=F32)
        h = gate * (1.0 / (1.0 + jnp.exp(-gate))) * up
        o_ref[...] = jnp.dot(h.astype(BF16), wd_ref[0], preferred_element_type=F32)

    @pl.when(j >= nu_ref[0])
    def _():
        o_ref[...] = jnp.zeros_like(o_ref)


def _experts(block_expert, n_used, rows, wg, wu, wd, n_blocks):
    d = rows.shape[1]
    de = wg.shape[2]
    grid_spec = pltpu.PrefetchScalarGridSpec(
        num_scalar_prefetch=2,
        grid=(n_blocks,),
        in_specs=[
            pl.BlockSpec((EXPERT_ROWS, d), lambda j, be, nu: (jnp.minimum(j, nu[0] - 1), 0)),
            pl.BlockSpec((1, d, de), lambda j, be, nu: (be[j], 0, 0)),
            pl.BlockSpec((1, d, de), lambda j, be, nu: (be[j], 0, 0)),
            pl.BlockSpec((1, de, d), lambda j, be, nu: (be[j], 0, 0)),
        ],
        out_specs=pl.BlockSpec((EXPERT_ROWS, d), lambda j, be, nu: (j, 0)),
    )
    return pl.pallas_call(
        _expert_body,
        grid_spec=grid_spec,
        out_shape=jax.ShapeDtypeStruct(rows.shape, F32),
        compiler_params=_cparams(("arbitrary",)),
        name="experts",
    )(block_expert, n_used, rows, wg, wu, wd)


def _combine_body(d0_ref, dn_ref, x1_ref, gt_ref, rows_hbm, g2_ref, b2_ref, x2_ref, rbuf, sem, *, alpha):
    i = pl.program_id(0)
    n = pl.num_programs(0)
    tm = x1_ref.shape[0]
    slot = i % 2

    def gather(idx_ref, s):
        def body(r, c):
            for k in range(2):
                pltpu.make_async_copy(rows_hbm.at[pl.ds(idx_ref[2 + k, r], 1), :],
                                      rbuf.at[s, k].at[pl.ds(r, 1), :], sem.at[s]).start()
            return c
        lax.fori_loop(0, tm, body, 0, unroll=8)

    @pl.when(i == 0)
    def _():
        gather(d0_ref, 0)

    @pl.when(i + 1 < n)
    def _():
        gather(dn_ref, 1 - slot)

    for k in range(2):
        pltpu.make_async_copy(rows_hbm.at[pl.ds(0, tm), :], rbuf.at[slot, k], sem.at[slot]).wait()
    g0 = gt_ref[:, 0:1]
    g1 = gt_ref[:, 1:2]
    y = rbuf[slot, 0] * g0 + rbuf[slot, 1] * g1
    x2_ref[...] = _layer_norm_rows(np.float32(alpha) * x1_ref[...] + y, g2_ref[...], b2_ref[...])


def _combine(meta_i, x1, gates_t, out_rows, g2, b2, alpha, tm):
    t_all, d = x1.shape
    n_tiles = t_all // tm
    kernel = functools.partial(_combine_body, alpha=alpha)
    return pl.pallas_call(
        kernel,
        grid=(n_tiles,),
        in_specs=[
            pl.BlockSpec((8, tm), lambda i: (0, 0), memory_space=pltpu.SMEM),
            pl.BlockSpec((8, tm), lambda i: (0, jnp.minimum(i + 1, n_tiles - 1)), memory_space=pltpu.SMEM),
            pl.BlockSpec((tm, d), lambda i: (i, 0)),
            pl.BlockSpec((tm, 2), lambda i: (i, 0)),
            pl.BlockSpec(memory_space=pl.ANY),
            _resident((1, d), lambda i: (0, 0)),
            _resident((1, d), lambda i: (0, 0)),
        ],
        out_specs=pl.BlockSpec((tm, d), lambda i: (i, 0)),
        out_shape=jax.ShapeDtypeStruct((t_all, d), F32),
        scratch_shapes=[pltpu.VMEM((2, 2, tm, d), F32), pltpu.SemaphoreType.DMA((2,))],
        compiler_params=_cparams(("arbitrary",)),
        name="combine_ln",
    )(meta_i, meta_i, x1, gates_t, out_rows, g2, b2)


def _rope_table(pos):
    half = ROT_DIM // 2
    inv_freq = jnp.power(ROPE_THETA, -2.0 * jnp.arange(half, dtype=F32) / ROT_DIM)
    ang = pos.astype(F32)[:, None] * inv_freq[None, :]
    cos, sin = jnp.cos(ang), jnp.sin(ang)
    n = pos.shape[0]
    ones = jnp.ones((n, HEAD_DIM - ROT_DIM), F32)
    zeros = jnp.zeros((n, HEAD_DIM - ROT_DIM), F32)
    zh = jnp.zeros((n, half), F32)
    c = jnp.concatenate([cos, cos, ones], axis=1)
    s1 = jnp.concatenate([-sin, zh, zeros], axis=1)
    s2 = jnp.concatenate([zh, sin, zeros], axis=1)
    rep = LANES // HEAD_DIM
    return jnp.concatenate([jnp.tile(c, (1, rep)), jnp.tile(s1, (1, rep)), jnp.tile(s2, (1, rep))], axis=1)


def _block_plan(meta_i, counts, n_blocks):
    cnt = counts[:, 0].astype(jnp.int32)
    padded = (cnt + EXPERT_ROWS - 1) // EXPERT_ROWS * EXPERT_ROWS
    pend = jnp.cumsum(padded)
    pstart = pend - padded
    n_used = pend[-1] // EXPERT_ROWS
    blk = jnp.minimum(jnp.arange(n_blocks, dtype=jnp.int32), n_used - 1)
    block_expert = jnp.sum((blk[:, None] * EXPERT_ROWS >= pend[None, :]).astype(jnp.int32), axis=1)
    experts = jnp.arange(N_EXPERTS, dtype=jnp.int32)[None, :, None]
    start_of = jnp.sum(jnp.where(meta_i[0:2, None, :] == experts, pstart[None, :, None], 0), axis=1)
    dest = jnp.concatenate([meta_i[0:2], start_of + meta_i[2:4], meta_i[4:8]], axis=0)
    return block_expert, n_used.reshape(1), pstart + cnt, pend, dest


def kernel(x_prompt, x_sample, cache_k, cache_v, w_in, w_o, attn_sinks, attn_out_g, gm_out_g,
           gm_ln_g, gm_ln_b, gm_ws, gm_bs, ln1_g, ln1_b, ln2_g, ln2_b, router_w, router_b,
           w_gate, w_up, w_down):
    depth = w_in.shape[0]
    batch, seq, d = x_prompt.shape
    dbatch, ds, _ = x_sample.shape
    alpha = (2 * depth) ** 0.25
    t_p, t_s = batch * seq, dbatch * ds
    t_all = t_p + t_s
    tm = 256 if (t_p % 256 == 0 and t_s % 256 == 0) else 128
    tq = 512 if seq % 512 == 0 else seq
    assert seq % GM_CHUNK == 0 and seq % tq == 0 and t_all % tm == 0 and ds <= CHUNK
    gm_tile_p = 4 * GM_CHUNK if seq % (4 * GM_CHUNK) == 0 else GM_CHUNK
    assert t_p % (dbatch * ds) == 0
    n_blocks = -(-2 * t_all // EXPERT_ROWS) + N_EXPERTS

    x = jnp.concatenate([x_prompt.reshape(t_p, d), x_sample.reshape(t_s, d)], axis=0)
    rope_tab = jnp.concatenate(
        [jnp.tile(_rope_table(jnp.arange(seq)), (batch, 1)),
         jnp.tile(_rope_table(PAST_LEN + jnp.arange(ds)), (dbatch, 1))], axis=0)
    rwt = router_w.T.astype(BF16)
    rb = jnp.broadcast_to(router_b.astype(F32)[:, None], (N_EXPERTS, LANES))
    cache_k = cache_k.reshape(depth, dbatch, WINDOW, KV_WIDTH)
    cache_v = cache_v.reshape(depth, dbatch, WINDOW, KV_WIDTH)

    kp_out, vp_out, ks_out, vs_out, gv_out = [], [], [], [], []
    for l in range(depth):
        q, k32, v32, kp, vp, gu, vn = _inproj(
            x, w_in[l].astype(BF16), rope_tab,
            gm_ln_g[l].reshape(1, GM_WIDTH), gm_ln_b[l].reshape(1, GM_WIDTH), tm)
        g_attn = attn_out_g[l].reshape(1, ATTN_WIDTH)
        g_gm = gm_out_g[l].reshape(1, GM_WIDTH)
        attn_p = _attn_prompt(attn_sinks[l], q, kp, vp, g_attn, batch, seq, tq)
        attn_s = _attn_sample(attn_sinks[l], q, k32, v32, cache_k[l], cache_v[l], g_attn, t_p, dbatch, ds)
        bias_full = jnp.repeat(gm_bs[l].T, GM_GROUP_DIM, axis=1)
        gm_p = _gmlp(gu, vn, gm_ws[l], bias_full, g_gm, 0, t_p, GM_CHUNK, gm_tile_p // GM_CHUNK)
        gm_s = _gmlp(gu, vn, gm_ws[l], bias_full, g_gm, t_p, t_s, ds, dbatch)
        attn_n = jnp.concatenate([attn_p, attn_s], axis=0)
        gm_n = jnp.concatenate([gm_p, gm_s], axis=0)

        x1, meta_i, meta_f, counts = _outproj(
            attn_n, gm_n, x, w_o[l].astype(BF16), ln1_g[l].reshape(1, d), ln1_b[l].reshape(1, d),
            rwt, rb, alpha, tm)
        block_expert, n_used, pad_lo, pad_hi, dest = _block_plan(meta_i, counts, n_blocks)
        rows = _dispatch(pad_lo, pad_hi, n_used, dest, x1, n_blocks, tm)
        out_rows = _experts(block_expert, n_used, rows, w_gate[l].astype(BF16),
                            w_up[l].astype(BF16), w_down[l].astype(BF16), n_blocks)
        x = _combine(dest, x1, meta_f[0:2].T, out_rows, ln2_g[l].reshape(1, d),
                     ln2_b[l].reshape(1, d), alpha, tm)

        k_p = k32[:t_p].reshape(batch, seq, N_KV_HEADS, HEAD_DIM)
        v_p = v32[:t_p].reshape(batch, seq, N_KV_HEADS, HEAD_DIM)
        kp_out.append(k_p[:, seq - WINDOW:])
        vp_out.append(v_p[:, seq - WINDOW:])
        k_s = k32[t_p:].reshape(dbatch, ds, KV_WIDTH)
        v_s = v32[t_p:].reshape(dbatch, ds, KV_WIDTH)
        ks_out.append(jnp.concatenate([cache_k[l][:, ds:], k_s], axis=1)
                      .reshape(dbatch, WINDOW, N_KV_HEADS, HEAD_DIM))
        vs_out.append(jnp.concatenate([cache_v[l][:, ds:], v_s], axis=1)
                      .reshape(dbatch, WINDOW, N_KV_HEADS, HEAD_DIM))
        gv_out.append(vn[t_p:].reshape(dbatch, ds, GM_GROUPS, GM_GROUP_DIM))

    y_prompt = x[:t_p].reshape(batch, seq, d)
    y_sample = x[t_p:].reshape(dbatch, ds, d)
    return (y_prompt, y_sample, jnp.stack(kp_out), jnp.stack(vp_out), jnp.stack(ks_out),
            jnp.stack(vs_out), jnp.stack(gv_out))
```

```python
import functools

import jax
import jax.numpy as jnp
import numpy as np
from jax import lax
from jax.experimental import pallas as pl
from jax.experimental.pallas import tpu as pltpu

F32 = jnp.float32
BF16 = jnp.bfloat16

CHUNK = 64
N_HEADS = 16
N_KV_HEADS = 4
HEAD_DIM = 64
Q_PER_KV = N_HEADS // N_KV_HEADS
ATTN_WIDTH = N_HEADS * HEAD_DIM
KV_WIDTH = N_KV_HEADS * HEAD_DIM
WINDOW = 128
ROT_DIM = HEAD_DIM // 4
ROPE_THETA = 500000.0
GM_CHUNK = 128
GM_GROUPS = 8
GM_GROUP_DIM = 128
GM_WIDTH = GM_GROUPS * GM_GROUP_DIM
N_EXPERTS = 16
N_EXPERT_GROUPS = 4
EXPERTS_PER_GROUP = N_EXPERTS // N_EXPERT_GROUPS
NORM_EPS = 1e-5
NEG_INF = -1e30
PAST_LEN = 4096

LANES = 128
KEY_BLOCK = 256
Q_BLOCK = 128
EXPERT_ROWS = 256
VMEM_LIMIT = 56 * 1024 * 1024


def _cparams(sem):
    return pltpu.CompilerParams(dimension_semantics=sem, vmem_limit_bytes=VMEM_LIMIT)


def _resident(shape, index_map):
    return pl.BlockSpec(shape, index_map, pipeline_mode=pl.Buffered(1))


def _gelu(a):
    return 0.5 * a * (1.0 + lax.erf(a * np.float32(0.7071067811865476)))


def _pair_expand(t):
    r = pltpu.roll(t, HEAD_DIM, 1)
    lane = lax.broadcasted_iota(jnp.int32, t.shape, 1)
    lo = lane < HEAD_DIM
    return jnp.where(lo, t, r), jnp.where(lo, r, t)


def _expand_kv(a):
    a0, b0 = _pair_expand(a[:, 0:LANES])
    a1, b1 = _pair_expand(a[:, LANES:2 * LANES])
    return jnp.concatenate([a0, b0, a1, b1], axis=1)


def _pick_rows(first_ref, second_ref, n_first):
    return jnp.where(pl.program_id(0) < n_first, first_ref[...], second_ref[...])


def _split_specs(block, n_first):
    return [pl.BlockSpec(block, lambda i: (jnp.minimum(i, n_first - 1), 0)),
            pl.BlockSpec(block, lambda i: (jnp.maximum(i - n_first, 0), 0))]


def _inproj_body(xp_ref, xs_ref, w_ref, rope_ref, lng_ref, lnb_ref,
                 q_ref, k_ref, v_ref, kp_ref, vp_ref, gu_ref, vn_ref, *, n_first):
    xb = _pick_rows(xp_ref, xs_ref, n_first).astype(BF16)
    cs = rope_ref[:, 0:LANES]
    s1 = rope_ref[:, LANES:2 * LANES]
    s2 = rope_ref[:, 2 * LANES:3 * LANES]

    def rope(a):
        n = a.shape[1] // LANES
        width = a.shape[1]
        c = jnp.concatenate([cs] * n, axis=1)
        m1 = jnp.concatenate([s1] * n, axis=1)
        m2 = jnp.concatenate([s2] * n, axis=1)
        return a * c + pltpu.roll(a, width - ROT_DIM // 2, 1) * m1 + pltpu.roll(a, ROT_DIM // 2, 1) * m2

    def proj(c0, c1):
        return jnp.dot(xb, w_ref[:, c0:c1], preferred_element_type=F32)

    half = ATTN_WIDTH // 2
    for h in range(2):
        q = rope(proj(h * half, (h + 1) * half)) * np.float32(HEAD_DIM ** -0.5)
        q_ref[:, h * half:(h + 1) * half] = q.astype(BF16)

    kv = proj(ATTN_WIDTH, ATTN_WIDTH + 2 * KV_WIDTH)
    k = rope(kv[:, 0:KV_WIDTH])
    v = kv[:, KV_WIDTH:2 * KV_WIDTH]
    k_ref[...] = k
    v_ref[...] = v
    kp_ref[...] = _expand_kv(k).astype(BF16)
    vp_ref[...] = _expand_kv(v).astype(BF16)

    base = ATTN_WIDTH + 2 * KV_WIDTH
    for h in range(2):
        gu_ref[:, h * half:(h + 1) * half] = _gelu(proj(base + h * half, base + (h + 1) * half))

    base = base + GM_WIDTH
    for h in range(2):
        gv = _gelu(proj(base + h * half, base + (h + 1) * half))
        for g in range(half // GM_GROUP_DIM):
            t = gv[:, g * GM_GROUP_DIM:(g + 1) * GM_GROUP_DIM]
            col = h * half + g * GM_GROUP_DIM
            tc = t - jnp.mean(t, axis=-1, keepdims=True)
            var = jnp.mean(tc * tc, axis=-1, keepdims=True)
            y = tc * lax.rsqrt(var + NORM_EPS) * lng_ref[:, col:col + GM_GROUP_DIM] \
                + lnb_ref[:, col:col + GM_GROUP_DIM]
            vn_ref[:, col:col + GM_GROUP_DIM] = y


def _inproj(xp, xs, w, rope_tab, lng, lnb, tm):
    d = xp.shape[1]
    t_all = xp.shape[0] + xs.shape[0]
    n_first = xp.shape[0] // tm
    n_in = w.shape[1]
    row = lambda i: (i, 0)
    const = lambda i: (0, 0)
    out_shape = (
        jax.ShapeDtypeStruct((t_all, ATTN_WIDTH), BF16),
        jax.ShapeDtypeStruct((t_all, KV_WIDTH), F32),
        jax.ShapeDtypeStruct((t_all, KV_WIDTH), F32),
        jax.ShapeDtypeStruct((t_all, 2 * KV_WIDTH), BF16),
        jax.ShapeDtypeStruct((t_all, 2 * KV_WIDTH), BF16),
        jax.ShapeDtypeStruct((t_all, GM_WIDTH), F32),
        jax.ShapeDtypeStruct((t_all, GM_WIDTH), F32),
    )
    return pl.pallas_call(
        functools.partial(_inproj_body, n_first=n_first),
        grid=(t_all // tm,),
        in_specs=_split_specs((tm, d), n_first) + [
            _resident((d, n_in), const),
            pl.BlockSpec((tm, 3 * LANES), row),
            _resident((1, GM_WIDTH), const),
            _resident((1, GM_WIDTH), const),
        ],
        out_specs=(
            pl.BlockSpec((tm, ATTN_WIDTH), row),
            pl.BlockSpec((tm, KV_WIDTH), row),
            pl.BlockSpec((tm, KV_WIDTH), row),
            pl.BlockSpec((tm, 2 * KV_WIDTH), row),
            pl.BlockSpec((tm, 2 * KV_WIDTH), row),
            pl.BlockSpec((tm, GM_WIDTH), row),
            pl.BlockSpec((tm, GM_WIDTH), row),
        ),
        out_shape=out_shape,
        compiler_params=_cparams(("parallel",)),
        name="inproj",
    )(xp, xs, w, rope_tab, lng, lnb)


def _attn_group(qg, kexp, vexp, bias, sinks):
    nk = kexp.shape[0]
    seg = lax.broadcasted_iota(jnp.int32, kexp.shape, 1) // HEAD_DIM
    zero = jnp.zeros_like(kexp)
    kbd = jnp.concatenate([jnp.where(seg == j, kexp, zero) for j in range(Q_PER_KV)], axis=0)
    vbd = jnp.concatenate([jnp.where(seg == j, vexp, zero) for j in range(Q_PER_KV)], axis=0)
    s = lax.dot_general(qg, kbd, (((1,), (1,)), ((), ())), preferred_element_type=F32)
    es, rs = [], []
    for j in range(Q_PER_KV):
        sj = s[:, j * nk:(j + 1) * nk] + bias
        m = jnp.maximum(jnp.max(sj, axis=-1, keepdims=True), sinks[j])
        e = jnp.exp(sj - m)
        den = jnp.sum(e, axis=-1, keepdims=True) + jnp.exp(sinks[j] - m)
        es.append(e.astype(BF16))
        rs.append(1.0 / den)
    p = jnp.concatenate(es, axis=1)
    o = jnp.dot(p, vbd, preferred_element_type=F32)
    oseg = lax.broadcasted_iota(jnp.int32, o.shape, 1) // HEAD_DIM
    r = jnp.where(oseg == 0, rs[0], jnp.where(oseg == 1, rs[1], jnp.where(oseg == 2, rs[2], rs[3])))
    return o * r


def _attn_rows(q, kbuf, vbuf, bias, sink_ref, g_row):
    outs = []
    for h in range(N_KV_HEADS):
        kt = kbuf[:, h * LANES:(h + 1) * LANES]
        vt = vbuf[:, h * LANES:(h + 1) * LANES]
        kexp = jnp.concatenate([kt, kt], axis=1)
        vexp = jnp.concatenate([vt, vt], axis=1)
        sinks = [sink_ref[h * Q_PER_KV + j] for j in range(Q_PER_KV)]
        width = Q_PER_KV * HEAD_DIM
        outs.append(_attn_group(q[:, h * width:(h + 1) * width], kexp, vexp, bias, sinks))
    o = jnp.concatenate(outs, axis=1)
    ms = jnp.mean(o * o, axis=-1, keepdims=True)
    return o * lax.rsqrt(ms + NORM_EPS) * g_row


def _attn_prompt_body(sink_ref, q_ref, kc_ref, kprev_ref, vc_ref, vprev_ref, g_ref, o_ref,
                      kbuf, vbuf, *, tq):
    i = pl.program_id(1)
    kbuf[0:WINDOW, :] = kprev_ref[...]
    kbuf[WINDOW:WINDOW + tq, :] = kc_ref[...]
    vbuf[0:WINDOW, :] = vprev_ref[...]
    vbuf[WINDOW:WINDOW + tq, :] = vc_ref[...]
    rq = lax.broadcasted_iota(jnp.int32, (Q_BLOCK, KEY_BLOCK), 0) // CHUNK
    ck = lax.broadcasted_iota(jnp.int32, (Q_BLOCK, KEY_BLOCK), 1) // CHUNK
    rel = ck - rq
    band = (rel >= 0) & (rel <= WINDOW // CHUNK)
    g_row = g_ref[...]

    def sub(s, carry):
        r0 = pl.multiple_of(s * Q_BLOCK, Q_BLOCK)
        first = jnp.logical_and(i == 0, s == 0)
        kmin = jnp.where(first, WINDOW // CHUNK, 0)
        ok = jnp.logical_and(band, ck >= kmin)
        bias = jnp.where(ok, np.float32(0.0), np.float32(NEG_INF))
        q = q_ref[pl.ds(r0, Q_BLOCK), :]
        kb = kbuf[pl.ds(r0, KEY_BLOCK), :]
        vb = vbuf[pl.ds(r0, KEY_BLOCK), :]
        o_ref[pl.ds(r0, Q_BLOCK), :] = _attn_rows(q, kb, vb, bias, sink_ref, g_row).astype(BF16)
        return carry

    lax.fori_loop(0, tq // Q_BLOCK, sub, 0)


def _attn_prompt(sinks, q, kp, vp, g_attn, batch, seq, tq):
    nq = seq // tq
    per = tq // WINDOW
    cur = lambda b, i: (b * nq + i, 0)
    prev = lambda b, i: (b * (seq // WINDOW) + jnp.maximum(i * per - 1, 0), 0)
    kernel = functools.partial(_attn_prompt_body, tq=tq)
    return pl.pallas_call(
        kernel,
        grid=(batch, nq),
        in_specs=[
            pl.BlockSpec(memory_space=pltpu.SMEM),
            pl.BlockSpec((tq, ATTN_WIDTH), cur),
            pl.BlockSpec((tq, 2 * KV_WIDTH), cur),
            pl.BlockSpec((WINDOW, 2 * KV_WIDTH), prev),
            pl.BlockSpec((tq, 2 * KV_WIDTH), cur),
            pl.BlockSpec((WINDOW, 2 * KV_WIDTH), prev),
            pl.BlockSpec((1, ATTN_WIDTH), lambda b, i: (0, 0)),
        ],
        out_specs=pl.BlockSpec((tq, ATTN_WIDTH), cur),
        out_shape=jax.ShapeDtypeStruct((batch * seq, ATTN_WIDTH), BF16),
        scratch_shapes=[pltpu.VMEM((WINDOW + tq, 2 * KV_WIDTH), BF16),
                        pltpu.VMEM((WINDOW + tq, 2 * KV_WIDTH), BF16)],
        compiler_params=_cparams(("parallel", "parallel")),
        name="attn_prompt",
    )(sinks, q, kp, kp, vp, vp, g_attn)


def _attn_sample_body(sink_ref, q_ref, ck_ref, cv_ref, kn_ref, vn_ref, g_ref, o_ref, *, ds):
    pad = jnp.zeros((KEY_BLOCK - WINDOW - ds, KV_WIDTH), F32)
    k_all = jnp.concatenate([ck_ref[0], kn_ref[...], pad], axis=0)
    v_all = jnp.concatenate([cv_ref[0], vn_ref[...], pad], axis=0)
    kb = _expand_kv(k_all).astype(BF16)
    vb = _expand_kv(v_all).astype(BF16)
    col = lax.broadcasted_iota(jnp.int32, (ds, KEY_BLOCK), 1)
    bias = jnp.where(col < WINDOW + ds, np.float32(0.0), np.float32(NEG_INF))
    o_ref[...] = _attn_rows(q_ref[...], kb, vb, bias, sink_ref, g_ref[...]).astype(BF16)


def _attn_sample(sinks, q, k32, v32, cache_k, cache_v, g_attn, row0, dbatch, ds):
    off = row0 // ds
    new = lambda b: (off + b, 0)
    kernel = functools.partial(_attn_sample_body, ds=ds)
    return pl.pallas_call(
        kernel,
        grid=(dbatch,),
        in_specs=[
            pl.BlockSpec(memory_space=pltpu.SMEM),
            pl.BlockSpec((ds, ATTN_WIDTH), new),
            pl.BlockSpec((1, WINDOW, KV_WIDTH), lambda b: (b, 0, 0)),
            pl.BlockSpec((1, WINDOW, KV_WIDTH), lambda b: (b, 0, 0)),
            pl.BlockSpec((ds, KV_WIDTH), new),
            pl.BlockSpec((ds, KV_WIDTH), new),
            pl.BlockSpec((1, ATTN_WIDTH), lambda b: (0, 0)),
        ],
        out_specs=pl.BlockSpec((ds, ATTN_WIDTH), lambda b: (b, 0)),
        out_shape=jax.ShapeDtypeStruct((dbatch * ds, ATTN_WIDTH), BF16),
        compiler_params=_cparams(("parallel",)),
        name="attn_sample",
    )(sinks, q, cache_k, cache_v, k32, v32, g_attn)


def _gmlp_body(gu_ref, vn_ref, ws_ref, bias_ref, g_ref, o_ref, gm_scr, *, rows, nchunk):
    ci = lax.broadcasted_iota(jnp.int32, (GM_CHUNK, GM_CHUNK), 0) // CHUNK
    cj = lax.broadcasted_iota(jnp.int32, (GM_CHUNK, GM_CHUNK), 1) // CHUNK
    visible = ci >= cj
    for g in range(GM_GROUPS):
        lo, hi = g * GM_GROUP_DIM, (g + 1) * GM_GROUP_DIM
        w = jnp.where(visible, ws_ref[g], np.float32(0.0))[0:rows, :].astype(BF16)
        pieces = []
        for n in range(nchunk):
            piece = vn_ref[n * rows:(n + 1) * rows, lo:hi]
            if rows < GM_CHUNK:
                piece = jnp.concatenate([piece, jnp.zeros((GM_CHUNK - rows, GM_GROUP_DIM), F32)], axis=0)
            pieces.append(piece.astype(BF16))
        rhs = jnp.concatenate(pieces, axis=1)
        mix = jnp.dot(w, rhs, preferred_element_type=F32)
        b = bias_ref[0:rows, lo:hi]
        for n in range(nchunk):
            gm_scr[n * rows:(n + 1) * rows, lo:hi] = \
                gu_ref[n * rows:(n + 1) * rows, lo:hi] * (mix[:, n * GM_GROUP_DIM:(n + 1) * GM_GROUP_DIM] + b)
    gm = gm_scr[...]
    ms = jnp.mean(gm * gm, axis=-1, keepdims=True)
    o_ref[...] = (gm * lax.rsqrt(ms + NORM_EPS) * g_ref[...]).astype(BF16)


def _gmlp(gu, vn, ws, bias_full, g_gm, row0, n_rows, rows, nchunk):
    tile = rows * nchunk
    off = row0 // tile
    kernel = functools.partial(_gmlp_body, rows=rows, nchunk=nchunk)
    return pl.pallas_call(
        kernel,
        grid=(n_rows // tile,),
        in_specs=[
            pl.BlockSpec((tile, GM_WIDTH), lambda i: (off + i, 0)),
            pl.BlockSpec((tile, GM_WIDTH), lambda i: (off + i, 0)),
            pl.BlockSpec((GM_GROUPS, GM_CHUNK, GM_CHUNK), lambda i: (0, 0, 0)),
            pl.BlockSpec((GM_CHUNK, GM_WIDTH), lambda i: (0, 0)),
            pl.BlockSpec((1, GM_WIDTH), lambda i: (0, 0)),
        ],
        out_specs=pl.BlockSpec((tile, GM_WIDTH), lambda i: (i, 0)),
        out_shape=jax.ShapeDtypeStruct((n_rows, GM_WIDTH), BF16),
        scratch_shapes=[pltpu.VMEM((tile, GM_WIDTH), F32)],
        compiler_params=_cparams(("parallel",)),
        name="gmlp",
    )(gu, vn, ws, bias_full, g_gm)


def _layer_norm_rows(z, g, b):
    zc = z - jnp.mean(z, axis=-1, keepdims=True)
    var = jnp.mean(zc * zc, axis=-1, keepdims=True)
    return zc * lax.rsqrt(var + NORM_EPS) * g + b


def _tree_sum(items):
    while len(items) > 1:
        items = [a + b for a, b in zip(items[0::2], items[1::2])] + (items[-1:] if len(items) % 2 else [])
    return items[0]


def _first_max(items):
    while len(items) > 1:
        nxt = [tuple(jnp.where(a[0] >= b[0], x, y) for x, y in zip(a, b))
               for a, b in zip(items[0::2], items[1::2])]
        items = nxt + (items[-1:] if len(items) % 2 else [])
    return items[0]


def _route_rows(lg, rb):
    sc = 1.0 / (1.0 + jnp.exp(-lg))
    sel = sc + rb
    rows = [sel[e:e + 1, :] for e in range(N_EXPERTS)]
    scr = [sc[e:e + 1, :] for e in range(N_EXPERTS)]
    groups = []
    for g in range(N_EXPERT_GROUPS):
        a, b, c, d = rows[g * EXPERTS_PER_GROUP:(g + 1) * EXPERTS_PER_GROUP]
        hi1, lo1 = jnp.maximum(a, b), jnp.minimum(a, b)
        hi2, lo2 = jnp.maximum(c, d), jnp.minimum(c, d)
        top1 = jnp.maximum(hi1, hi2)
        top2 = jnp.maximum(jnp.minimum(hi1, hi2), jnp.maximum(lo1, lo2))
        members = tuple(rows[g * EXPERTS_PER_GROUP + j] for j in range(EXPERTS_PER_GROUP))
        scores = tuple(scr[g * EXPERTS_PER_GROUP + j] for j in range(EXPERTS_PER_GROUP))
        groups.append((top1 + top2, jnp.full_like(top1, g * EXPERTS_PER_GROUP)) + members + scores)
    best = _first_max(groups)
    first_id = best[1]
    cand = [(best[2 + j], best[2 + EXPERTS_PER_GROUP + j], first_id + np.float32(j))
            for j in range(EXPERTS_PER_GROUP)]
    _, g0, idx0 = _first_max(cand)
    neg = np.float32(-np.inf)
    cand2 = [(jnp.where(c[2] == idx0, neg, c[0]), c[1], c[2]) for c in cand]
    _, g1, idx1 = _first_max(cand2)
    tot = g0 + g1
    return idx0, idx1, g0 / tot, g1 / tot


def _outproj_body(ap_ref, as_ref, gp_ref, gs_ref, xp_ref, xs_ref, wo_ref, g1_ref, b1_ref, rwt_ref, rb_ref,
                  x1_ref, mi_ref, mf_ref, cnt_ref, carry, *, alpha, n_first):
    i = pl.program_id(0)
    tm = xp_ref.shape[0]

    @pl.when(i == 0)
    def _():
        carry[...] = jnp.zeros_like(carry)

    y = jnp.dot(_pick_rows(ap_ref, as_ref, n_first), wo_ref[0:ATTN_WIDTH, :], preferred_element_type=F32)
    y = y + jnp.dot(_pick_rows(gp_ref, gs_ref, n_first), wo_ref[ATTN_WIDTH:ATTN_WIDTH + GM_WIDTH, :],
                    preferred_element_type=F32)
    x1 = _layer_norm_rows(np.float32(alpha) * _pick_rows(xp_ref, xs_ref, n_first) + y,
                          g1_ref[...], b1_ref[...])
    x1_ref[...] = x1

    lg = lax.dot_general(rwt_ref[...], x1.astype(BF16), (((1,), (1,)), ((), ())),
                         preferred_element_type=F32)
    rb = jnp.concatenate([rb_ref[...]] * (tm // LANES), axis=1)
    idx0, idx1, gate0, gate1 = _route_rows(lg, rb)

    one, zero = np.float32(1.0), np.float32(0.0)
    hit0 = [idx0 == np.float32(e) for e in range(N_EXPERTS)]
    hit1 = [idx1 == np.float32(e) for e in range(N_EXPERTS)]
    chosen = jnp.concatenate(
        [jnp.where(jnp.logical_or(hit0[e], hit1[e]), one, zero) for e in range(N_EXPERTS)], axis=0)
    s_idx = lax.broadcasted_iota(jnp.int32, (tm, tm), 0)
    t_idx = lax.broadcasted_iota(jnp.int32, (tm, tm), 1)
    before = jnp.where(s_idx < t_idx, one, zero).astype(BF16)
    base = carry[...]
    rank = jnp.dot(chosen.astype(BF16), before, preferred_element_type=F32) \
        + jnp.concatenate([base] * (tm // LANES), axis=1)
    carry[...] = base + jnp.sum(chosen, axis=1, keepdims=True)
    cnt_ref[...] = carry[...]

    rank_rows = [rank[e:e + 1, :] for e in range(N_EXPERTS)]
    rank0 = _tree_sum([jnp.where(hit0[e], rank_rows[e], zero) for e in range(N_EXPERTS)])
    rank1 = _tree_sum([jnp.where(hit1[e], rank_rows[e], zero) for e in range(N_EXPERTS)])
    zr = jnp.zeros((1, tm), F32)
    mi_ref[...] = jnp.concatenate([idx0, idx1, rank0, rank1, zr, zr, zr, zr], axis=0).astype(jnp.int32)
    mf_ref[...] = jnp.concatenate([gate0, gate1, zr, zr, zr, zr, zr, zr], axis=0)


def _outproj(attn_p, attn_s, gm_p, gm_s, xp, xs, wo, g1, b1, rwt, rb, alpha, tm):
    d = xp.shape[1]
    t_all = xp.shape[0] + xs.shape[0]
    n_first = xp.shape[0] // tm
    row = lambda i: (i, 0)
    col = lambda i: (0, i)
    const = lambda i: (0, 0)
    kernel = functools.partial(_outproj_body, alpha=alpha, n_first=n_first)
    return pl.pallas_call(
        kernel,
        grid=(t_all // tm,),
        in_specs=_split_specs((tm, ATTN_WIDTH), n_first) + _split_specs((tm, GM_WIDTH), n_first)
        + _split_specs((tm, d), n_first) + [
            _resident((ATTN_WIDTH + GM_WIDTH, d), const),
            _resident((1, d), const),
            _resident((1, d), const),
            _resident((N_EXPERTS, d), const),
            _resident((N_EXPERTS, LANES), const),
        ],
        out_specs=(
            pl.BlockSpec((tm, d), row),
            pl.BlockSpec((8, tm), col),
            pl.BlockSpec((8, tm), col),
            pl.BlockSpec((N_EXPERTS, LANES), const),
        ),
        out_shape=(
            jax.ShapeDtypeStruct((t_all, d), F32),
            jax.ShapeDtypeStruct((8, t_all), jnp.int32),
            jax.ShapeDtypeStruct((8, t_all), F32),
            jax.ShapeDtypeStruct((N_EXPERTS, LANES), F32),
        ),
        scratch_shapes=[pltpu.VMEM((N_EXPERTS, LANES), F32)],
        compiler_params=_cparams(("arbitrary",)),
        name="outproj_router",
    )(attn_p, attn_s, gm_p, gm_s, xp, xs, wo, g1, b1, rwt, rb)


def _dispatch_body(pz_ref, pe_ref, nu_ref, dest_ref, x1_ref, rows_hbm, x1s, zbuf, sem, zsem, *, tb, n_blocks):
    i = pl.program_id(0)
    n = pl.num_programs(0)
    slot = i % 2

    def wait_tile(s):
        for _ in range(2):
            pltpu.make_async_copy(x1s.at[s], rows_hbm.at[pl.ds(0, tb), :], sem.at[s]).wait()

    x1s[slot] = x1_ref[...]

    def issue(r, c):
        for k in (2, 3):
            pltpu.make_async_copy(x1s.at[slot].at[pl.ds(r, 1), :],
                                  rows_hbm.at[pl.ds(dest_ref[k, r], 1), :], sem.at[slot]).start()
        return c

    lax.fori_loop(0, tb, issue, 0, unroll=8)

    @pl.when(i > 0)
    def _():
        wait_tile(1 - slot)

    @pl.when(i == n - 1)
    def _():
        wait_tile(slot)
        zbuf[...] = jnp.zeros_like(zbuf)

        def row_copy(r):
            return pltpu.make_async_copy(zbuf.at[pl.ds(0, 1), :], rows_hbm.at[pl.ds(r, 1), :], zsem.at[0])

        def block_copy(j):
            r0 = pl.multiple_of(j * EXPERT_ROWS, EXPERT_ROWS)
            return pltpu.make_async_copy(zbuf, rows_hbm.at[pl.ds(r0, EXPERT_ROWS), :], zsem.at[0])

        for e in range(N_EXPERTS):
            lax.fori_loop(pz_ref[e], pe_ref[e], lambda r, c: (row_copy(r).start(), c)[1], 0)
        lax.fori_loop(nu_ref[0], n_blocks, lambda j, c: (block_copy(j).start(), c)[1], 0)
        for e in range(N_EXPERTS):
            lax.fori_loop(pz_ref[e], pe_ref[e], lambda r, c: (row_copy(r).wait(), c)[1], 0)
        lax.fori_loop(nu_ref[0], n_blocks, lambda j, c: (block_copy(j).wait(), c)[1], 0)


def _dispatch(pad_lo, pad_hi, n_used, dest, x1, n_blocks, tb):
    t_all, d = x1.shape
    kernel = functools.partial(_dispatch_body, tb=tb, n_blocks=n_blocks)
    grid_spec = pltpu.PrefetchScalarGridSpec(
        num_scalar_prefetch=3,
        grid=(t_all // tb,),
        in_specs=[
            pl.BlockSpec((8, tb), lambda i, pz, pe, nu: (0, i), memory_space=pltpu.SMEM),
            pl.BlockSpec((tb, d), lambda i, pz, pe, nu: (i, 0)),
        ],
        out_specs=pl.BlockSpec(memory_space=pl.ANY),
        scratch_shapes=[pltpu.VMEM((2, tb, d), F32), pltpu.VMEM((EXPERT_ROWS, d), F32),
                        pltpu.SemaphoreType.DMA((2,)), pltpu.SemaphoreType.DMA((1,))],
    )
    return pl.pallas_call(
        kernel,
        grid_spec=grid_spec,
        out_shape=jax.ShapeDtypeStruct((n_blocks * EXPERT_ROWS, d), F32),
        compiler_params=_cparams(("arbitrary",)),
        name="dispatch",
    )(pad_lo, pad_hi, n_used, dest, x1)


def _expert_body(be_ref, nu_ref, x_ref, wg_ref, wu_ref, wd_ref, o_ref):
    j = pl.program_id(0)

    @pl.when(j < nu_ref[0])
    def _():
        xb = x_ref[...].astype(BF16)
        gate = jnp.dot(xb, wg_ref[0], preferred_element_type=F32)
        up = jnp.dot(xb, wu_ref[0], preferred_element_type=F32)
        h = gate * (1.0 / (1.0 + jnp.exp(-gate))) * up
        o_ref[...] = jnp.dot(h.astype(BF16), wd_ref[0], preferred_element_type=F32)

    @pl.when(j >= nu_ref[0])
    def _():
        o_ref[...] = jnp.zeros_like(o_ref)


def _experts(block_expert, n_used, rows, wg, wu, wd, n_blocks):
    d = rows.shape[1]
    de = wg.shape[2]
    grid_spec = pltpu.PrefetchScalarGridSpec(
        num_scalar_prefetch=2,
        grid=(n_blocks,),
        in_specs=[
            pl.BlockSpec((EXPERT_ROWS, d), lambda j, be, nu: (jnp.minimum(j, nu[0] - 1), 0)),
            pl.BlockSpec((1, d, de), lambda j, be, nu: (be[j], 0, 0)),
            pl.BlockSpec((1, d, de), lambda j, be, nu: (be[j], 0, 0)),
            pl.BlockSpec((1, de, d), lambda j, be, nu: (be[j], 0, 0)),
        ],
        out_specs=pl.BlockSpec((EXPERT_ROWS, d), lambda j, be, nu: (j, 0)),
    )
    return pl.pallas_call(
        _expert_body,
        grid_spec=grid_spec,
        out_shape=jax.ShapeDtypeStruct(rows.shape, F32),
        compiler_params=_cparams(("arbitrary",)),
        name="experts",
    )(block_expert, n_used, rows, wg, wu, wd)


def _combine_body(d0_ref, dn_ref, x1_ref, gt_ref, rows_hbm, g2_ref, b2_ref, x2_ref, rbuf, sem, *, alpha):
    i = pl.program_id(0)
    n = pl.num_programs(0)
    tm = x1_ref.shape[0]
    slot = i % 2

    def gather(idx_ref, s):
        def body(r, c):
            for k in range(2):
                pltpu.make_async_copy(rows_hbm.at[pl.ds(idx_ref[2 + k, r], 1), :],
                                      rbuf.at[s, k].at[pl.ds(r, 1), :], sem.at[s]).start()
            return c
        lax.fori_loop(0, tm, body, 0, unroll=8)

    @pl.when(i == 0)
    def _():
        gather(d0_ref, 0)

    @pl.when(i + 1 < n)
    def _():
        gather(dn_ref, 1 - slot)

    for k in range(2):
        pltpu.make_async_copy(rows_hbm.at[pl.ds(0, tm), :], rbuf.at[slot, k], sem.at[slot]).wait()
    g0 = gt_ref[:, 0:1]
    g1 = gt_ref[:, 1:2]
    y = rbuf[slot, 0] * g0 + rbuf[slot, 1] * g1
    x2_ref[...] = _layer_norm_rows(np.float32(alpha) * x1_ref[...] + y, g2_ref[...], b2_ref[...])


def _combine(meta_i, x1, gates_t, out_rows, g2, b2, alpha, tm, tile0, n_tiles):
    d = x1.shape[1]
    kernel = functools.partial(_combine_body, alpha=alpha)
    return pl.pallas_call(
        kernel,
        grid=(n_tiles,),
        in_specs=[
            pl.BlockSpec((8, tm), lambda i: (0, tile0), memory_space=pltpu.SMEM),
            pl.BlockSpec((8, tm), lambda i: (0, tile0 + jnp.minimum(i + 1, n_tiles - 1)),
                         memory_space=pltpu.SMEM),
            pl.BlockSpec((tm, d), lambda i: (tile0 + i, 0)),
            pl.BlockSpec((tm, 2), lambda i: (tile0 + i, 0)),
            pl.BlockSpec(memory_space=pl.ANY),
            _resident((1, d), lambda i: (0, 0)),
            _resident((1, d), lambda i: (0, 0)),
        ],
        out_specs=pl.BlockSpec((tm, d), lambda i: (i, 0)),
        out_shape=jax.ShapeDtypeStruct((n_tiles * tm, d), F32),
        scratch_shapes=[pltpu.VMEM((2, 2, tm, d), F32), pltpu.SemaphoreType.DMA((2,))],
        compiler_params=_cparams(("arbitrary",)),
        name="combine_ln",
    )(meta_i, meta_i, x1, gates_t, out_rows, g2, b2)


def _rope_table(pos):
    half = ROT_DIM // 2
    inv_freq = jnp.power(ROPE_THETA, -2.0 * jnp.arange(half, dtype=F32) / ROT_DIM)
    ang = pos.astype(F32)[:, None] * inv_freq[None, :]
    cos, sin = jnp.cos(ang), jnp.sin(ang)
    n = pos.shape[0]
    ones = jnp.ones((n, HEAD_DIM - ROT_DIM), F32)
    zeros = jnp.zeros((n, HEAD_DIM - ROT_DIM), F32)
    zh = jnp.zeros((n, half), F32)
    c = jnp.concatenate([cos, cos, ones], axis=1)
    s1 = jnp.concatenate([-sin, zh, zeros], axis=1)
    s2 = jnp.concatenate([zh, sin, zeros], axis=1)
    rep = LANES // HEAD_DIM
    return jnp.concatenate([jnp.tile(c, (1, rep)), jnp.tile(s1, (1, rep)), jnp.tile(s2, (1, rep))], axis=1)


def _block_plan(meta_i, counts, n_blocks):
    cnt = counts[:, 0].astype(jnp.int32)
    padded = (cnt + EXPERT_ROWS - 1) // EXPERT_ROWS * EXPERT_ROWS
    pend = jnp.cumsum(padded)
    pstart = pend - padded
    n_used = pend[-1] // EXPERT_ROWS
    blk = jnp.minimum(jnp.arange(n_blocks, dtype=jnp.int32), n_used - 1)
    block_expert = jnp.sum((blk[:, None] * EXPERT_ROWS >= pend[None, :]).astype(jnp.int32), axis=1)
    experts = jnp.arange(N_EXPERTS, dtype=jnp.int32)[None, :, None]
    start_of = jnp.sum(jnp.where(meta_i[0:2, None, :] == experts, pstart[None, :, None], 0), axis=1)
    dest = jnp.concatenate([meta_i[0:2], start_of + meta_i[2:4], meta_i[4:8]], axis=0)
    return block_expert, n_used.reshape(1), pstart + cnt, pend, dest


def kernel(x_prompt, x_sample, cache_k, cache_v, w_in, w_o, attn_sinks, attn_out_g, gm_out_g,
           gm_ln_g, gm_ln_b, gm_ws, gm_bs, ln1_g, ln1_b, ln2_g, ln2_b, router_w, router_b,
           w_gate, w_up, w_down):
    depth = w_in.shape[0]
    batch, seq, d = x_prompt.shape
    dbatch, ds, _ = x_sample.shape
    alpha = (2 * depth) ** 0.25
    t_p, t_s = batch * seq, dbatch * ds
    t_all = t_p + t_s
    tm = 256 if (t_p % 256 == 0 and t_s % 256 == 0) else 128
    tq = 512 if seq % 512 == 0 else seq
    assert seq % GM_CHUNK == 0 and seq % tq == 0 and t_all % tm == 0 and ds <= CHUNK
    gm_tile_p = 4 * GM_CHUNK if seq % (4 * GM_CHUNK) == 0 else GM_CHUNK
    assert t_p % (dbatch * ds) == 0
    n_blocks = -(-2 * t_all // EXPERT_ROWS) + N_EXPERTS

    xp, xs = x_prompt.reshape(t_p, d), x_sample.reshape(t_s, d)
    rope_tab = jnp.concatenate(
        [jnp.tile(_rope_table(jnp.arange(seq)), (batch, 1)),
         jnp.tile(_rope_table(PAST_LEN + jnp.arange(ds)), (dbatch, 1))], axis=0)
    rwt = router_w.T.astype(BF16)
    rb = jnp.broadcast_to(router_b.astype(F32)[:, None], (N_EXPERTS, LANES))
    cache_k = cache_k.reshape(depth, dbatch, WINDOW, KV_WIDTH)
    cache_v = cache_v.reshape(depth, dbatch, WINDOW, KV_WIDTH)

    kp_out, vp_out, ks_out, vs_out, gv_out = [], [], [], [], []
    for l in range(depth):
        q, k32, v32, kp, vp, gu, vn = _inproj(
            xp, xs, w_in[l].astype(BF16), rope_tab,
            gm_ln_g[l].reshape(1, GM_WIDTH), gm_ln_b[l].reshape(1, GM_WIDTH), tm)
        g_attn = attn_out_g[l].reshape(1, ATTN_WIDTH)
        g_gm = gm_out_g[l].reshape(1, GM_WIDTH)
        attn_p = _attn_prompt(attn_sinks[l], q, kp, vp, g_attn, batch, seq, tq)
        attn_s = _attn_sample(attn_sinks[l], q, k32, v32, cache_k[l], cache_v[l], g_attn, t_p, dbatch, ds)
        bias_full = jnp.repeat(gm_bs[l].T, GM_GROUP_DIM, axis=1)
        gm_p = _gmlp(gu, vn, gm_ws[l], bias_full, g_gm, 0, t_p, GM_CHUNK, gm_tile_p // GM_CHUNK)
        gm_s = _gmlp(gu, vn, gm_ws[l], bias_full, g_gm, t_p, t_s, ds, dbatch)
        x1, meta_i, meta_f, counts = _outproj(
            attn_p, attn_s, gm_p, gm_s, xp, xs, w_o[l].astype(BF16),
            ln1_g[l].reshape(1, d), ln1_b[l].reshape(1, d), rwt, rb, alpha, tm)
        block_expert, n_used, pad_lo, pad_hi, dest = _block_plan(meta_i, counts, n_blocks)
        rows = _dispatch(pad_lo, pad_hi, n_used, dest, x1, n_blocks, tm)
        out_rows = _experts(block_expert, n_used, rows, w_gate[l].astype(BF16),
                            w_up[l].astype(BF16), w_down[l].astype(BF16), n_blocks)
        gates_t = meta_f[0:2].T
        g2, b2 = ln2_g[l].reshape(1, d), ln2_b[l].reshape(1, d)
        xp = _combine(dest, x1, gates_t, out_rows, g2, b2, alpha, tm, 0, t_p // tm)
        xs = _combine(dest, x1, gates_t, out_rows, g2, b2, alpha, tm, t_p // tm, t_s // tm)

        def last_window(a):
            return a[:t_p].reshape(batch, seq, KV_WIDTH)[:, seq - WINDOW:] \
                .reshape(batch, WINDOW, N_KV_HEADS, HEAD_DIM)

        kp_out.append(last_window(k32))
        vp_out.append(last_window(v32))
        k_s = k32[t_p:].reshape(dbatch, ds, KV_WIDTH)
        v_s = v32[t_p:].reshape(dbatch, ds, KV_WIDTH)
        ks_out.append(jnp.concatenate([cache_k[l][:, ds:], k_s], axis=1)
                      .reshape(dbatch, WINDOW, N_KV_HEADS, HEAD_DIM))
        vs_out.append(jnp.concatenate([cache_v[l][:, ds:], v_s], axis=1)
                      .reshape(dbatch, WINDOW, N_KV_HEADS, HEAD_DIM))
        gv_out.append(vn[t_p:].reshape(dbatch, ds, GM_GROUPS, GM_GROUP_DIM))

    return (xp.reshape(batch, seq, d), xs.reshape(dbatch, ds, d), jnp.stack(kp_out), jnp.stack(vp_out), jnp.stack(ks_out),
            jnp.stack(vs_out), jnp.stack(gv_out))
```

```python
import functools

import jax
import jax.numpy as jnp
import numpy as np
from jax import lax
from jax.experimental import pallas as pl
from jax.experimental.pallas import tpu as pltpu

F32 = jnp.float32
BF16 = jnp.bfloat16

CHUNK = 64
N_HEADS = 16
N_KV_HEADS = 4
HEAD_DIM = 64
Q_PER_KV = N_HEADS // N_KV_HEADS
ATTN_WIDTH = N_HEADS * HEAD_DIM
KV_WIDTH = N_KV_HEADS * HEAD_DIM
WINDOW = 128
ROT_DIM = HEAD_DIM // 4
ROPE_THETA = 500000.0
GM_CHUNK = 128
GM_GROUPS = 8
GM_GROUP_DIM = 128
GM_WIDTH = GM_GROUPS * GM_GROUP_DIM
N_EXPERTS = 16
N_EXPERT_GROUPS = 4
EXPERTS_PER_GROUP = N_EXPERTS // N_EXPERT_GROUPS
NORM_EPS = 1e-5
NEG_INF = -1e30
PAST_LEN = 4096

LANES = 128
KEY_BLOCK = 256
Q_BLOCK = 128
EXPERT_ROWS = 256
VMEM_LIMIT = 56 * 1024 * 1024


def _cparams(sem):
    return pltpu.CompilerParams(dimension_semantics=sem, vmem_limit_bytes=VMEM_LIMIT)


def _resident(shape, index_map):
    return pl.BlockSpec(shape, index_map, pipeline_mode=pl.Buffered(1))


def _gelu(a):
    return 0.5 * a * (1.0 + lax.erf(a * np.float32(0.7071067811865476)))


def _pair_expand(t):
    r = pltpu.roll(t, HEAD_DIM, 1)
    lane = lax.broadcasted_iota(jnp.int32, t.shape, 1)
    lo = lane < HEAD_DIM
    return jnp.where(lo, t, r), jnp.where(lo, r, t)


def _expand_kv(a):
    a0, b0 = _pair_expand(a[:, 0:LANES])
    a1, b1 = _pair_expand(a[:, LANES:2 * LANES])
    return jnp.concatenate([a0, b0, a1, b1], axis=1)


def _pick_rows(first_ref, second_ref, n_first):
    return jnp.where(pl.program_id(0) < n_first, first_ref[...], second_ref[...])


def _split_specs(block, n_first):
    return [pl.BlockSpec(block, lambda i: (jnp.minimum(i, n_first - 1), 0)),
            pl.BlockSpec(block, lambda i: (jnp.maximum(i - n_first, 0), 0))]


def _inproj_body(xp_ref, xs_ref, w_ref, rope_ref, lng_ref, lnb_ref,
                 q_ref, k_ref, v_ref, kp_ref, vp_ref, gu_ref, vn_ref, *, n_first):
    xb = _pick_rows(xp_ref, xs_ref, n_first).astype(BF16)
    cs = rope_ref[:, 0:LANES]
    s1 = rope_ref[:, LANES:2 * LANES]
    s2 = rope_ref[:, 2 * LANES:3 * LANES]

    def rope(a):
        n = a.shape[1] // LANES
        width = a.shape[1]
        c = jnp.concatenate([cs] * n, axis=1)
        m1 = jnp.concatenate([s1] * n, axis=1)
        m2 = jnp.concatenate([s2] * n, axis=1)
        return a * c + pltpu.roll(a, width - ROT_DIM // 2, 1) * m1 + pltpu.roll(a, ROT_DIM // 2, 1) * m2

    def proj(c0, c1):
        return jnp.dot(xb, w_ref[0, :, c0:c1], preferred_element_type=F32)

    half = ATTN_WIDTH // 2
    for h in range(2):
        q = rope(proj(h * half, (h + 1) * half)) * np.float32(HEAD_DIM ** -0.5)
        q_ref[:, h * half:(h + 1) * half] = q.astype(BF16)

    kv = proj(ATTN_WIDTH, ATTN_WIDTH + 2 * KV_WIDTH)
    k = rope(kv[:, 0:KV_WIDTH])
    v = kv[:, KV_WIDTH:2 * KV_WIDTH]
    k_ref[...] = k
    v_ref[...] = v
    kp_ref[...] = _expand_kv(k).astype(BF16)
    vp_ref[...] = _expand_kv(v).astype(BF16)

    base = ATTN_WIDTH + 2 * KV_WIDTH
    for h in range(2):
        gu_ref[:, h * half:(h + 1) * half] = _gelu(proj(base + h * half, base + (h + 1) * half))

    base = base + GM_WIDTH
    for h in range(2):
        gv = _gelu(proj(base + h * half, base + (h + 1) * half))
        for g in range(half // GM_GROUP_DIM):
            t = gv[:, g * GM_GROUP_DIM:(g + 1) * GM_GROUP_DIM]
            col = h * half + g * GM_GROUP_DIM
            tc = t - jnp.mean(t, axis=-1, keepdims=True)
            var = jnp.mean(tc * tc, axis=-1, keepdims=True)
            y = tc * lax.rsqrt(var + NORM_EPS) * lng_ref[:, col:col + GM_GROUP_DIM] \
                + lnb_ref[:, col:col + GM_GROUP_DIM]
            vn_ref[:, col:col + GM_GROUP_DIM] = y


def _inproj(xp, xs, w, layer, rope_tab, lng, lnb, tm):
    d = xp.shape[1]
    t_all = xp.shape[0] + xs.shape[0]
    n_first = xp.shape[0] // tm
    n_in = w.shape[2]
    row = lambda i: (i, 0)
    const = lambda i: (0, 0)
    out_shape = (
        jax.ShapeDtypeStruct((t_all, ATTN_WIDTH), BF16),
        jax.ShapeDtypeStruct((t_all, KV_WIDTH), F32),
        jax.ShapeDtypeStruct((t_all, KV_WIDTH), F32),
        jax.ShapeDtypeStruct((t_all, 2 * KV_WIDTH), BF16),
        jax.ShapeDtypeStruct((t_all, 2 * KV_WIDTH), BF16),
        jax.ShapeDtypeStruct((t_all, GM_WIDTH), F32),
        jax.ShapeDtypeStruct((t_all, GM_WIDTH), F32),
    )
    return pl.pallas_call(
        functools.partial(_inproj_body, n_first=n_first),
        grid=(t_all // tm,),
        in_specs=_split_specs((tm, d), n_first) + [
            _resident((1, d, n_in), lambda i: (layer, 0, 0)),
            pl.BlockSpec((tm, 3 * LANES), row),
            _resident((1, GM_WIDTH), const),
            _resident((1, GM_WIDTH), const),
        ],
        out_specs=(
            pl.BlockSpec((tm, ATTN_WIDTH), row),
            pl.BlockSpec((tm, KV_WIDTH), row),
            pl.BlockSpec((tm, KV_WIDTH), row),
            pl.BlockSpec((tm, 2 * KV_WIDTH), row),
            pl.BlockSpec((tm, 2 * KV_WIDTH), row),
            pl.BlockSpec((tm, GM_WIDTH), row),
            pl.BlockSpec((tm, GM_WIDTH), row),
        ),
        out_shape=out_shape,
        compiler_params=_cparams(("parallel",)),
        name="inproj",
    )(xp, xs, w, rope_tab, lng, lnb)


def _attn_group(qg, kexp, vexp, bias, sinks):
    nk = kexp.shape[0]
    seg = lax.broadcasted_iota(jnp.int32, kexp.shape, 1) // HEAD_DIM
    zero = jnp.zeros_like(kexp)
    kbd = jnp.concatenate([jnp.where(seg == j, kexp, zero) for j in range(Q_PER_KV)], axis=0)
    vbd = jnp.concatenate([jnp.where(seg == j, vexp, zero) for j in range(Q_PER_KV)], axis=0)
    s = lax.dot_general(qg, kbd, (((1,), (1,)), ((), ())), preferred_element_type=F32)
    es, rs = [], []
    for j in range(Q_PER_KV):
        sj = s[:, j * nk:(j + 1) * nk] + bias
        m = jnp.maximum(jnp.max(sj, axis=-1, keepdims=True), sinks[j])
        e = jnp.exp(sj - m)
        den = jnp.sum(e, axis=-1, keepdims=True) + jnp.exp(sinks[j] - m)
        es.append(e.astype(BF16))
        rs.append(1.0 / den)
    p = jnp.concatenate(es, axis=1)
    o = jnp.dot(p, vbd, preferred_element_type=F32)
    oseg = lax.broadcasted_iota(jnp.int32, o.shape, 1) // HEAD_DIM
    r = jnp.where(oseg == 0, rs[0], jnp.where(oseg == 1, rs[1], jnp.where(oseg == 2, rs[2], rs[3])))
    return o * r


def _attn_rows(q, kbuf, vbuf, bias, sink_ref, g_row):
    outs = []
    for h in range(N_KV_HEADS):
        kt = kbuf[:, h * LANES:(h + 1) * LANES]
        vt = vbuf[:, h * LANES:(h + 1) * LANES]
        kexp = jnp.concatenate([kt, kt], axis=1)
        vexp = jnp.concatenate([vt, vt], axis=1)
        sinks = [sink_ref[h * Q_PER_KV + j] for j in range(Q_PER_KV)]
        width = Q_PER_KV * HEAD_DIM
        outs.append(_attn_group(q[:, h * width:(h + 1) * width], kexp, vexp, bias, sinks))
    o = jnp.concatenate(outs, axis=1)
    ms = jnp.mean(o * o, axis=-1, keepdims=True)
    return o * lax.rsqrt(ms + NORM_EPS) * g_row


def _attn_prompt_body(sink_ref, q_ref, kc_ref, kprev_ref, vc_ref, vprev_ref, g_ref, o_ref,
                      kbuf, vbuf, *, tq):
    i = pl.program_id(1)
    kbuf[0:WINDOW, :] = kprev_ref[...]
    kbuf[WINDOW:WINDOW + tq, :] = kc_ref[...]
    vbuf[0:WINDOW, :] = vprev_ref[...]
    vbuf[WINDOW:WINDOW + tq, :] = vc_ref[...]
    rq = lax.broadcasted_iota(jnp.int32, (Q_BLOCK, KEY_BLOCK), 0) // CHUNK
    ck = lax.broadcasted_iota(jnp.int32, (Q_BLOCK, KEY_BLOCK), 1) // CHUNK
    rel = ck - rq
    band = (rel >= 0) & (rel <= WINDOW // CHUNK)
    g_row = g_ref[...]

    def sub(s, carry):
        r0 = pl.multiple_of(s * Q_BLOCK, Q_BLOCK)
        first = jnp.logical_and(i == 0, s == 0)
        kmin = jnp.where(first, WINDOW // CHUNK, 0)
        ok = jnp.logical_and(band, ck >= kmin)
        bias = jnp.where(ok, np.float32(0.0), np.float32(NEG_INF))
        q = q_ref[pl.ds(r0, Q_BLOCK), :]
        kb = kbuf[pl.ds(r0, KEY_BLOCK), :]
        vb = vbuf[pl.ds(r0, KEY_BLOCK), :]
        o_ref[pl.ds(r0, Q_BLOCK), :] = _attn_rows(q, kb, vb, bias, sink_ref, g_row).astype(BF16)
        return carry

    lax.fori_loop(0, tq // Q_BLOCK, sub, 0)


def _attn_prompt(sinks, q, kp, vp, g_attn, batch, seq, tq):
    nq = seq // tq
    per = tq // WINDOW
    cur = lambda b, i: (b * nq + i, 0)
    prev = lambda b, i: (b * (seq // WINDOW) + jnp.maximum(i * per - 1, 0), 0)
    kernel = functools.partial(_attn_prompt_body, tq=tq)
    return pl.pallas_call(
        kernel,
        grid=(batch, nq),
        in_specs=[
            pl.BlockSpec(memory_space=pltpu.SMEM),
            pl.BlockSpec((tq, ATTN_WIDTH), cur),
            pl.BlockSpec((tq, 2 * KV_WIDTH), cur),
            pl.BlockSpec((WINDOW, 2 * KV_WIDTH), prev),
            pl.BlockSpec((tq, 2 * KV_WIDTH), cur),
            pl.BlockSpec((WINDOW, 2 * KV_WIDTH), prev),
            pl.BlockSpec((1, ATTN_WIDTH), lambda b, i: (0, 0)),
        ],
        out_specs=pl.BlockSpec((tq, ATTN_WIDTH), cur),
        out_shape=jax.ShapeDtypeStruct((batch * seq, ATTN_WIDTH), BF16),
        scratch_shapes=[pltpu.VMEM((WINDOW + tq, 2 * KV_WIDTH), BF16),
                        pltpu.VMEM((WINDOW + tq, 2 * KV_WIDTH), BF16)],
        compiler_params=_cparams(("parallel", "parallel")),
        name="attn_prompt",
    )(sinks, q, kp, kp, vp, vp, g_attn)


def _attn_sample_body(sink_ref, q_ref, ck_ref, cv_ref, kn_ref, vn_ref, g_ref, o_ref, *, ds):
    pad = jnp.zeros((KEY_BLOCK - WINDOW - ds, KV_WIDTH), F32)
    k_all = jnp.concatenate([ck_ref[0], kn_ref[...], pad], axis=0)
    v_all = jnp.concatenate([cv_ref[0], vn_ref[...], pad], axis=0)
    kb = _expand_kv(k_all).astype(BF16)
    vb = _expand_kv(v_all).astype(BF16)
    col = lax.broadcasted_iota(jnp.int32, (ds, KEY_BLOCK), 1)
    bias = jnp.where(col < WINDOW + ds, np.float32(0.0), np.float32(NEG_INF))
    o_ref[...] = _attn_rows(q_ref[...], kb, vb, bias, sink_ref, g_ref[...]).astype(BF16)


def _attn_sample(sinks, q, k32, v32, cache_k, cache_v, g_attn, row0, dbatch, ds):
    off = row0 // ds
    new = lambda b: (off + b, 0)
    kernel = functools.partial(_attn_sample_body, ds=ds)
    return pl.pallas_call(
        kernel,
        grid=(dbatch,),
        in_specs=[
            pl.BlockSpec(memory_space=pltpu.SMEM),
            pl.BlockSpec((ds, ATTN_WIDTH), new),
            pl.BlockSpec((1, WINDOW, KV_WIDTH), lambda b: (b, 0, 0)),
            pl.BlockSpec((1, WINDOW, KV_WIDTH), lambda b: (b, 0, 0)),
            pl.BlockSpec((ds, KV_WIDTH), new),
            pl.BlockSpec((ds, KV_WIDTH), new),
            pl.BlockSpec((1, ATTN_WIDTH), lambda b: (0, 0)),
        ],
        out_specs=pl.BlockSpec((ds, ATTN_WIDTH), lambda b: (b, 0)),
        out_shape=jax.ShapeDtypeStruct((dbatch * ds, ATTN_WIDTH), BF16),
        compiler_params=_cparams(("parallel",)),
        name="attn_sample",
    )(sinks, q, cache_k, cache_v, k32, v32, g_attn)


def _gmlp_body(gu_ref, vn_ref, ws_ref, bias_ref, g_ref, o_ref, gm_scr, *, rows, nchunk):
    ci = lax.broadcasted_iota(jnp.int32, (GM_CHUNK, GM_CHUNK), 0) // CHUNK
    cj = lax.broadcasted_iota(jnp.int32, (GM_CHUNK, GM_CHUNK), 1) // CHUNK
    visible = ci >= cj
    for g in range(GM_GROUPS):
        lo, hi = g * GM_GROUP_DIM, (g + 1) * GM_GROUP_DIM
        w = jnp.where(visible, ws_ref[g], np.float32(0.0))[0:rows, :].astype(BF16)
        pieces = []
        for n in range(nchunk):
            piece = vn_ref[n * rows:(n + 1) * rows, lo:hi]
            if rows < GM_CHUNK:
                piece = jnp.concatenate([piece, jnp.zeros((GM_CHUNK - rows, GM_GROUP_DIM), F32)], axis=0)
            pieces.append(piece.astype(BF16))
        rhs = jnp.concatenate(pieces, axis=1)
        mix = jnp.dot(w, rhs, preferred_element_type=F32)
        b = bias_ref[0:rows, lo:hi]
        for n in range(nchunk):
            gm_scr[n * rows:(n + 1) * rows, lo:hi] = \
                gu_ref[n * rows:(n + 1) * rows, lo:hi] * (mix[:, n * GM_GROUP_DIM:(n + 1) * GM_GROUP_DIM] + b)
    gm = gm_scr[...]
    ms = jnp.mean(gm * gm, axis=-1, keepdims=True)
    o_ref[...] = (gm * lax.rsqrt(ms + NORM_EPS) * g_ref[...]).astype(BF16)


def _gmlp(gu, vn, ws, bias_full, g_gm, row0, n_rows, rows, nchunk):
    tile = rows * nchunk
    off = row0 // tile
    kernel = functools.partial(_gmlp_body, rows=rows, nchunk=nchunk)
    return pl.pallas_call(
        kernel,
        grid=(n_rows // tile,),
        in_specs=[
            pl.BlockSpec((tile, GM_WIDTH), lambda i: (off + i, 0)),
            pl.BlockSpec((tile, GM_WIDTH), lambda i: (off + i, 0)),
            pl.BlockSpec((GM_GROUPS, GM_CHUNK, GM_CHUNK), lambda i: (0, 0, 0)),
            pl.BlockSpec((GM_CHUNK, GM_WIDTH), lambda i: (0, 0)),
            pl.BlockSpec((1, GM_WIDTH), lambda i: (0, 0)),
        ],
        out_specs=pl.BlockSpec((tile, GM_WIDTH), lambda i: (i, 0)),
        out_shape=jax.ShapeDtypeStruct((n_rows, GM_WIDTH), BF16),
        scratch_shapes=[pltpu.VMEM((tile, GM_WIDTH), F32)],
        compiler_params=_cparams(("parallel",)),
        name="gmlp",
    )(gu, vn, ws, bias_full, g_gm)


def _layer_norm_rows(z, g, b):
    zc = z - jnp.mean(z, axis=-1, keepdims=True)
    var = jnp.mean(zc * zc, axis=-1, keepdims=True)
    return zc * lax.rsqrt(var + NORM_EPS) * g + b


def _tree_sum(items):
    while len(items) > 1:
        items = [a + b for a, b in zip(items[0::2], items[1::2])] + (items[-1:] if len(items) % 2 else [])
    return items[0]


def _first_max(items):
    while len(items) > 1:
        nxt = [tuple(jnp.where(a[0] >= b[0], x, y) for x, y in zip(a, b))
               for a, b in zip(items[0::2], items[1::2])]
        items = nxt + (items[-1:] if len(items) % 2 else [])
    return items[0]


def _route_rows(lg, rb):
    sc = 1.0 / (1.0 + jnp.exp(-lg))
    sel = sc + rb
    rows = [sel[e:e + 1, :] for e in range(N_EXPERTS)]
    scr = [sc[e:e + 1, :] for e in range(N_EXPERTS)]
    groups = []
    for g in range(N_EXPERT_GROUPS):
        a, b, c, d = rows[g * EXPERTS_PER_GROUP:(g + 1) * EXPERTS_PER_GROUP]
        hi1, lo1 = jnp.maximum(a, b), jnp.minimum(a, b)
        hi2, lo2 = jnp.maximum(c, d), jnp.minimum(c, d)
        top1 = jnp.maximum(hi1, hi2)
        top2 = jnp.maximum(jnp.minimum(hi1, hi2), jnp.maximum(lo1, lo2))
        members = tuple(rows[g * EXPERTS_PER_GROUP + j] for j in range(EXPERTS_PER_GROUP))
        scores = tuple(scr[g * EXPERTS_PER_GROUP + j] for j in range(EXPERTS_PER_GROUP))
        groups.append((top1 + top2, jnp.full_like(top1, g * EXPERTS_PER_GROUP)) + members + scores)
    best = _first_max(groups)
    first_id = best[1]
    cand = [(best[2 + j], best[2 + EXPERTS_PER_GROUP + j], first_id + np.float32(j))
            for j in range(EXPERTS_PER_GROUP)]
    _, g0, idx0 = _first_max(cand)
    neg = np.float32(-np.inf)
    cand2 = [(jnp.where(c[2] == idx0, neg, c[0]), c[1], c[2]) for c in cand]
    _, g1, idx1 = _first_max(cand2)
    tot = g0 + g1
    return idx0, idx1, g0 / tot, g1 / tot


def _outproj_body(ap_ref, as_ref, gp_ref, gs_ref, xp_ref, xs_ref, wo_ref, g1_ref, b1_ref, rwt_ref, rb_ref,
                  x1_ref, mi_ref, mf_ref, cnt_ref, carry, *, alpha, n_first):
    i = pl.program_id(0)
    tm = xp_ref.shape[0]

    @pl.when(i == 0)
    def _():
        carry[...] = jnp.zeros_like(carry)

    y = jnp.dot(_pick_rows(ap_ref, as_ref, n_first), wo_ref[0, 0:ATTN_WIDTH, :], preferred_element_type=F32)
    y = y + jnp.dot(_pick_rows(gp_ref, gs_ref, n_first), wo_ref[0, ATTN_WIDTH:ATTN_WIDTH + GM_WIDTH, :],
                    preferred_element_type=F32)
    x1 = _layer_norm_rows(np.float32(alpha) * _pick_rows(xp_ref, xs_ref, n_first) + y,
                          g1_ref[...], b1_ref[...])
    x1_ref[...] = x1

    lg = lax.dot_general(rwt_ref[...], x1.astype(BF16), (((1,), (1,)), ((), ())),
                         preferred_element_type=F32)
    rb = jnp.concatenate([rb_ref[...]] * (tm // LANES), axis=1)
    idx0, idx1, gate0, gate1 = _route_rows(lg, rb)

    one, zero = np.float32(1.0), np.float32(0.0)
    hit0 = [idx0 == np.float32(e) for e in range(N_EXPERTS)]
    hit1 = [idx1 == np.float32(e) for e in range(N_EXPERTS)]
    chosen = jnp.concatenate(
        [jnp.where(jnp.logical_or(hit0[e], hit1[e]), one, zero) for e in range(N_EXPERTS)], axis=0)
    s_idx = lax.broadcasted_iota(jnp.int32, (tm, tm), 0)
    t_idx = lax.broadcasted_iota(jnp.int32, (tm, tm), 1)
    before = jnp.where(s_idx < t_idx, one, zero).astype(BF16)
    base = carry[...]
    rank = jnp.dot(chosen.astype(BF16), before, preferred_element_type=F32) \
        + jnp.concatenate([base] * (tm // LANES), axis=1)
    carry[...] = base + jnp.sum(chosen, axis=1, keepdims=True)
    cnt_ref[...] = carry[...]

    rank_rows = [rank[e:e + 1, :] for e in range(N_EXPERTS)]
    rank0 = _tree_sum([jnp.where(hit0[e], rank_rows[e], zero) for e in range(N_EXPERTS)])
    rank1 = _tree_sum([jnp.where(hit1[e], rank_rows[e], zero) for e in range(N_EXPERTS)])
    zr = jnp.zeros((1, tm), F32)
    mi_ref[...] = jnp.concatenate([idx0, idx1, rank0, rank1, zr, zr, zr, zr], axis=0).astype(jnp.int32)
    mf_ref[...] = jnp.concatenate([gate0, gate1, zr, zr, zr, zr, zr, zr], axis=0)


def _outproj(attn_p, attn_s, gm_p, gm_s, xp, xs, wo, layer, g1, b1, rwt, rb, alpha, tm):
    d = xp.shape[1]
    t_all = xp.shape[0] + xs.shape[0]
    n_first = xp.shape[0] // tm
    row = lambda i: (i, 0)
    col = lambda i: (0, i)
    const = lambda i: (0, 0)
    kernel = functools.partial(_outproj_body, alpha=alpha, n_first=n_first)
    return pl.pallas_call(
        kernel,
        grid=(t_all // tm,),
        in_specs=_split_specs((tm, ATTN_WIDTH), n_first) + _split_specs((tm, GM_WIDTH), n_first)
        + _split_specs((tm, d), n_first) + [
            _resident((1, ATTN_WIDTH + GM_WIDTH, d), lambda i: (layer, 0, 0)),
            _resident((1, d), const),
            _resident((1, d), const),
            _resident((N_EXPERTS, d), const),
            _resident((N_EXPERTS, LANES), const),
        ],
        out_specs=(
            pl.BlockSpec((tm, d), row),
            pl.BlockSpec((8, tm), col),
            pl.BlockSpec((8, tm), col),
            pl.BlockSpec((N_EXPERTS, LANES), const),
        ),
        out_shape=(
            jax.ShapeDtypeStruct((t_all, d), F32),
            jax.ShapeDtypeStruct((8, t_all), jnp.int32),
            jax.ShapeDtypeStruct((8, t_all), F32),
            jax.ShapeDtypeStruct((N_EXPERTS, LANES), F32),
        ),
        scratch_shapes=[pltpu.VMEM((N_EXPERTS, LANES), F32)],
        compiler_params=_cparams(("arbitrary",)),
        name="outproj_router",
    )(attn_p, attn_s, gm_p, gm_s, xp, xs, wo, g1, b1, rwt, rb)


def _dispatch_body(pz_ref, pe_ref, nu_ref, dest_ref, x1_ref, rows_hbm, x1s, zbuf, sem, zsem, *, tb, n_blocks):
    i = pl.program_id(0)
    n = pl.num_programs(0)
    slot = i % 2

    def wait_tile(s):
        for _ in range(2):
            pltpu.make_async_copy(x1s.at[s], rows_hbm.at[pl.ds(0, tb), :], sem.at[s]).wait()

    x1s[slot] = x1_ref[...]

    for s in range(2):
        @pl.when(slot == s)
        def _(s=s):
            for r in range(tb):
                for k in (2, 3):
                    pltpu.make_async_copy(x1s.at[s, pl.ds(r, 1), :],
                                          rows_hbm.at[pl.ds(dest_ref[k, r], 1), :], sem.at[s]).start()

    @pl.when(i > 0)
    def _():
        wait_tile(1 - slot)

    @pl.when(i == n - 1)
    def _():
        wait_tile(slot)
        zbuf[...] = jnp.zeros_like(zbuf)

        def row_copy(r):
            return pltpu.make_async_copy(zbuf.at[pl.ds(0, 1), :], rows_hbm.at[pl.ds(r, 1), :], zsem.at[0])

        def block_copy(j):
            r0 = pl.multiple_of(j * EXPERT_ROWS, EXPERT_ROWS)
            return pltpu.make_async_copy(zbuf, rows_hbm.at[pl.ds(r0, EXPERT_ROWS), :], zsem.at[0])

        for e in range(N_EXPERTS):
            lax.fori_loop(pz_ref[e], pe_ref[e], lambda r, c: (row_copy(r).start(), c)[1], 0)
        lax.fori_loop(nu_ref[0], n_blocks, lambda j, c: (block_copy(j).start(), c)[1], 0)
        for e in range(N_EXPERTS):
            lax.fori_loop(pz_ref[e], pe_ref[e], lambda r, c: (row_copy(r).wait(), c)[1], 0)
        lax.fori_loop(nu_ref[0], n_blocks, lambda j, c: (block_copy(j).wait(), c)[1], 0)


def _dispatch(pad_lo, pad_hi, n_used, dest, x1, n_blocks, tb):
    t_all, d = x1.shape
    kernel = functools.partial(_dispatch_body, tb=tb, n_blocks=n_blocks)
    grid_spec = pltpu.PrefetchScalarGridSpec(
        num_scalar_prefetch=3,
        grid=(t_all // tb,),
        in_specs=[
            pl.BlockSpec((8, tb), lambda i, pz, pe, nu: (0, i), memory_space=pltpu.SMEM),
            pl.BlockSpec((tb, d), lambda i, pz, pe, nu: (i, 0)),
        ],
        out_specs=pl.BlockSpec(memory_space=pl.ANY),
        scratch_shapes=[pltpu.VMEM((2, tb, d), F32), pltpu.VMEM((EXPERT_ROWS, d), F32),
                        pltpu.SemaphoreType.DMA((2,)), pltpu.SemaphoreType.DMA((1,))],
    )
    return pl.pallas_call(
        kernel,
        grid_spec=grid_spec,
        out_shape=jax.ShapeDtypeStruct((n_blocks * EXPERT_ROWS, d), F32),
        compiler_params=_cparams(("arbitrary",)),
        name="dispatch",
    )(pad_lo, pad_hi, n_used, dest, x1)


def _expert_body(be_ref, nu_ref, x_ref, wg_ref, wu_ref, wd_ref, o_ref, wg_s, wu_s, wd_s):
    j = pl.program_id(0)

    @pl.when(jnp.logical_or(j == 0, be_ref[j] != be_ref[jnp.maximum(j - 1, 0)]))
    def _():
        wg_s[...] = wg_ref[0, 0].astype(BF16)
        wu_s[...] = wu_ref[0, 0].astype(BF16)
        wd_s[...] = wd_ref[0, 0].astype(BF16)

    @pl.when(j < nu_ref[0])
    def _():
        xb = x_ref[...].astype(BF16)
        gate = jnp.dot(xb, wg_s[...], preferred_element_type=F32)
        up = jnp.dot(xb, wu_s[...], preferred_element_type=F32)
        h = gate * (1.0 / (1.0 + jnp.exp(-gate))) * up
        o_ref[...] = jnp.dot(h.astype(BF16), wd_s[...], preferred_element_type=F32)

    @pl.when(j >= nu_ref[0])
    def _():
        o_ref[...] = jnp.zeros_like(o_ref)


def _experts(block_expert, n_used, rows, wg, wu, wd, layer, n_blocks):
    d = rows.shape[1]
    de = wg.shape[3]
    grid_spec = pltpu.PrefetchScalarGridSpec(
        num_scalar_prefetch=2,
        grid=(n_blocks,),
        in_specs=[
            pl.BlockSpec((EXPERT_ROWS, d), lambda j, be, nu: (jnp.minimum(j, nu[0] - 1), 0)),
            pl.BlockSpec((1, 1, d, de), lambda j, be, nu: (layer, be[j], 0, 0)),
            pl.BlockSpec((1, 1, d, de), lambda j, be, nu: (layer, be[j], 0, 0)),
            pl.BlockSpec((1, 1, de, d), lambda j, be, nu: (layer, be[j], 0, 0)),
        ],
        out_specs=pl.BlockSpec((EXPERT_ROWS, d), lambda j, be, nu: (j, 0)),
        scratch_shapes=[pltpu.VMEM((d, de), BF16), pltpu.VMEM((d, de), BF16), pltpu.VMEM((de, d), BF16)],
    )
    return pl.pallas_call(
        _expert_body,
        grid_spec=grid_spec,
        out_shape=jax.ShapeDtypeStruct(rows.shape, F32),
        compiler_params=_cparams(("arbitrary",)),
        name="experts",
    )(block_expert, n_used, rows, wg, wu, wd)


def _combine_body(d0_ref, dn_ref, x1_ref, gt_ref, rows_hbm, g2_ref, b2_ref, x2_ref, rbuf, sem, *, alpha):
    i = pl.program_id(0)
    n = pl.num_programs(0)
    tm = x1_ref.shape[0]
    slot = i % 2

    def row_copy(src_row, s, k, r):
        return pltpu.make_async_copy(rows_hbm.at[pl.ds(src_row, 1), :],
                                     rbuf.at[s, k, pl.ds(r, 1), :], sem.at[s])

    @pl.when(i == 0)
    def _():
        def body(r, c):
            for k in range(2):
                row_copy(d0_ref[2 + k, r], 0, k, r).start()
            return c
        lax.fori_loop(0, tm, body, 0, unroll=8)

    def wait_slot(s):
        for k in range(2):
            pltpu.make_async_copy(rows_hbm.at[pl.ds(0, tm), :], rbuf.at[s, k], sem.at[s]).wait()

    for s in range(2):
        @pl.when(jnp.logical_and(i + 1 < n, 1 - slot == s))
        def _(s=s):
            for r in range(tm):
                for k in range(2):
                    row_copy(dn_ref[2 + k, r], s, k, r).start()

    wait_slot(slot)
    y = rbuf[slot, 0] * gt_ref[:, 0:1] + rbuf[slot, 1] * gt_ref[:, 1:2]
    x2_ref[...] = _layer_norm_rows(np.float32(alpha) * x1_ref[...] + y, g2_ref[...], b2_ref[...])


def _combine(meta_i, x1, gates_t, out_rows, g2, b2, alpha, tm, tile0, n_tiles):
    d = x1.shape[1]
    kernel = functools.partial(_combine_body, alpha=alpha)
    return pl.pallas_call(
        kernel,
        grid=(n_tiles,),
        in_specs=[
            pl.BlockSpec((8, tm), lambda i: (0, tile0), memory_space=pltpu.SMEM),
            pl.BlockSpec((8, tm), lambda i: (0, tile0 + jnp.minimum(i + 1, n_tiles - 1)),
                         memory_space=pltpu.SMEM),
            pl.BlockSpec((tm, d), lambda i: (tile0 + i, 0)),
            pl.BlockSpec((tm, 2), lambda i: (tile0 + i, 0)),
            pl.BlockSpec(memory_space=pl.ANY),
            _resident((1, d), lambda i: (0, 0)),
            _resident((1, d), lambda i: (0, 0)),
        ],
        out_specs=pl.BlockSpec((tm, d), lambda i: (i, 0)),
        out_shape=jax.ShapeDtypeStruct((n_tiles * tm, d), F32),
        scratch_shapes=[pltpu.VMEM((2, 2, tm, d), F32), pltpu.SemaphoreType.DMA((2,))],
        compiler_params=_cparams(("arbitrary",)),
        name="combine_ln",
    )(meta_i, meta_i, x1, gates_t, out_rows, g2, b2)


def _rope_table(pos):
    half = ROT_DIM // 2
    inv_freq = jnp.power(ROPE_THETA, -2.0 * jnp.arange(half, dtype=F32) / ROT_DIM)
    ang = pos.astype(F32)[:, None] * inv_freq[None, :]
    cos, sin = jnp.cos(ang), jnp.sin(ang)
    n = pos.shape[0]
    ones = jnp.ones((n, HEAD_DIM - ROT_DIM), F32)
    zeros = jnp.zeros((n, HEAD_DIM - ROT_DIM), F32)
    zh = jnp.zeros((n, half), F32)
    c = jnp.concatenate([cos, cos, ones], axis=1)
    s1 = jnp.concatenate([-sin, zh, zeros], axis=1)
    s2 = jnp.concatenate([zh, sin, zeros], axis=1)
    rep = LANES // HEAD_DIM
    return jnp.concatenate([jnp.tile(c, (1, rep)), jnp.tile(s1, (1, rep)), jnp.tile(s2, (1, rep))], axis=1)


def _block_plan(meta_i, counts, n_blocks):
    cnt = counts[:, 0].astype(jnp.int32)
    padded = (cnt + EXPERT_ROWS - 1) // EXPERT_ROWS * EXPERT_ROWS
    pend = jnp.cumsum(padded)
    pstart = pend - padded
    n_used = pend[-1] // EXPERT_ROWS
    blk = jnp.minimum(jnp.arange(n_blocks, dtype=jnp.int32), n_used - 1)
    block_expert = jnp.sum((blk[:, None] * EXPERT_ROWS >= pend[None, :]).astype(jnp.int32), axis=1)
    experts = jnp.arange(N_EXPERTS, dtype=jnp.int32)[None, :, None]
    start_of = jnp.sum(jnp.where(meta_i[0:2, None, :] == experts, pstart[None, :, None], 0), axis=1)
    dest = jnp.concatenate([meta_i[0:2], start_of + meta_i[2:4], meta_i[4:8]], axis=0)
    return block_expert, n_used.reshape(1), pstart + cnt, pend, dest


def kernel(x_prompt, x_sample, cache_k, cache_v, w_in, w_o, attn_sinks, attn_out_g, gm_out_g,
           gm_ln_g, gm_ln_b, gm_ws, gm_bs, ln1_g, ln1_b, ln2_g, ln2_b, router_w, router_b,
           w_gate, w_up, w_down):
    depth = w_in.shape[0]
    batch, seq, d = x_prompt.shape
    dbatch, ds, _ = x_sample.shape
    alpha = (2 * depth) ** 0.25
    t_p, t_s = batch * seq, dbatch * ds
    t_all = t_p + t_s
    tm = 256 if (t_p % 256 == 0 and t_s % 256 == 0) else 128
    tm_route = 512 if (t_p % 512 == 0 and t_s % 512 == 0) else tm
    w_in_b, w_o_b = w_in.astype(BF16), w_o.astype(BF16)
    tq = 512 if seq % 512 == 0 else seq
    assert seq % GM_CHUNK == 0 and seq % tq == 0 and t_all % tm == 0 and ds <= CHUNK
    gm_tile_p = 4 * GM_CHUNK if seq % (4 * GM_CHUNK) == 0 else GM_CHUNK
    assert t_p % (dbatch * ds) == 0
    n_blocks = -(-2 * t_all // EXPERT_ROWS) + N_EXPERTS

    xp, xs = x_prompt.reshape(t_p, d), x_sample.reshape(t_s, d)
    rope_tab = jnp.concatenate(
        [jnp.tile(_rope_table(jnp.arange(seq)), (batch, 1)),
         jnp.tile(_rope_table(PAST_LEN + jnp.arange(ds)), (dbatch, 1))], axis=0)
    rwt = router_w.T.astype(BF16)
    rb = jnp.broadcast_to(router_b.astype(F32)[:, None], (N_EXPERTS, LANES))
    cache_k = cache_k.reshape(depth, dbatch, WINDOW, KV_WIDTH)
    cache_v = cache_v.reshape(depth, dbatch, WINDOW, KV_WIDTH)

    kp_out, vp_out, ks_out, vs_out, gv_out = [], [], [], [], []
    for l in range(depth):
        q, k32, v32, kp, vp, gu, vn = _inproj(
            xp, xs, w_in_b, l, rope_tab,
            gm_ln_g[l].reshape(1, GM_WIDTH), gm_ln_b[l].reshape(1, GM_WIDTH), tm)
        g_attn = attn_out_g[l].reshape(1, ATTN_WIDTH)
        g_gm = gm_out_g[l].reshape(1, GM_WIDTH)
        attn_p = _attn_prompt(attn_sinks[l], q, kp, vp, g_attn, batch, seq, tq)
        attn_s = _attn_sample(attn_sinks[l], q, k32, v32, cache_k[l], cache_v[l], g_attn, t_p, dbatch, ds)
        bias_full = jnp.repeat(gm_bs[l].T, GM_GROUP_DIM, axis=1)
        gm_p = _gmlp(gu, vn, gm_ws[l], bias_full, g_gm, 0, t_p, GM_CHUNK, gm_tile_p // GM_CHUNK)
        gm_s = _gmlp(gu, vn, gm_ws[l], bias_full, g_gm, t_p, t_s, ds, dbatch)
        x1, meta_i, meta_f, counts = _outproj(
            attn_p, attn_s, gm_p, gm_s, xp, xs, w_o_b, l,
            ln1_g[l].reshape(1, d), ln1_b[l].reshape(1, d), rwt, rb, alpha, tm_route)
        block_expert, n_used, pad_lo, pad_hi, dest = _block_plan(meta_i, counts, n_blocks)
        rows = _dispatch(pad_lo, pad_hi, n_used, dest, x1, n_blocks, tm)
        out_rows = _experts(block_expert, n_used, rows, w_gate, w_up, w_down, l, n_blocks)
        gates_t = meta_f[0:2].T
        g2, b2 = ln2_g[l].reshape(1, d), ln2_b[l].reshape(1, d)
        xp = _combine(dest, x1, gates_t, out_rows, g2, b2, alpha, tm, 0, t_p // tm)
        xs = _combine(dest, x1, gates_t, out_rows, g2, b2, alpha, tm, t_p // tm, t_s // tm)

        def last_window(a):
            return a[:t_p].reshape(batch, seq, KV_WIDTH)[:, seq - WINDOW:] \
                .reshape(batch, WINDOW, N_KV_HEADS, HEAD_DIM)

        kp_out.append(last_window(k32))
        vp_out.append(last_window(v32))
        k_s = k32[t_p:].reshape(dbatch, ds, KV_WIDTH)
        v_s = v32[t_p:].reshape(dbatch, ds, KV_WIDTH)
        ks_out.append(jnp.concatenate([cache_k[l][:, ds:], k_s], axis=1)
                      .reshape(dbatch, WINDOW, N_KV_HEADS, HEAD_DIM))
        vs_out.append(jnp.concatenate([cache_v[l][:, ds:], v_s], axis=1)
                      .reshape(dbatch, WINDOW, N_KV_HEADS, HEAD_DIM))
        gv_out.append(vn[t_p:].reshape(dbatch, ds, GM_GROUPS, GM_GROUP_DIM))

    return (xp.reshape(batch, seq, d), xs.reshape(dbatch, ds, d), jnp.stack(kp_out), jnp.stack(vp_out), jnp.stack(ks_out),
            jnp.stack(vs_out), jnp.stack(gv_out))
```

```python
import functools

import jax
import jax.numpy as jnp
import numpy as np
from jax import lax
from jax.experimental import pallas as pl
from jax.experimental.pallas import tpu as pltpu

F32 = jnp.float32
BF16 = jnp.bfloat16

CHUNK = 64
N_HEADS = 16
N_KV_HEADS = 4
HEAD_DIM = 64
Q_PER_KV = N_HEADS // N_KV_HEADS
ATTN_WIDTH = N_HEADS * HEAD_DIM
KV_WIDTH = N_KV_HEADS * HEAD_DIM
WINDOW = 128
ROT_DIM = HEAD_DIM // 4
ROPE_THETA = 500000.0
GM_CHUNK = 128
GM_GROUPS = 8
GM_GROUP_DIM = 128
GM_WIDTH = GM_GROUPS * GM_GROUP_DIM
N_EXPERTS = 16
N_EXPERT_GROUPS = 4
EXPERTS_PER_GROUP = N_EXPERTS // N_EXPERT_GROUPS
NORM_EPS = 1e-5
NEG_INF = -1e30
LOG2_E = 1.4426950408889634
PAST_LEN = 4096

LANES = 128
KEY_BLOCK = 256
Q_BLOCK = 128
EXPERT_ROWS = 256
VMEM_LIMIT = 56 * 1024 * 1024


def _cparams(sem):
    return pltpu.CompilerParams(dimension_semantics=sem, vmem_limit_bytes=VMEM_LIMIT)


def _resident(shape, index_map):
    return pl.BlockSpec(shape, index_map, pipeline_mode=pl.Buffered(1))


def _gelu(a):
    return 0.5 * a * (1.0 + lax.erf(a * np.float32(0.7071067811865476)))


def _pair_expand(t):
    r = pltpu.roll(t, HEAD_DIM, 1)
    lane = lax.broadcasted_iota(jnp.int32, t.shape, 1)
    lo = lane < HEAD_DIM
    return jnp.where(lo, t, r), jnp.where(lo, r, t)


def _expand_kv(a):
    a0, b0 = _pair_expand(a[:, 0:LANES])
    a1, b1 = _pair_expand(a[:, LANES:2 * LANES])
    return jnp.concatenate([a0, b0, a1, b1], axis=1)


def _pick_rows(first_ref, second_ref, n_first):
    return jnp.where(pl.program_id(0) < n_first, first_ref[...], second_ref[...])


def _split_specs(block, n_first):
    return [pl.BlockSpec(block, lambda i: (jnp.minimum(i, n_first - 1), 0)),
            pl.BlockSpec(block, lambda i: (jnp.maximum(i - n_first, 0), 0))]


def _inproj_body(xp_ref, xs_ref, w_ref, rope_ref, lng_ref, lnb_ref,
                 q_ref, k_ref, v_ref, kp_ref, vp_ref, gu_ref, vn_ref, *, n_first):
    xb = _pick_rows(xp_ref, xs_ref, n_first).astype(BF16)
    cs = rope_ref[:, 0:LANES]
    s1 = rope_ref[:, LANES:2 * LANES]
    s2 = rope_ref[:, 2 * LANES:3 * LANES]

    def rope(a):
        n = a.shape[1] // LANES
        width = a.shape[1]
        c = jnp.concatenate([cs] * n, axis=1)
        m1 = jnp.concatenate([s1] * n, axis=1)
        m2 = jnp.concatenate([s2] * n, axis=1)
        return a * c + pltpu.roll(a, width - ROT_DIM // 2, 1) * m1 + pltpu.roll(a, ROT_DIM // 2, 1) * m2

    def proj(c0, c1):
        return jnp.dot(xb, w_ref[0, :, c0:c1], preferred_element_type=F32)

    half = ATTN_WIDTH // 2
    for h in range(2):
        q = rope(proj(h * half, (h + 1) * half)) * np.float32(HEAD_DIM ** -0.5 * LOG2_E)
        q_ref[:, h * half:(h + 1) * half] = q.astype(BF16)

    kv = proj(ATTN_WIDTH, ATTN_WIDTH + 2 * KV_WIDTH)
    k = rope(kv[:, 0:KV_WIDTH])
    v = kv[:, KV_WIDTH:2 * KV_WIDTH]
    k_ref[...] = k
    v_ref[...] = v
    kp_ref[...] = _expand_kv(k).astype(BF16)
    vp_ref[...] = _expand_kv(v).astype(BF16)

    base = ATTN_WIDTH + 2 * KV_WIDTH
    for h in range(2):
        gu_ref[:, h * half:(h + 1) * half] = _gelu(proj(base + h * half, base + (h + 1) * half))

    base = base + GM_WIDTH
    for h in range(2):
        gv = _gelu(proj(base + h * half, base + (h + 1) * half))
        for g in range(half // GM_GROUP_DIM):
            t = gv[:, g * GM_GROUP_DIM:(g + 1) * GM_GROUP_DIM]
            col = h * half + g * GM_GROUP_DIM
            tc = t - jnp.mean(t, axis=-1, keepdims=True)
            var = jnp.mean(tc * tc, axis=-1, keepdims=True)
            y = tc * lax.rsqrt(var + NORM_EPS) * lng_ref[:, col:col + GM_GROUP_DIM] \
                + lnb_ref[:, col:col + GM_GROUP_DIM]
            vn_ref[:, col:col + GM_GROUP_DIM] = y


def _inproj(xp, xs, w, layer, rope_tab, lng, lnb, tm):
    d = xp.shape[1]
    t_all = xp.shape[0] + xs.shape[0]
    n_first = xp.shape[0] // tm
    n_in = w.shape[2]
    row = lambda i: (i, 0)
    const = lambda i: (0, 0)
    out_shape = (
        jax.ShapeDtypeStruct((t_all, ATTN_WIDTH), BF16),
        jax.ShapeDtypeStruct((t_all, KV_WIDTH), F32),
        jax.ShapeDtypeStruct((t_all, KV_WIDTH), F32),
        jax.ShapeDtypeStruct((t_all, 2 * KV_WIDTH), BF16),
        jax.ShapeDtypeStruct((t_all, 2 * KV_WIDTH), BF16),
        jax.ShapeDtypeStruct((t_all, GM_WIDTH), F32),
        jax.ShapeDtypeStruct((t_all, GM_WIDTH), F32),
    )
    return pl.pallas_call(
        functools.partial(_inproj_body, n_first=n_first),
        grid=(t_all // tm,),
        in_specs=_split_specs((tm, d), n_first) + [
            _resident((1, d, n_in), lambda i: (layer, 0, 0)),
            pl.BlockSpec((tm, 3 * LANES), row),
            _resident((None, 1, GM_WIDTH), lambda i: (layer, 0, 0)),
            _resident((None, 1, GM_WIDTH), lambda i: (layer, 0, 0)),
        ],
        out_specs=(
            pl.BlockSpec((tm, ATTN_WIDTH), row),
            pl.BlockSpec((tm, KV_WIDTH), row),
            pl.BlockSpec((tm, KV_WIDTH), row),
            pl.BlockSpec((tm, 2 * KV_WIDTH), row),
            pl.BlockSpec((tm, 2 * KV_WIDTH), row),
            pl.BlockSpec((tm, GM_WIDTH), row),
            pl.BlockSpec((tm, GM_WIDTH), row),
        ),
        out_shape=out_shape,
        compiler_params=_cparams(("parallel",)),
        name="inproj",
    )(xp, xs, w, rope_tab, lng, lnb)


def _attn_group(qg, kexp, vexp, bias, sinks):
    nk = kexp.shape[0]
    seg = lax.broadcasted_iota(jnp.int32, kexp.shape, 1) // HEAD_DIM
    zero = jnp.zeros_like(kexp)
    kbd = jnp.concatenate([jnp.where(seg == j, kexp, zero) for j in range(Q_PER_KV)], axis=0)
    vbd = jnp.concatenate([jnp.where(seg == j, vexp, zero) for j in range(Q_PER_KV)], axis=0)
    s = lax.dot_general(qg, kbd, (((1,), (1,)), ((), ())), preferred_element_type=F32)
    es, rs = [], []
    for j in range(Q_PER_KV):
        sj = s[:, j * nk:(j + 1) * nk] + bias
        sink = sinks[j] * np.float32(LOG2_E)
        m = jnp.maximum(jnp.max(sj, axis=-1, keepdims=True), sink)
        e = jnp.exp2(sj - m)
        den = jnp.sum(e, axis=-1, keepdims=True) + jnp.exp2(sink - m)
        es.append(e.astype(BF16))
        rs.append(1.0 / den)
    p = jnp.concatenate(es, axis=1)
    o = jnp.dot(p, vbd, preferred_element_type=F32)
    oseg = lax.broadcasted_iota(jnp.int32, o.shape, 1) // HEAD_DIM
    r = jnp.where(oseg == 0, rs[0], jnp.where(oseg == 1, rs[1], jnp.where(oseg == 2, rs[2], rs[3])))
    return o * r


def _attn_rows(q, kbuf, vbuf, bias, sink_ref, g_row):
    outs = []
    for h in range(N_KV_HEADS):
        kt = kbuf[:, h * LANES:(h + 1) * LANES]
        vt = vbuf[:, h * LANES:(h + 1) * LANES]
        kexp = jnp.concatenate([kt, kt], axis=1)
        vexp = jnp.concatenate([vt, vt], axis=1)
        sinks = [sink_ref[h * Q_PER_KV + j] for j in range(Q_PER_KV)]
        width = Q_PER_KV * HEAD_DIM
        outs.append(_attn_group(q[:, h * width:(h + 1) * width], kexp, vexp, bias, sinks))
    o = jnp.concatenate(outs, axis=1)
    ms = jnp.mean(o * o, axis=-1, keepdims=True)
    return o * lax.rsqrt(ms + NORM_EPS) * g_row


def _attn_prompt_body(sink_ref, q_ref, kc_ref, kprev_ref, vc_ref, vprev_ref, g_ref, o_ref,
                      kbuf, vbuf, *, tq):
    i = pl.program_id(1)
    kbuf[0:WINDOW, :] = kprev_ref[...]
    kbuf[WINDOW:WINDOW + tq, :] = kc_ref[...]
    vbuf[0:WINDOW, :] = vprev_ref[...]
    vbuf[WINDOW:WINDOW + tq, :] = vc_ref[...]
    rq = lax.broadcasted_iota(jnp.int32, (Q_BLOCK, KEY_BLOCK), 0) // CHUNK
    ck = lax.broadcasted_iota(jnp.int32, (Q_BLOCK, KEY_BLOCK), 1) // CHUNK
    rel = ck - rq
    band = (rel >= 0) & (rel <= WINDOW // CHUNK)
    g_row = g_ref[...]

    def sub(s, carry):
        r0 = pl.multiple_of(s * Q_BLOCK, Q_BLOCK)
        first = jnp.logical_and(i == 0, s == 0)
        kmin = jnp.where(first, WINDOW // CHUNK, 0)
        ok = jnp.logical_and(band, ck >= kmin)
        bias = jnp.where(ok, np.float32(0.0), np.float32(NEG_INF))
        q = q_ref[pl.ds(r0, Q_BLOCK), :]
        kb = kbuf[pl.ds(r0, KEY_BLOCK), :]
        vb = vbuf[pl.ds(r0, KEY_BLOCK), :]
        o_ref[pl.ds(r0, Q_BLOCK), :] = _attn_rows(q, kb, vb, bias, sink_ref, g_row).astype(BF16)
        return carry

    lax.fori_loop(0, tq // Q_BLOCK, sub, 0)


def _attn_prompt(sinks, q, kp, vp, g_attn, layer, batch, seq, tq):
    nq = seq // tq
    per = tq // WINDOW
    cur = lambda b, i: (b * nq + i, 0)
    prev = lambda b, i: (b * (seq // WINDOW) + jnp.maximum(i * per - 1, 0), 0)
    kernel = functools.partial(_attn_prompt_body, tq=tq)
    return pl.pallas_call(
        kernel,
        grid=(batch, nq),
        in_specs=[
            pl.BlockSpec(memory_space=pltpu.SMEM),
            pl.BlockSpec((tq, ATTN_WIDTH), cur),
            pl.BlockSpec((tq, 2 * KV_WIDTH), cur),
            pl.BlockSpec((WINDOW, 2 * KV_WIDTH), prev),
            pl.BlockSpec((tq, 2 * KV_WIDTH), cur),
            pl.BlockSpec((WINDOW, 2 * KV_WIDTH), prev),
            pl.BlockSpec((None, 1, ATTN_WIDTH), lambda b, i: (layer, 0, 0)),
        ],
        out_specs=pl.BlockSpec((tq, ATTN_WIDTH), cur),
        out_shape=jax.ShapeDtypeStruct((batch * seq, ATTN_WIDTH), BF16),
        scratch_shapes=[pltpu.VMEM((WINDOW + tq, 2 * KV_WIDTH), BF16),
                        pltpu.VMEM((WINDOW + tq, 2 * KV_WIDTH), BF16)],
        compiler_params=_cparams(("parallel", "parallel")),
        name="attn_prompt",
    )(sinks, q, kp, kp, vp, vp, g_attn)


def _attn_sample_body(sink_ref, q_ref, ck_ref, cv_ref, kn_ref, vn_ref, g_ref, o_ref, *, ds):
    pad = jnp.zeros((KEY_BLOCK - WINDOW - ds, KV_WIDTH), F32)
    k_all = jnp.concatenate([ck_ref[0], kn_ref[...], pad], axis=0)
    v_all = jnp.concatenate([cv_ref[0], vn_ref[...], pad], axis=0)
    kb = _expand_kv(k_all).astype(BF16)
    vb = _expand_kv(v_all).astype(BF16)
    col = lax.broadcasted_iota(jnp.int32, (ds, KEY_BLOCK), 1)
    bias = jnp.where(col < WINDOW + ds, np.float32(0.0), np.float32(NEG_INF))
    o_ref[...] = _attn_rows(q_ref[...], kb, vb, bias, sink_ref, g_ref[...]).astype(BF16)


def _attn_sample(sinks, q, k32, v32, cache_k, cache_v, g_attn, layer, row0, dbatch, ds):
    off = row0 // ds
    new = lambda b: (off + b, 0)
    kernel = functools.partial(_attn_sample_body, ds=ds)
    return pl.pallas_call(
        kernel,
        grid=(dbatch,),
        in_specs=[
            pl.BlockSpec(memory_space=pltpu.SMEM),
            pl.BlockSpec((ds, ATTN_WIDTH), new),
            pl.BlockSpec((None, 1, WINDOW, KV_WIDTH), lambda b: (layer, b, 0, 0)),
            pl.BlockSpec((None, 1, WINDOW, KV_WIDTH), lambda b: (layer, b, 0, 0)),
            pl.BlockSpec((ds, KV_WIDTH), new),
            pl.BlockSpec((ds, KV_WIDTH), new),
            pl.BlockSpec((None, 1, ATTN_WIDTH), lambda b: (layer, 0, 0)),
        ],
        out_specs=pl.BlockSpec((ds, ATTN_WIDTH), lambda b: (b, 0)),
        out_shape=jax.ShapeDtypeStruct((dbatch * ds, ATTN_WIDTH), BF16),
        compiler_params=_cparams(("parallel",)),
        name="attn_sample",
    )(sinks, q, cache_k, cache_v, k32, v32, g_attn)


def _gmlp_body(gu_ref, vn_ref, ws_ref, bias_ref, g_ref, o_ref, gm_scr, *, rows, nchunk):
    ci = lax.broadcasted_iota(jnp.int32, (GM_CHUNK, GM_CHUNK), 0) // CHUNK
    cj = lax.broadcasted_iota(jnp.int32, (GM_CHUNK, GM_CHUNK), 1) // CHUNK
    visible = ci >= cj
    for g in range(GM_GROUPS):
        lo, hi = g * GM_GROUP_DIM, (g + 1) * GM_GROUP_DIM
        w = jnp.where(visible, ws_ref[g], np.float32(0.0))[0:rows, :].astype(BF16)
        pieces = []
        for n in range(nchunk):
            piece = vn_ref[n * rows:(n + 1) * rows, lo:hi]
            if rows < GM_CHUNK:
                piece = jnp.concatenate([piece, jnp.zeros((GM_CHUNK - rows, GM_GROUP_DIM), F32)], axis=0)
            pieces.append(piece.astype(BF16))
        rhs = jnp.concatenate(pieces, axis=1)
        mix = jnp.dot(w, rhs, preferred_element_type=F32)
        b = bias_ref[0:rows, lo:hi]
        for n in range(nchunk):
            gm_scr[n * rows:(n + 1) * rows, lo:hi] = \
                gu_ref[n * rows:(n + 1) * rows, lo:hi] * (mix[:, n * GM_GROUP_DIM:(n + 1) * GM_GROUP_DIM] + b)
    gm = gm_scr[...]
    ms = jnp.mean(gm * gm, axis=-1, keepdims=True)
    o_ref[...] = (gm * lax.rsqrt(ms + NORM_EPS) * g_ref[...]).astype(BF16)


def _gmlp(gu, vn, ws, bias_full, g_gm, layer, row0, n_rows, rows, nchunk):
    tile = rows * nchunk
    off = row0 // tile
    kernel = functools.partial(_gmlp_body, rows=rows, nchunk=nchunk)
    return pl.pallas_call(
        kernel,
        grid=(n_rows // tile,),
        in_specs=[
            pl.BlockSpec((tile, GM_WIDTH), lambda i: (off + i, 0)),
            pl.BlockSpec((tile, GM_WIDTH), lambda i: (off + i, 0)),
            pl.BlockSpec((None, GM_GROUPS, GM_CHUNK, GM_CHUNK), lambda i: (layer, 0, 0, 0)),
            pl.BlockSpec((None, GM_CHUNK, GM_WIDTH), lambda i: (layer, 0, 0)),
            pl.BlockSpec((None, 1, GM_WIDTH), lambda i: (layer, 0, 0)),
        ],
        out_specs=pl.BlockSpec((tile, GM_WIDTH), lambda i: (i, 0)),
        out_shape=jax.ShapeDtypeStruct((n_rows, GM_WIDTH), BF16),
        scratch_shapes=[pltpu.VMEM((tile, GM_WIDTH), F32)],
        compiler_params=_cparams(("parallel",)),
        name="gmlp",
    )(gu, vn, ws, bias_full, g_gm)


def _layer_norm_rows(z, g, b):
    zc = z - jnp.mean(z, axis=-1, keepdims=True)
    var = jnp.mean(zc * zc, axis=-1, keepdims=True)
    return zc * lax.rsqrt(var + NORM_EPS) * g + b


def _tree_sum(items):
    while len(items) > 1:
        items = [a + b for a, b in zip(items[0::2], items[1::2])] + (items[-1:] if len(items) % 2 else [])
    return items[0]


def _first_max(items):
    while len(items) > 1:
        nxt = [tuple(jnp.where(a[0] >= b[0], x, y) for x, y in zip(a, b))
               for a, b in zip(items[0::2], items[1::2])]
        items = nxt + (items[-1:] if len(items) % 2 else [])
    return items[0]


def _route_rows(lg, rb):
    sc = 1.0 / (1.0 + jnp.exp(-lg))
    sel = sc + rb
    rows = [sel[e:e + 1, :] for e in range(N_EXPERTS)]
    scr = [sc[e:e + 1, :] for e in range(N_EXPERTS)]
    groups = []
    for g in range(N_EXPERT_GROUPS):
        a, b, c, d = rows[g * EXPERTS_PER_GROUP:(g + 1) * EXPERTS_PER_GROUP]
        hi1, lo1 = jnp.maximum(a, b), jnp.minimum(a, b)
        hi2, lo2 = jnp.maximum(c, d), jnp.minimum(c, d)
        top1 = jnp.maximum(hi1, hi2)
        top2 = jnp.maximum(jnp.minimum(hi1, hi2), jnp.maximum(lo1, lo2))
        members = tuple(rows[g * EXPERTS_PER_GROUP + j] for j in range(EXPERTS_PER_GROUP))
        scores = tuple(scr[g * EXPERTS_PER_GROUP + j] for j in range(EXPERTS_PER_GROUP))
        groups.append((top1 + top2, jnp.full_like(top1, g * EXPERTS_PER_GROUP)) + members + scores)
    best = _first_max(groups)
    first_id = best[1]
    cand = [(best[2 + j], best[2 + EXPERTS_PER_GROUP + j], first_id + np.float32(j))
            for j in range(EXPERTS_PER_GROUP)]
    _, g0, idx0 = _first_max(cand)
    neg = np.float32(-np.inf)
    cand2 = [(jnp.where(c[2] == idx0, neg, c[0]), c[1], c[2]) for c in cand]
    _, g1, idx1 = _first_max(cand2)
    tot = g0 + g1
    return idx0, idx1, g0 / tot, g1 / tot


def _outproj_body(ap_ref, as_ref, gp_ref, gs_ref, xp_ref, xs_ref, wo_ref, g1_ref, b1_ref, rwt_ref, rb_ref,
                  x1_ref, mi_ref, mf_ref, cnt_ref, carry, *, alpha, n_first):
    i = pl.program_id(0)
    tm = xp_ref.shape[0]

    @pl.when(i == 0)
    def _():
        carry[...] = jnp.zeros_like(carry)

    y = jnp.dot(_pick_rows(ap_ref, as_ref, n_first), wo_ref[0, 0:ATTN_WIDTH, :], preferred_element_type=F32)
    y = y + jnp.dot(_pick_rows(gp_ref, gs_ref, n_first), wo_ref[0, ATTN_WIDTH:ATTN_WIDTH + GM_WIDTH, :],
                    preferred_element_type=F32)
    x1 = _layer_norm_rows(np.float32(alpha) * _pick_rows(xp_ref, xs_ref, n_first) + y,
                          g1_ref[...], b1_ref[...])
    x1_ref[...] = x1

    lg = lax.dot_general(rwt_ref[...], x1.astype(BF16), (((1,), (1,)), ((), ())),
                         preferred_element_type=F32)
    rb = jnp.concatenate([rb_ref[...]] * (tm // LANES), axis=1)
    idx0, idx1, gate0, gate1 = _route_rows(lg, rb)

    one, zero = np.float32(1.0), np.float32(0.0)
    hit0 = [idx0 == np.float32(e) for e in range(N_EXPERTS)]
    hit1 = [idx1 == np.float32(e) for e in range(N_EXPERTS)]
    chosen = jnp.concatenate(
        [jnp.where(jnp.logical_or(hit0[e], hit1[e]), one, zero) for e in range(N_EXPERTS)], axis=0)
    s_idx = lax.broadcasted_iota(jnp.int32, (tm, tm), 0)
    t_idx = lax.broadcasted_iota(jnp.int32, (tm, tm), 1)
    before = jnp.where(s_idx < t_idx, one, zero).astype(BF16)
    base = carry[...]
    rank = jnp.dot(chosen.astype(BF16), before, preferred_element_type=F32) \
        + jnp.concatenate([base] * (tm // LANES), axis=1)
    carry[...] = base + jnp.sum(chosen, axis=1, keepdims=True)
    cnt_ref[...] = carry[...]

    rank_rows = [rank[e:e + 1, :] for e in range(N_EXPERTS)]
    rank0 = _tree_sum([jnp.where(hit0[e], rank_rows[e], zero) for e in range(N_EXPERTS)])
    rank1 = _tree_sum([jnp.where(hit1[e], rank_rows[e], zero) for e in range(N_EXPERTS)])
    zr = jnp.zeros((1, tm), F32)
    mi_ref[...] = jnp.concatenate([idx0, idx1, rank0, rank1, zr, zr, zr, zr], axis=0).astype(jnp.int32)
    mf_ref[...] = jnp.concatenate([gate0, gate1, zr, zr, zr, zr, zr, zr], axis=0)


def _outproj(attn_p, attn_s, gm_p, gm_s, xp, xs, wo, layer, g1, b1, rwt, rb, alpha, tm):
    d = xp.shape[1]
    t_all = xp.shape[0] + xs.shape[0]
    n_first = xp.shape[0] // tm
    row = lambda i: (i, 0)
    col = lambda i: (0, i)
    const = lambda i: (0, 0)
    kernel = functools.partial(_outproj_body, alpha=alpha, n_first=n_first)
    return pl.pallas_call(
        kernel,
        grid=(t_all // tm,),
        in_specs=_split_specs((tm, ATTN_WIDTH), n_first) + _split_specs((tm, GM_WIDTH), n_first)
        + _split_specs((tm, d), n_first) + [
            _resident((1, ATTN_WIDTH + GM_WIDTH, d), lambda i: (layer, 0, 0)),
            _resident((None, 1, d), lambda i: (layer, 0, 0)),
            _resident((None, 1, d), lambda i: (layer, 0, 0)),
            _resident((N_EXPERTS, d), const),
            _resident((N_EXPERTS, LANES), const),
        ],
        out_specs=(
            pl.BlockSpec((tm, d), row),
            pl.BlockSpec((8, tm), col),
            pl.BlockSpec((8, tm), col),
            pl.BlockSpec((N_EXPERTS, LANES), const),
        ),
        out_shape=(
            jax.ShapeDtypeStruct((t_all, d), F32),
            jax.ShapeDtypeStruct((8, t_all), jnp.int32),
            jax.ShapeDtypeStruct((8, t_all), F32),
            jax.ShapeDtypeStruct((N_EXPERTS, LANES), F32),
        ),
        scratch_shapes=[pltpu.VMEM((N_EXPERTS, LANES), F32)],
        compiler_params=_cparams(("arbitrary",)),
        name="outproj_router",
    )(attn_p, attn_s, gm_p, gm_s, xp, xs, wo, g1, b1, rwt, rb)


def _dispatch_body(pz_ref, pe_ref, nu_ref, dest_ref, x1_ref, rows_hbm, x1s, zbuf, sem, zsem, *, tb, n_blocks):
    i = pl.program_id(0)
    n = pl.num_programs(0)
    slot = i % 2

    def wait_tile(s):
        for _ in range(2):
            pltpu.make_async_copy(x1s.at[s], rows_hbm.at[pl.ds(0, tb), :], sem.at[s]).wait()

    x1s[slot] = x1_ref[...]

    for s in range(2):
        @pl.when(slot == s)
        def _(s=s):
            for r in range(tb):
                for k in (2, 3):
                    pltpu.make_async_copy(x1s.at[s, pl.ds(r, 1), :],
                                          rows_hbm.at[pl.ds(dest_ref[k, r], 1), :], sem.at[s]
                                          ).start(priority=k - 2)

    @pl.when(i > 0)
    def _():
        wait_tile(1 - slot)

    @pl.when(i == n - 1)
    def _():
        wait_tile(slot)
        zbuf[...] = jnp.zeros_like(zbuf)

        def row_copy(r):
            return pltpu.make_async_copy(zbuf.at[pl.ds(0, 1), :], rows_hbm.at[pl.ds(r, 1), :], zsem.at[0])

        def block_copy(j):
            r0 = pl.multiple_of(j * EXPERT_ROWS, EXPERT_ROWS)
            return pltpu.make_async_copy(zbuf, rows_hbm.at[pl.ds(r0, EXPERT_ROWS), :], zsem.at[0])

        for e in range(N_EXPERTS):
            lax.fori_loop(pz_ref[e], pe_ref[e], lambda r, c: (row_copy(r).start(), c)[1], 0)
        lax.fori_loop(nu_ref[0], n_blocks, lambda j, c: (block_copy(j).start(), c)[1], 0)
        for e in range(N_EXPERTS):
            lax.fori_loop(pz_ref[e], pe_ref[e], lambda r, c: (row_copy(r).wait(), c)[1], 0)
        lax.fori_loop(nu_ref[0], n_blocks, lambda j, c: (block_copy(j).wait(), c)[1], 0)


def _dispatch(pad_lo, pad_hi, n_used, dest, x1, n_blocks, tb):
    t_all, d = x1.shape
    kernel = functools.partial(_dispatch_body, tb=tb, n_blocks=n_blocks)
    grid_spec = pltpu.PrefetchScalarGridSpec(
        num_scalar_prefetch=3,
        grid=(t_all // tb,),
        in_specs=[
            pl.BlockSpec((8, tb), lambda i, pz, pe, nu: (0, i), memory_space=pltpu.SMEM),
            pl.BlockSpec((tb, d), lambda i, pz, pe, nu: (i, 0)),
        ],
        out_specs=pl.BlockSpec(memory_space=pl.ANY),
        scratch_shapes=[pltpu.VMEM((2, tb, d), F32), pltpu.VMEM((EXPERT_ROWS, d), F32),
                        pltpu.SemaphoreType.DMA((2,)), pltpu.SemaphoreType.DMA((1,))],
    )
    return pl.pallas_call(
        kernel,
        grid_spec=grid_spec,
        out_shape=jax.ShapeDtypeStruct((n_blocks * EXPERT_ROWS, d), F32),
        compiler_params=_cparams(("arbitrary",)),
        name="dispatch",
    )(pad_lo, pad_hi, n_used, dest, x1)


def _expert_body(be_ref, nu_ref, x_ref, wg_ref, wu_ref, wd_ref, o_ref, wg_s, wu_s, wd_s):
    j = pl.program_id(0)

    @pl.when(jnp.logical_or(j == 0, be_ref[j] != be_ref[jnp.maximum(j - 1, 0)]))
    def _():
        wg_s[...] = wg_ref[0, 0].astype(BF16)
        wu_s[...] = wu_ref[0, 0].astype(BF16)
        wd_s[...] = wd_ref[0, 0].astype(BF16)

    @pl.when(j < nu_ref[0])
    def _():
        xb = x_ref[...].astype(BF16)
        gate = jnp.dot(xb, wg_s[...], preferred_element_type=F32)
        up = jnp.dot(xb, wu_s[...], preferred_element_type=F32)
        h = gate * (1.0 / (1.0 + jnp.exp(-gate))) * up
        o_ref[...] = jnp.dot(h.astype(BF16), wd_s[...], preferred_element_type=F32)

    @pl.when(j >= nu_ref[0])
    def _():
        o_ref[...] = jnp.zeros_like(o_ref)


def _experts(block_expert, n_used, rows, wg, wu, wd, layer, n_blocks):
    d = rows.shape[1]
    de = wg.shape[3]
    grid_spec = pltpu.PrefetchScalarGridSpec(
        num_scalar_prefetch=2,
        grid=(n_blocks,),
        in_specs=[
            pl.BlockSpec((EXPERT_ROWS, d), lambda j, be, nu: (jnp.minimum(j, nu[0] - 1), 0)),
            pl.BlockSpec((1, 1, d, de), lambda j, be, nu: (layer, be[j], 0, 0)),
            pl.BlockSpec((1, 1, d, de), lambda j, be, nu: (layer, be[j], 0, 0)),
            pl.BlockSpec((1, 1, de, d), lambda j, be, nu: (layer, be[j], 0, 0)),
        ],
        out_specs=pl.BlockSpec((EXPERT_ROWS, d), lambda j, be, nu: (j, 0)),
        scratch_shapes=[pltpu.VMEM((d, de), BF16), pltpu.VMEM((d, de), BF16), pltpu.VMEM((de, d), BF16)],
    )
    return pl.pallas_call(
        _expert_body,
        grid_spec=grid_spec,
        out_shape=jax.ShapeDtypeStruct(rows.shape, F32),
        compiler_params=_cparams(("arbitrary",)),
        name="experts",
    )(block_expert, n_used, rows, wg, wu, wd)


def _combine_body(d0_ref, dn_ref, x1_ref, gt_ref, rows_hbm, g2_ref, b2_ref, x2_ref, rbuf, sem, *, alpha):
    i = pl.program_id(0)
    n = pl.num_programs(0)
    tm = x1_ref.shape[0]
    slot = i % 2

    def row_copy(src_row, s, k, r):
        return pltpu.make_async_copy(rows_hbm.at[pl.ds(src_row, 1), :],
                                     rbuf.at[s, k, pl.ds(r, 1), :], sem.at[s])

    @pl.when(i == 0)
    def _():
        def body(r, c):
            for k in range(2):
                row_copy(d0_ref[2 + k, r], 0, k, r).start()
            return c
        lax.fori_loop(0, tm, body, 0, unroll=8)

    def wait_slot(s):
        for k in range(2):
            pltpu.make_async_copy(rows_hbm.at[pl.ds(0, tm), :], rbuf.at[s, k], sem.at[s]).wait()

    for s in range(2):
        @pl.when(jnp.logical_and(i + 1 < n, 1 - slot == s))
        def _(s=s):
            for r in range(tm):
                for k in range(2):
                    row_copy(dn_ref[2 + k, r], s, k, r).start(priority=k)

    wait_slot(slot)
    y = rbuf[slot, 0] * gt_ref[:, 0:1] + rbuf[slot, 1] * gt_ref[:, 1:2]
    x2_ref[...] = _layer_norm_rows(np.float32(alpha) * x1_ref[...] + y, g2_ref[...], b2_ref[...])


def _combine(meta_i, x1, gates_t, out_rows, g2, b2, layer, alpha, tm, tile0, n_tiles):
    d = x1.shape[1]
    kernel = functools.partial(_combine_body, alpha=alpha)
    return pl.pallas_call(
        kernel,
        grid=(n_tiles,),
        in_specs=[
            pl.BlockSpec((8, tm), lambda i: (0, tile0), memory_space=pltpu.SMEM),
            pl.BlockSpec((8, tm), lambda i: (0, tile0 + jnp.minimum(i + 1, n_tiles - 1)),
                         memory_space=pltpu.SMEM),
            pl.BlockSpec((tm, d), lambda i: (tile0 + i, 0)),
            pl.BlockSpec((tm, 2), lambda i: (tile0 + i, 0)),
            pl.BlockSpec(memory_space=pl.ANY),
            _resident((None, 1, d), lambda i: (layer, 0, 0)),
            _resident((None, 1, d), lambda i: (layer, 0, 0)),
        ],
        out_specs=pl.BlockSpec((tm, d), lambda i: (i, 0)),
        out_shape=jax.ShapeDtypeStruct((n_tiles * tm, d), F32),
        scratch_shapes=[pltpu.VMEM((2, 2, tm, d), F32), pltpu.SemaphoreType.DMA((2,))],
        compiler_params=_cparams(("arbitrary",)),
        name="combine_ln",
    )(meta_i, meta_i, x1, gates_t, out_rows, g2, b2)


def _rope_table(pos):
    half = ROT_DIM // 2
    inv_freq = jnp.power(ROPE_THETA, -2.0 * jnp.arange(half, dtype=F32) / ROT_DIM)
    ang = pos.astype(F32)[:, None] * inv_freq[None, :]
    cos, sin = jnp.cos(ang), jnp.sin(ang)
    n = pos.shape[0]
    ones = jnp.ones((n, HEAD_DIM - ROT_DIM), F32)
    zeros = jnp.zeros((n, HEAD_DIM - ROT_DIM), F32)
    zh = jnp.zeros((n, half), F32)
    c = jnp.concatenate([cos, cos, ones], axis=1)
    s1 = jnp.concatenate([-sin, zh, zeros], axis=1)
    s2 = jnp.concatenate([zh, sin, zeros], axis=1)
    rep = LANES // HEAD_DIM
    return jnp.concatenate([jnp.tile(c, (1, rep)), jnp.tile(s1, (1, rep)), jnp.tile(s2, (1, rep))], axis=1)


def _block_plan(meta_i, counts, n_blocks):
    cnt = counts[:, 0].astype(jnp.int32)
    padded = (cnt + EXPERT_ROWS - 1) // EXPERT_ROWS * EXPERT_ROWS
    pend = jnp.cumsum(padded)
    pstart = pend - padded
    n_used = pend[-1] // EXPERT_ROWS
    blk = jnp.minimum(jnp.arange(n_blocks, dtype=jnp.int32), n_used - 1)
    block_expert = jnp.sum((blk[:, None] * EXPERT_ROWS >= pend[None, :]).astype(jnp.int32), axis=1)
    experts = jnp.arange(N_EXPERTS, dtype=jnp.int32)[None, :, None]
    start_of = jnp.sum(jnp.where(meta_i[0:2, None, :] == experts, pstart[None, :, None], 0), axis=1)
    dest = jnp.concatenate([meta_i[0:2], start_of + meta_i[2:4], meta_i[4:8]], axis=0)
    return block_expert, n_used.reshape(1), pstart + cnt, pend, dest


def kernel(x_prompt, x_sample, cache_k, cache_v, w_in, w_o, attn_sinks, attn_out_g, gm_out_g,
           gm_ln_g, gm_ln_b, gm_ws, gm_bs, ln1_g, ln1_b, ln2_g, ln2_b, router_w, router_b,
           w_gate, w_up, w_down):
    depth = w_in.shape[0]
    batch, seq, d = x_prompt.shape
    dbatch, ds, _ = x_sample.shape
    alpha = (2 * depth) ** 0.25
    t_p, t_s = batch * seq, dbatch * ds
    t_all = t_p + t_s
    tm = 256 if (t_p % 256 == 0 and t_s % 256 == 0) else 128
    tm_route = 512 if (t_p % 512 == 0 and t_s % 512 == 0) else tm
    w_in_b, w_o_b = w_in.astype(BF16), w_o.astype(BF16)
    tq = 512 if seq % 512 == 0 else seq
    assert seq % GM_CHUNK == 0 and seq % tq == 0 and t_all % tm == 0 and ds <= CHUNK
    gm_tile_p = 4 * GM_CHUNK if seq % (4 * GM_CHUNK) == 0 else GM_CHUNK
    assert t_p % (dbatch * ds) == 0
    n_blocks = -(-2 * t_all // EXPERT_ROWS) + N_EXPERTS

    xp, xs = x_prompt.reshape(t_p, d), x_sample.reshape(t_s, d)
    rope_tab = jnp.concatenate(
        [jnp.tile(_rope_table(jnp.arange(seq)), (batch, 1)),
         jnp.tile(_rope_table(PAST_LEN + jnp.arange(ds)), (dbatch, 1))], axis=0)
    rwt = router_w.T.astype(BF16)
    rb = jnp.broadcast_to(router_b.astype(F32)[:, None], (N_EXPERTS, LANES))
    cache_k = cache_k.reshape(depth, dbatch, WINDOW, KV_WIDTH)
    cache_v = cache_v.reshape(depth, dbatch, WINDOW, KV_WIDTH)

    ln_g = gm_ln_g.reshape(depth, 1, GM_WIDTH)
    ln_b = gm_ln_b.reshape(depth, 1, GM_WIDTH)
    g_attn = attn_out_g.reshape(depth, 1, ATTN_WIDTH)
    g_gm = gm_out_g.reshape(depth, 1, GM_WIDTH)
    bias_full = jnp.repeat(jnp.swapaxes(gm_bs, 1, 2), GM_GROUP_DIM, axis=2)
    g1, b1 = ln1_g.reshape(depth, 1, d), ln1_b.reshape(depth, 1, d)
    g2, b2 = ln2_g.reshape(depth, 1, d), ln2_b.reshape(depth, 1, d)

    kp_out, vp_out, ks_out, vs_out, gv_out = [], [], [], [], []
    for l in range(depth):
        q, k32, v32, kp, vp, gu, vn = _inproj(xp, xs, w_in_b, l, rope_tab, ln_g, ln_b, tm)
        attn_p = _attn_prompt(attn_sinks[l], q, kp, vp, g_attn, l, batch, seq, tq)
        attn_s = _attn_sample(attn_sinks[l], q, k32, v32, cache_k, cache_v, g_attn, l, t_p, dbatch, ds)
        gm_p = _gmlp(gu, vn, gm_ws, bias_full, g_gm, l, 0, t_p, GM_CHUNK, gm_tile_p // GM_CHUNK)
        gm_s = _gmlp(gu, vn, gm_ws, bias_full, g_gm, l, t_p, t_s, ds, dbatch)
        x1, meta_i, meta_f, counts = _outproj(
            attn_p, attn_s, gm_p, gm_s, xp, xs, w_o_b, l, g1, b1, rwt, rb, alpha, tm_route)
        block_expert, n_used, pad_lo, pad_hi, dest = _block_plan(meta_i, counts, n_blocks)
        rows = _dispatch(pad_lo, pad_hi, n_used, dest, x1, n_blocks, tm)
        out_rows = _experts(block_expert, n_used, rows, w_gate, w_up, w_down, l, n_blocks)
        gates_t = meta_f[0:2].T
        xp = _combine(dest, x1, gates_t, out_rows, g2, b2, l, alpha, tm, 0, t_p // tm)
        xs = _combine(dest, x1, gates_t, out_rows, g2, b2, l, alpha, tm, t_p // tm, t_s // tm)

        def last_window(a):
            return a[:t_p].reshape(batch, seq, KV_WIDTH)[:, seq - WINDOW:] \
                .reshape(batch, WINDOW, N_KV_HEADS, HEAD_DIM)

        kp_out.append(last_window(k32))
        vp_out.append(last_window(v32))
        k_s = k32[t_p:].reshape(dbatch, ds, KV_WIDTH)
        v_s = v32[t_p:].reshape(dbatch, ds, KV_WIDTH)
        ks_out.append(jnp.concatenate([cache_k[l][:, ds:], k_s], axis=1)
                      .reshape(dbatch, WINDOW, N_KV_HEADS, HEAD_DIM))
        vs_out.append(jnp.concatenate([cache_v[l][:, ds:], v_s], axis=1)
                      .reshape(dbatch, WINDOW, N_KV_HEADS, HEAD_DIM))
        gv_out.append(vn[t_p:].reshape(dbatch, ds, GM_GROUPS, GM_GROUP_DIM))

    return (xp.reshape(batch, seq, d), xs.reshape(dbatch, ds, d), jnp.stack(kp_out), jnp.stack(vp_out), jnp.stack(ks_out),
            jnp.stack(vs_out), jnp.stack(gv_out))
```

```python
import functools

import jax
import jax.numpy as jnp
import numpy as np
from jax import lax
from jax.experimental import pallas as pl
from jax.experimental.pallas import tpu as pltpu

F32 = jnp.float32
BF16 = jnp.bfloat16

CHUNK = 64
N_HEADS = 16
N_KV_HEADS = 4
HEAD_DIM = 64
Q_PER_KV = N_HEADS // N_KV_HEADS
ATTN_WIDTH = N_HEADS * HEAD_DIM
KV_WIDTH = N_KV_HEADS * HEAD_DIM
WINDOW = 128
ROT_DIM = HEAD_DIM // 4
ROPE_THETA = 500000.0
GM_CHUNK = 128
GM_GROUPS = 8
GM_GROUP_DIM = 128
GM_WIDTH = GM_GROUPS * GM_GROUP_DIM
N_EXPERTS = 16
N_EXPERT_GROUPS = 4
EXPERTS_PER_GROUP = N_EXPERTS // N_EXPERT_GROUPS
NORM_EPS = 1e-5
NEG_INF = -1e30
LOG2_E = 1.4426950408889634
PAST_LEN = 4096

LANES = 128
KEY_BLOCK = 256
Q_BLOCK = 128
EXPERT_ROWS = 512
VMEM_LIMIT = 56 * 1024 * 1024


def _cparams(sem):
    return pltpu.CompilerParams(dimension_semantics=sem, vmem_limit_bytes=VMEM_LIMIT)


def _resident(shape, index_map):
    return pl.BlockSpec(shape, index_map, pipeline_mode=pl.Buffered(1))


def _gelu(a):
    return 0.5 * a * (1.0 + lax.erf(a * np.float32(0.7071067811865476)))


def _pair_expand(t):
    r = pltpu.roll(t, HEAD_DIM, 1)
    lane = lax.broadcasted_iota(jnp.int32, t.shape, 1)
    lo = lane < HEAD_DIM
    return jnp.where(lo, t, r), jnp.where(lo, r, t)


def _expand_kv(a):
    a0, b0 = _pair_expand(a[:, 0:LANES])
    a1, b1 = _pair_expand(a[:, LANES:2 * LANES])
    return jnp.concatenate([a0, b0, a1, b1], axis=1)


def _pick_rows(first_ref, second_ref, n_first):
    return jnp.where(pl.program_id(0) < n_first, first_ref[...], second_ref[...])


def _split_specs(block, n_first):
    return [pl.BlockSpec(block, lambda i: (jnp.minimum(i, n_first - 1), 0)),
            pl.BlockSpec(block, lambda i: (jnp.maximum(i - n_first, 0), 0))]


def _inproj_body(xp_ref, xs_ref, w_ref, rope_ref, lng_ref, lnb_ref,
                 q_ref, k_ref, v_ref, kp_ref, vp_ref, gu_ref, vn_ref, *, n_first):
    xb = _pick_rows(xp_ref, xs_ref, n_first).astype(BF16)
    cs = rope_ref[:, 0:LANES]
    s1 = rope_ref[:, LANES:2 * LANES]
    s2 = rope_ref[:, 2 * LANES:3 * LANES]

    def rope(a):
        n = a.shape[1] // LANES
        width = a.shape[1]
        c = jnp.concatenate([cs] * n, axis=1)
        m1 = jnp.concatenate([s1] * n, axis=1)
        m2 = jnp.concatenate([s2] * n, axis=1)
        return a * c + pltpu.roll(a, width - ROT_DIM // 2, 1) * m1 + pltpu.roll(a, ROT_DIM // 2, 1) * m2

    def proj(c0, c1):
        return jnp.dot(xb, w_ref[0, :, c0:c1], preferred_element_type=F32)

    half = ATTN_WIDTH // 2
    for h in range(2):
        q = rope(proj(h * half, (h + 1) * half)) * np.float32(HEAD_DIM ** -0.5 * LOG2_E)
        q_ref[:, h * half:(h + 1) * half] = q.astype(BF16)

    kv = proj(ATTN_WIDTH, ATTN_WIDTH + 2 * KV_WIDTH)
    k = rope(kv[:, 0:KV_WIDTH])
    v = kv[:, KV_WIDTH:2 * KV_WIDTH]
    k_ref[...] = k
    v_ref[...] = v
    kp_ref[...] = _expand_kv(k).astype(BF16)
    vp_ref[...] = _expand_kv(v).astype(BF16)

    base = ATTN_WIDTH + 2 * KV_WIDTH
    for h in range(2):
        gu_ref[:, h * half:(h + 1) * half] = _gelu(proj(base + h * half, base + (h + 1) * half))

    base = base + GM_WIDTH
    for h in range(2):
        gv = _gelu(proj(base + h * half, base + (h + 1) * half))
        for g in range(half // GM_GROUP_DIM):
            t = gv[:, g * GM_GROUP_DIM:(g + 1) * GM_GROUP_DIM]
            col = h * half + g * GM_GROUP_DIM
            tc = t - jnp.mean(t, axis=-1, keepdims=True)
            var = jnp.mean(tc * tc, axis=-1, keepdims=True)
            y = tc * lax.rsqrt(var + NORM_EPS) * lng_ref[:, col:col + GM_GROUP_DIM] \
                + lnb_ref[:, col:col + GM_GROUP_DIM]
            vn_ref[:, col:col + GM_GROUP_DIM] = y


def _inproj(xp, xs, w, layer, rope_tab, lng, lnb, tm):
    d = xp.shape[1]
    t_all = xp.shape[0] + xs.shape[0]
    n_first = xp.shape[0] // tm
    n_in = w.shape[2]
    row = lambda i: (i, 0)
    const = lambda i: (0, 0)
    out_shape = (
        jax.ShapeDtypeStruct((t_all, ATTN_WIDTH), BF16),
        jax.ShapeDtypeStruct((t_all, KV_WIDTH), F32),
        jax.ShapeDtypeStruct((t_all, KV_WIDTH), F32),
        jax.ShapeDtypeStruct((t_all, 2 * KV_WIDTH), BF16),
        jax.ShapeDtypeStruct((t_all, 2 * KV_WIDTH), BF16),
        jax.ShapeDtypeStruct((t_all, GM_WIDTH), F32),
        jax.ShapeDtypeStruct((t_all, GM_WIDTH), F32),
    )
    return pl.pallas_call(
        functools.partial(_inproj_body, n_first=n_first),
        grid=(t_all // tm,),
        in_specs=_split_specs((tm, d), n_first) + [
            _resident((1, d, n_in), lambda i: (layer, 0, 0)),
            pl.BlockSpec((tm, 3 * LANES), row),
            _resident((None, 1, GM_WIDTH), lambda i: (layer, 0, 0)),
            _resident((None, 1, GM_WIDTH), lambda i: (layer, 0, 0)),
        ],
        out_specs=(
            pl.BlockSpec((tm, ATTN_WIDTH), row),
            pl.BlockSpec((tm, KV_WIDTH), row),
            pl.BlockSpec((tm, KV_WIDTH), row),
            pl.BlockSpec((tm, 2 * KV_WIDTH), row),
            pl.BlockSpec((tm, 2 * KV_WIDTH), row),
            pl.BlockSpec((tm, GM_WIDTH), row),
            pl.BlockSpec((tm, GM_WIDTH), row),
        ),
        out_shape=out_shape,
        compiler_params=_cparams(("parallel",)),
        name="inproj",
    )(xp, xs, w, rope_tab, lng, lnb)


def _attn_group(qg, kexp, vexp, bias, sinks):
    nk = kexp.shape[0]
    seg = lax.broadcasted_iota(jnp.int32, kexp.shape, 1) // HEAD_DIM
    zero = jnp.zeros_like(kexp)
    kbd = jnp.concatenate([jnp.where(seg == j, kexp, zero) for j in range(Q_PER_KV)], axis=0)
    vbd = jnp.concatenate([jnp.where(seg == j, vexp, zero) for j in range(Q_PER_KV)], axis=0)
    s = lax.dot_general(qg, kbd, (((1,), (1,)), ((), ())), preferred_element_type=F32)
    es, rs = [], []
    for j in range(Q_PER_KV):
        sj = s[:, j * nk:(j + 1) * nk] + bias
        sink = sinks[j] * np.float32(LOG2_E)
        m = jnp.maximum(jnp.max(sj, axis=-1, keepdims=True), sink)
        e = jnp.exp2(sj - m)
        den = jnp.sum(e, axis=-1, keepdims=True) + jnp.exp2(sink - m)
        es.append(e.astype(BF16))
        rs.append(1.0 / den)
    p = jnp.concatenate(es, axis=1)
    o = jnp.dot(p, vbd, preferred_element_type=F32)
    oseg = lax.broadcasted_iota(jnp.int32, o.shape, 1) // HEAD_DIM
    r = jnp.where(oseg == 0, rs[0], jnp.where(oseg == 1, rs[1], jnp.where(oseg == 2, rs[2], rs[3])))
    return o * r


def _attn_rows(q, kbuf, vbuf, bias, sink_ref, g_row):
    outs = []
    for h in range(N_KV_HEADS):
        kt = kbuf[:, h * LANES:(h + 1) * LANES]
        vt = vbuf[:, h * LANES:(h + 1) * LANES]
        kexp = jnp.concatenate([kt, kt], axis=1)
        vexp = jnp.concatenate([vt, vt], axis=1)
        sinks = [sink_ref[h * Q_PER_KV + j] for j in range(Q_PER_KV)]
        width = Q_PER_KV * HEAD_DIM
        outs.append(_attn_group(q[:, h * width:(h + 1) * width], kexp, vexp, bias, sinks))
    o = jnp.concatenate(outs, axis=1)
    ms = jnp.mean(o * o, axis=-1, keepdims=True)
    return o * lax.rsqrt(ms + NORM_EPS) * g_row


def _attn_prompt_body(sink_ref, q_ref, kc_ref, kprev_ref, vc_ref, vprev_ref, g_ref, o_ref,
                      kbuf, vbuf, *, tq):
    i = pl.program_id(1)
    kbuf[0:WINDOW, :] = kprev_ref[...]
    kbuf[WINDOW:WINDOW + tq, :] = kc_ref[...]
    vbuf[0:WINDOW, :] = vprev_ref[...]
    vbuf[WINDOW:WINDOW + tq, :] = vc_ref[...]
    rq = lax.broadcasted_iota(jnp.int32, (Q_BLOCK, KEY_BLOCK), 0) // CHUNK
    ck = lax.broadcasted_iota(jnp.int32, (Q_BLOCK, KEY_BLOCK), 1) // CHUNK
    rel = ck - rq
    band = (rel >= 0) & (rel <= WINDOW // CHUNK)
    g_row = g_ref[...]

    def sub(s, carry):
        r0 = pl.multiple_of(s * Q_BLOCK, Q_BLOCK)
        first = jnp.logical_and(i == 0, s == 0)
        kmin = jnp.where(first, WINDOW // CHUNK, 0)
        ok = jnp.logical_and(band, ck >= kmin)
        bias = jnp.where(ok, np.float32(0.0), np.float32(NEG_INF))
        q = q_ref[pl.ds(r0, Q_BLOCK), :]
        kb = kbuf[pl.ds(r0, KEY_BLOCK), :]
        vb = vbuf[pl.ds(r0, KEY_BLOCK), :]
        o_ref[pl.ds(r0, Q_BLOCK), :] = _attn_rows(q, kb, vb, bias, sink_ref, g_row).astype(BF16)
        return carry

    lax.fori_loop(0, tq // Q_BLOCK, sub, 0)


def _attn_prompt(sinks, q, kp, vp, g_attn, layer, batch, seq, tq):
    nq = seq // tq
    per = tq // WINDOW
    cur = lambda b, i: (b * nq + i, 0)
    prev = lambda b, i: (b * (seq // WINDOW) + jnp.maximum(i * per - 1, 0), 0)
    kernel = functools.partial(_attn_prompt_body, tq=tq)
    return pl.pallas_call(
        kernel,
        grid=(batch, nq),
        in_specs=[
            pl.BlockSpec(memory_space=pltpu.SMEM),
            pl.BlockSpec((tq, ATTN_WIDTH), cur),
            pl.BlockSpec((tq, 2 * KV_WIDTH), cur),
            pl.BlockSpec((WINDOW, 2 * KV_WIDTH), prev),
            pl.BlockSpec((tq, 2 * KV_WIDTH), cur),
            pl.BlockSpec((WINDOW, 2 * KV_WIDTH), prev),
            pl.BlockSpec((None, 1, ATTN_WIDTH), lambda b, i: (layer, 0, 0)),
        ],
        out_specs=pl.BlockSpec((tq, ATTN_WIDTH), cur),
        out_shape=jax.ShapeDtypeStruct((batch * seq, ATTN_WIDTH), BF16),
        scratch_shapes=[pltpu.VMEM((WINDOW + tq, 2 * KV_WIDTH), BF16),
                        pltpu.VMEM((WINDOW + tq, 2 * KV_WIDTH), BF16)],
        compiler_params=_cparams(("parallel", "parallel")),
        name="attn_prompt",
    )(sinks, q, kp, kp, vp, vp, g_attn)


def _attn_sample_body(sink_ref, q_ref, ck_ref, cv_ref, kn_ref, vn_ref, g_ref, o_ref, *, ds):
    pad = jnp.zeros((KEY_BLOCK - WINDOW - ds, KV_WIDTH), F32)
    k_all = jnp.concatenate([ck_ref[0], kn_ref[...], pad], axis=0)
    v_all = jnp.concatenate([cv_ref[0], vn_ref[...], pad], axis=0)
    kb = _expand_kv(k_all).astype(BF16)
    vb = _expand_kv(v_all).astype(BF16)
    col = lax.broadcasted_iota(jnp.int32, (ds, KEY_BLOCK), 1)
    bias = jnp.where(col < WINDOW + ds, np.float32(0.0), np.float32(NEG_INF))
    o_ref[...] = _attn_rows(q_ref[...], kb, vb, bias, sink_ref, g_ref[...]).astype(BF16)


def _attn_sample(sinks, q, k32, v32, cache_k, cache_v, g_attn, layer, row0, dbatch, ds):
    off = row0 // ds
    new = lambda b: (off + b, 0)
    kernel = functools.partial(_attn_sample_body, ds=ds)
    return pl.pallas_call(
        kernel,
        grid=(dbatch,),
        in_specs=[
            pl.BlockSpec(memory_space=pltpu.SMEM),
            pl.BlockSpec((ds, ATTN_WIDTH), new),
            pl.BlockSpec((None, 1, WINDOW, KV_WIDTH), lambda b: (layer, b, 0, 0)),
            pl.BlockSpec((None, 1, WINDOW, KV_WIDTH), lambda b: (layer, b, 0, 0)),
            pl.BlockSpec((ds, KV_WIDTH), new),
            pl.BlockSpec((ds, KV_WIDTH), new),
            pl.BlockSpec((None, 1, ATTN_WIDTH), lambda b: (layer, 0, 0)),
        ],
        out_specs=pl.BlockSpec((ds, ATTN_WIDTH), lambda b: (b, 0)),
        out_shape=jax.ShapeDtypeStruct((dbatch * ds, ATTN_WIDTH), BF16),
        compiler_params=_cparams(("parallel",)),
        name="attn_sample",
    )(sinks, q, cache_k, cache_v, k32, v32, g_attn)


def _gmlp_body(gu_ref, vn_ref, ws_ref, bias_ref, g_ref, o_ref, gm_scr, *, rows, nchunk):
    ci = lax.broadcasted_iota(jnp.int32, (GM_CHUNK, GM_CHUNK), 0) // CHUNK
    cj = lax.broadcasted_iota(jnp.int32, (GM_CHUNK, GM_CHUNK), 1) // CHUNK
    visible = ci >= cj
    for g in range(GM_GROUPS):
        lo, hi = g * GM_GROUP_DIM, (g + 1) * GM_GROUP_DIM
        w = jnp.where(visible, ws_ref[g], np.float32(0.0))[0:rows, :].astype(BF16)
        pieces = []
        for n in range(nchunk):
            piece = vn_ref[n * rows:(n + 1) * rows, lo:hi]
            if rows < GM_CHUNK:
                piece = jnp.concatenate([piece, jnp.zeros((GM_CHUNK - rows, GM_GROUP_DIM), F32)], axis=0)
            pieces.append(piece.astype(BF16))
        rhs = jnp.concatenate(pieces, axis=1)
        mix = jnp.dot(w, rhs, preferred_element_type=F32)
        b = bias_ref[0:rows, lo:hi]
        for n in range(nchunk):
            gm_scr[n * rows:(n + 1) * rows, lo:hi] = \
                gu_ref[n * rows:(n + 1) * rows, lo:hi] * (mix[:, n * GM_GROUP_DIM:(n + 1) * GM_GROUP_DIM] + b)
    gm = gm_scr[...]
    ms = jnp.mean(gm * gm, axis=-1, keepdims=True)
    o_ref[...] = (gm * lax.rsqrt(ms + NORM_EPS) * g_ref[...]).astype(BF16)


def _gmlp(gu, vn, ws, bias_full, g_gm, layer, row0, n_rows, rows, nchunk):
    tile = rows * nchunk
    off = row0 // tile
    kernel = functools.partial(_gmlp_body, rows=rows, nchunk=nchunk)
    return pl.pallas_call(
        kernel,
        grid=(n_rows // tile,),
        in_specs=[
            pl.BlockSpec((tile, GM_WIDTH), lambda i: (off + i, 0)),
            pl.BlockSpec((tile, GM_WIDTH), lambda i: (off + i, 0)),
            pl.BlockSpec((None, GM_GROUPS, GM_CHUNK, GM_CHUNK), lambda i: (layer, 0, 0, 0)),
            pl.BlockSpec((None, GM_CHUNK, GM_WIDTH), lambda i: (layer, 0, 0)),
            pl.BlockSpec((None, 1, GM_WIDTH), lambda i: (layer, 0, 0)),
        ],
        out_specs=pl.BlockSpec((tile, GM_WIDTH), lambda i: (i, 0)),
        out_shape=jax.ShapeDtypeStruct((n_rows, GM_WIDTH), BF16),
        scratch_shapes=[pltpu.VMEM((tile, GM_WIDTH), F32)],
        compiler_params=_cparams(("parallel",)),
        name="gmlp",
    )(gu, vn, ws, bias_full, g_gm)


def _layer_norm_rows(z, g, b):
    zc = z - jnp.mean(z, axis=-1, keepdims=True)
    var = jnp.mean(zc * zc, axis=-1, keepdims=True)
    return zc * lax.rsqrt(var + NORM_EPS) * g + b


def _tree_sum(items):
    while len(items) > 1:
        items = [a + b for a, b in zip(items[0::2], items[1::2])] + (items[-1:] if len(items) % 2 else [])
    return items[0]


def _first_max(items):
    while len(items) > 1:
        nxt = [tuple(jnp.where(a[0] >= b[0], x, y) for x, y in zip(a, b))
               for a, b in zip(items[0::2], items[1::2])]
        items = nxt + (items[-1:] if len(items) % 2 else [])
    return items[0]


def _route_rows(lg, rb):
    sc = 1.0 / (1.0 + jnp.exp(-lg))
    sel = sc + rb
    rows = [sel[e:e + 1, :] for e in range(N_EXPERTS)]
    scr = [sc[e:e + 1, :] for e in range(N_EXPERTS)]
    groups = []
    for g in range(N_EXPERT_GROUPS):
        a, b, c, d = rows[g * EXPERTS_PER_GROUP:(g + 1) * EXPERTS_PER_GROUP]
        hi1, lo1 = jnp.maximum(a, b), jnp.minimum(a, b)
        hi2, lo2 = jnp.maximum(c, d), jnp.minimum(c, d)
        top1 = jnp.maximum(hi1, hi2)
        top2 = jnp.maximum(jnp.minimum(hi1, hi2), jnp.maximum(lo1, lo2))
        members = tuple(rows[g * EXPERTS_PER_GROUP + j] for j in range(EXPERTS_PER_GROUP))
        scores = tuple(scr[g * EXPERTS_PER_GROUP + j] for j in range(EXPERTS_PER_GROUP))
        groups.append((top1 + top2, jnp.full_like(top1, g * EXPERTS_PER_GROUP)) + members + scores)
    best = _first_max(groups)
    first_id = best[1]
    cand = [(best[2 + j], best[2 + EXPERTS_PER_GROUP + j], first_id + np.float32(j))
            for j in range(EXPERTS_PER_GROUP)]
    _, g0, idx0 = _first_max(cand)
    neg = np.float32(-np.inf)
    cand2 = [(jnp.where(c[2] == idx0, neg, c[0]), c[1], c[2]) for c in cand]
    _, g1, idx1 = _first_max(cand2)
    tot = g0 + g1
    return idx0, idx1, g0 / tot, g1 / tot


def _outproj_body(ap_ref, as_ref, gp_ref, gs_ref, xp_ref, xs_ref, wo_ref, g1_ref, b1_ref, rwt_ref, rb_ref,
                  x1_ref, mi_ref, mf_ref, cnt_ref, carry, *, alpha, n_first):
    i = pl.program_id(0)
    tm = xp_ref.shape[0]

    @pl.when(i == 0)
    def _():
        carry[...] = jnp.zeros_like(carry)

    y = jnp.dot(_pick_rows(ap_ref, as_ref, n_first), wo_ref[0, 0:ATTN_WIDTH, :], preferred_element_type=F32)
    y = y + jnp.dot(_pick_rows(gp_ref, gs_ref, n_first), wo_ref[0, ATTN_WIDTH:ATTN_WIDTH + GM_WIDTH, :],
                    preferred_element_type=F32)
    x1 = _layer_norm_rows(np.float32(alpha) * _pick_rows(xp_ref, xs_ref, n_first) + y,
                          g1_ref[...], b1_ref[...])
    x1_ref[...] = x1

    lg = lax.dot_general(rwt_ref[...], x1.astype(BF16), (((1,), (1,)), ((), ())),
                         preferred_element_type=F32)
    rb = jnp.concatenate([rb_ref[...]] * (tm // LANES), axis=1)
    idx0, idx1, gate0, gate1 = _route_rows(lg, rb)

    one, zero = np.float32(1.0), np.float32(0.0)
    hit0 = [idx0 == np.float32(e) for e in range(N_EXPERTS)]
    hit1 = [idx1 == np.float32(e) for e in range(N_EXPERTS)]
    chosen = jnp.concatenate(
        [jnp.where(jnp.logical_or(hit0[e], hit1[e]), one, zero) for e in range(N_EXPERTS)], axis=0)
    s_idx = lax.broadcasted_iota(jnp.int32, (tm, tm), 0)
    t_idx = lax.broadcasted_iota(jnp.int32, (tm, tm), 1)
    before = jnp.where(s_idx < t_idx, np.float32(1.0), zero).astype(BF16)
    base = carry[...]
    rank = jnp.dot(chosen.astype(BF16), before, preferred_element_type=F32) \
        + jnp.concatenate([base] * (tm // LANES), axis=1)
    carry[...] = base + jnp.sum(chosen, axis=1, keepdims=True)
    cnt_ref[...] = carry[...]

    rank_rows = [rank[e:e + 1, :] for e in range(N_EXPERTS)]
    rank0 = _tree_sum([jnp.where(hit0[e], rank_rows[e], zero) for e in range(N_EXPERTS)])
    rank1 = _tree_sum([jnp.where(hit1[e], rank_rows[e], zero) for e in range(N_EXPERTS)])
    zr = jnp.zeros((1, tm), F32)
    mi_ref[...] = jnp.concatenate([idx0, idx1, rank0, rank1, zr, zr, zr, zr], axis=0).astype(jnp.int32)
    mf_ref[...] = jnp.concatenate([gate0, gate1, zr, zr, zr, zr, zr, zr], axis=0)


def _outproj(attn_p, attn_s, gm_p, gm_s, xp, xs, wo, layer, g1, b1, rwt, rb, alpha, tm):
    d = xp.shape[1]
    t_all = xp.shape[0] + xs.shape[0]
    n_first = xp.shape[0] // tm
    row = lambda i: (i, 0)
    col = lambda i: (0, i)
    const = lambda i: (0, 0)
    kernel = functools.partial(_outproj_body, alpha=alpha, n_first=n_first)
    return pl.pallas_call(
        kernel,
        grid=(t_all // tm,),
        in_specs=_split_specs((tm, ATTN_WIDTH), n_first) + _split_specs((tm, GM_WIDTH), n_first)
        + _split_specs((tm, d), n_first) + [
            _resident((1, ATTN_WIDTH + GM_WIDTH, d), lambda i: (layer, 0, 0)),
            _resident((None, 1, d), lambda i: (layer, 0, 0)),
            _resident((None, 1, d), lambda i: (layer, 0, 0)),
            _resident((N_EXPERTS, d), const),
            _resident((N_EXPERTS, LANES), const),
        ],
        out_specs=(
            pl.BlockSpec((tm, d), row),
            pl.BlockSpec((8, tm), col),
            pl.BlockSpec((8, tm), col),
            pl.BlockSpec((N_EXPERTS, LANES), const),
        ),
        out_shape=(
            jax.ShapeDtypeStruct((t_all, d), F32),
            jax.ShapeDtypeStruct((8, t_all), jnp.int32),
            jax.ShapeDtypeStruct((8, t_all), F32),
            jax.ShapeDtypeStruct((N_EXPERTS, LANES), F32),
        ),
        scratch_shapes=[pltpu.VMEM((N_EXPERTS, LANES), F32)],
        compiler_params=_cparams(("arbitrary",)),
        name="outproj_router",
    )(attn_p, attn_s, gm_p, gm_s, xp, xs, wo, g1, b1, rwt, rb)


def _dispatch_body(pz_ref, pe_ref, nu_ref, dest_ref, x1_ref, rows_hbm, x1s, zbuf, sem, zsem, *, tb, n_blocks):
    i = pl.program_id(0)
    n = pl.num_programs(0)
    slot = i % 2

    def wait_tile(s):
        for _ in range(2):
            pltpu.make_async_copy(x1s.at[s], rows_hbm.at[pl.ds(0, tb), :], sem.at[s]).wait()

    x1s[slot] = x1_ref[...]

    for s in range(2):
        @pl.when(slot == s)
        def _(s=s):
            for r in range(tb):
                for k in (2, 3):
                    pltpu.make_async_copy(x1s.at[s, pl.ds(r, 1), :],
                                          rows_hbm.at[pl.ds(dest_ref[k, r], 1), :], sem.at[s]).start()

    @pl.when(i > 0)
    def _():
        wait_tile(1 - slot)

    @pl.when(i == n - 1)
    def _():
        wait_tile(slot)
        zbuf[...] = jnp.zeros_like(zbuf)

        def row_copy(r):
            return pltpu.make_async_copy(zbuf.at[pl.ds(0, 1), :], rows_hbm.at[pl.ds(r, 1), :], zsem.at[0])

        def block_copy(j):
            r0 = pl.multiple_of(j * EXPERT_ROWS, EXPERT_ROWS)
            return pltpu.make_async_copy(zbuf, rows_hbm.at[pl.ds(r0, EXPERT_ROWS), :], zsem.at[0])

        for e in range(N_EXPERTS):
            lax.fori_loop(pz_ref[e], pe_ref[e], lambda r, c: (row_copy(r).start(), c)[1], 0)
        lax.fori_loop(nu_ref[0], n_blocks, lambda j, c: (block_copy(j).start(), c)[1], 0)
        for e in range(N_EXPERTS):
            lax.fori_loop(pz_ref[e], pe_ref[e], lambda r, c: (row_copy(r).wait(), c)[1], 0)
        lax.fori_loop(nu_ref[0], n_blocks, lambda j, c: (block_copy(j).wait(), c)[1], 0)


def _dispatch(pad_lo, pad_hi, n_used, dest, x1, n_blocks, tb):
    t_all, d = x1.shape
    kernel = functools.partial(_dispatch_body, tb=tb, n_blocks=n_blocks)
    grid_spec = pltpu.PrefetchScalarGridSpec(
        num_scalar_prefetch=3,
        grid=(t_all // tb,),
        in_specs=[
            pl.BlockSpec((8, tb), lambda i, pz, pe, nu: (0, i), memory_space=pltpu.SMEM),
            pl.BlockSpec((tb, d), lambda i, pz, pe, nu: (i, 0)),
        ],
        out_specs=pl.BlockSpec(memory_space=pl.ANY),
        scratch_shapes=[pltpu.VMEM((2, tb, d), F32), pltpu.VMEM((EXPERT_ROWS, d), F32),
                        pltpu.SemaphoreType.DMA((2,)), pltpu.SemaphoreType.DMA((1,))],
    )
    return pl.pallas_call(
        kernel,
        grid_spec=grid_spec,
        out_shape=jax.ShapeDtypeStruct((n_blocks * EXPERT_ROWS, d), F32),
        compiler_params=_cparams(("arbitrary",)),
        name="dispatch",
    )(pad_lo, pad_hi, n_used, dest, x1)


def _expert_body(be_ref, nu_ref, x_ref, wg_ref, wu_ref, wd_ref, o_ref, wg_s, wu_s, wd_s):
    j = pl.program_id(0)

    @pl.when(jnp.logical_or(j == 0, be_ref[j] != be_ref[jnp.maximum(j - 1, 0)]))
    def _():
        wg_s[...] = wg_ref[0, 0].astype(BF16)
        wu_s[...] = wu_ref[0, 0].astype(BF16)
        wd_s[...] = wd_ref[0, 0].astype(BF16)

    @pl.when(j < nu_ref[0])
    def _():
        xb = x_ref[...].astype(BF16)
        gate = jnp.dot(xb, wg_s[...], preferred_element_type=F32)
        up = jnp.dot(xb, wu_s[...], preferred_element_type=F32)
        h = gate * (1.0 / (1.0 + jnp.exp(-gate))) * up
        o_ref[...] = jnp.dot(h.astype(BF16), wd_s[...], preferred_element_type=F32)

    @pl.when(j >= nu_ref[0])
    def _():
        o_ref[...] = jnp.zeros_like(o_ref)


def _experts(block_expert, n_used, rows, wg, wu, wd, layer, n_blocks):
    d = rows.shape[1]
    de = wg.shape[3]
    grid_spec = pltpu.PrefetchScalarGridSpec(
        num_scalar_prefetch=2,
        grid=(n_blocks,),
        in_specs=[
            pl.BlockSpec((EXPERT_ROWS, d), lambda j, be, nu: (jnp.minimum(j, nu[0] - 1), 0)),
            pl.BlockSpec((1, 1, d, de), lambda j, be, nu: (layer, be[j], 0, 0)),
            pl.BlockSpec((1, 1, d, de), lambda j, be, nu: (layer, be[j], 0, 0)),
            pl.BlockSpec((1, 1, de, d), lambda j, be, nu: (layer, be[j], 0, 0)),
        ],
        out_specs=pl.BlockSpec((EXPERT_ROWS, d), lambda j, be, nu: (j, 0)),
        scratch_shapes=[pltpu.VMEM((d, de), BF16), pltpu.VMEM((d, de), BF16), pltpu.VMEM((de, d), BF16)],
    )
    return pl.pallas_call(
        _expert_body,
        grid_spec=grid_spec,
        out_shape=jax.ShapeDtypeStruct(rows.shape, F32),
        compiler_params=_cparams(("arbitrary",)),
        name="experts",
    )(block_expert, n_used, rows, wg, wu, wd)


def _combine_body(d0_ref, dn_ref, x1_ref, gt_ref, rows_hbm, g2_ref, b2_ref, x2_ref, rbuf, sem, *, alpha):
    i = pl.program_id(0)
    n = pl.num_programs(0)
    tm = x1_ref.shape[0]
    slot = i % 2

    def row_copy(src_row, s, k, r):
        return pltpu.make_async_copy(rows_hbm.at[pl.ds(src_row, 1), :],
                                     rbuf.at[s, k, pl.ds(r, 1), :], sem.at[s])

    @pl.when(i == 0)
    def _():
        def body(r, c):
            for k in range(2):
                row_copy(d0_ref[2 + k, r], 0, k, r).start()
            return c
        lax.fori_loop(0, tm, body, 0, unroll=8)

    def wait_slot(s):
        for k in range(2):
            pltpu.make_async_copy(rows_hbm.at[pl.ds(0, tm), :], rbuf.at[s, k], sem.at[s]).wait()

    for s in range(2):
        @pl.when(jnp.logical_and(i + 1 < n, 1 - slot == s))
        def _(s=s):
            for r in range(tm):
                for k in range(2):
                    row_copy(dn_ref[2 + k, r], s, k, r).start()

    wait_slot(slot)
    y = rbuf[slot, 0] * gt_ref[:, 0:1] + rbuf[slot, 1] * gt_ref[:, 1:2]
    x2_ref[...] = _layer_norm_rows(np.float32(alpha) * x1_ref[...] + y, g2_ref[...], b2_ref[...])


def _combine(meta_i, x1, gates_t, out_rows, g2, b2, layer, alpha, tm, tile0, n_tiles):
    d = x1.shape[1]
    kernel = functools.partial(_combine_body, alpha=alpha)
    return pl.pallas_call(
        kernel,
        grid=(n_tiles,),
        in_specs=[
            pl.BlockSpec((8, tm), lambda i: (0, tile0), memory_space=pltpu.SMEM),
            pl.BlockSpec((8, tm), lambda i: (0, tile0 + jnp.minimum(i + 1, n_tiles - 1)),
                         memory_space=pltpu.SMEM),
            pl.BlockSpec((tm, d), lambda i: (tile0 + i, 0)),
            pl.BlockSpec((tm, 2), lambda i: (tile0 + i, 0)),
            pl.BlockSpec(memory_space=pl.ANY),
            _resident((None, 1, d), lambda i: (layer, 0, 0)),
            _resident((None, 1, d), lambda i: (layer, 0, 0)),
        ],
        out_specs=pl.BlockSpec((tm, d), lambda i: (i, 0)),
        out_shape=jax.ShapeDtypeStruct((n_tiles * tm, d), F32),
        scratch_shapes=[pltpu.VMEM((2, 2, tm, d), F32), pltpu.SemaphoreType.DMA((2,))],
        compiler_params=_cparams(("arbitrary",)),
        name="combine_ln",
    )(meta_i, meta_i, x1, gates_t, out_rows, g2, b2)


def _rope_table(pos):
    half = ROT_DIM // 2
    inv_freq = jnp.power(ROPE_THETA, -2.0 * jnp.arange(half, dtype=F32) / ROT_DIM)
    ang = pos.astype(F32)[:, None] * inv_freq[None, :]
    cos, sin = jnp.cos(ang), jnp.sin(ang)
    n = pos.shape[0]
    ones = jnp.ones((n, HEAD_DIM - ROT_DIM), F32)
    zeros = jnp.zeros((n, HEAD_DIM - ROT_DIM), F32)
    zh = jnp.zeros((n, half), F32)
    c = jnp.concatenate([cos, cos, ones], axis=1)
    s1 = jnp.concatenate([-sin, zh, zeros], axis=1)
    s2 = jnp.concatenate([zh, sin, zeros], axis=1)
    rep = LANES // HEAD_DIM
    return jnp.concatenate([jnp.tile(c, (1, rep)), jnp.tile(s1, (1, rep)), jnp.tile(s2, (1, rep))], axis=1)


def _block_plan(meta_i, counts, n_blocks):
    cnt = counts[:, 0].astype(jnp.int32)
    padded = (cnt + EXPERT_ROWS - 1) // EXPERT_ROWS * EXPERT_ROWS
    pend = jnp.cumsum(padded)
    pstart = pend - padded
    n_used = pend[-1] // EXPERT_ROWS
    blk = jnp.minimum(jnp.arange(n_blocks, dtype=jnp.int32), n_used - 1)
    block_expert = jnp.sum((blk[:, None] * EXPERT_ROWS >= pend[None, :]).astype(jnp.int32), axis=1)
    experts = jnp.arange(N_EXPERTS, dtype=jnp.int32)[None, :, None]
    start_of = jnp.sum(jnp.where(meta_i[0:2, None, :] == experts, pstart[None, :, None], 0), axis=1)
    dest = jnp.concatenate([meta_i[0:2], start_of + meta_i[2:4], meta_i[4:8]], axis=0)
    return block_expert, n_used.reshape(1), pstart + cnt, pend, dest


def kernel(x_prompt, x_sample, cache_k, cache_v, w_in, w_o, attn_sinks, attn_out_g, gm_out_g,
           gm_ln_g, gm_ln_b, gm_ws, gm_bs, ln1_g, ln1_b, ln2_g, ln2_b, router_w, router_b,
           w_gate, w_up, w_down):
    depth = w_in.shape[0]
    batch, seq, d = x_prompt.shape
    dbatch, ds, _ = x_sample.shape
    alpha = (2 * depth) ** 0.25
    t_p, t_s = batch * seq, dbatch * ds
    t_all = t_p + t_s
    tm = 256 if (t_p % 256 == 0 and t_s % 256 == 0) else 128
    tm_route = 512 if (t_p % 512 == 0 and t_s % 512 == 0) else tm
    w_in_b, w_o_b = w_in.astype(BF16), w_o.astype(BF16)
    tq = 512 if seq % 512 == 0 else seq
    assert seq % GM_CHUNK == 0 and seq % tq == 0 and t_all % tm == 0 and ds <= CHUNK
    gm_tile_p = 4 * GM_CHUNK if seq % (4 * GM_CHUNK) == 0 else GM_CHUNK
    assert t_p % (dbatch * ds) == 0
    n_blocks = -(-2 * t_all // EXPERT_ROWS) + N_EXPERTS

    xp, xs = x_prompt.reshape(t_p, d), x_sample.reshape(t_s, d)
    rope_tab = jnp.concatenate(
        [jnp.tile(_rope_table(jnp.arange(seq)), (batch, 1)),
         jnp.tile(_rope_table(PAST_LEN + jnp.arange(ds)), (dbatch, 1))], axis=0)
    rwt = router_w.T.astype(BF16)
    rb = jnp.broadcast_to(router_b.astype(F32)[:, None], (N_EXPERTS, LANES))
    cache_k = cache_k.reshape(depth, dbatch, WINDOW, KV_WIDTH)
    cache_v = cache_v.reshape(depth, dbatch, WINDOW, KV_WIDTH)

    ln_g = gm_ln_g.reshape(depth, 1, GM_WIDTH)
    ln_b = gm_ln_b.reshape(depth, 1, GM_WIDTH)
    g_attn = attn_out_g.reshape(depth, 1, ATTN_WIDTH)
    g_gm = gm_out_g.reshape(depth, 1, GM_WIDTH)
    bias_full = jnp.repeat(jnp.swapaxes(gm_bs, 1, 2), GM_GROUP_DIM, axis=2)
    g1, b1 = ln1_g.reshape(depth, 1, d), ln1_b.reshape(depth, 1, d)
    g2, b2 = ln2_g.reshape(depth, 1, d), ln2_b.reshape(depth, 1, d)

    kp_out, vp_out, ks_out, vs_out, gv_out = [], [], [], [], []
    for l in range(depth):
        q, k32, v32, kp, vp, gu, vn = _inproj(xp, xs, w_in_b, l, rope_tab, ln_g, ln_b, tm)
        attn_p = _attn_prompt(attn_sinks[l], q, kp, vp, g_attn, l, batch, seq, tq)
        attn_s = _attn_sample(attn_sinks[l], q, k32, v32, cache_k, cache_v, g_attn, l, t_p, dbatch, ds)
        gm_p = _gmlp(gu, vn, gm_ws, bias_full, g_gm, l, 0, t_p, GM_CHUNK, gm_tile_p // GM_CHUNK)
        gm_s = _gmlp(gu, vn, gm_ws, bias_full, g_gm, l, t_p, t_s, ds, dbatch)
        x1, meta_i, meta_f, counts = _outproj(
            attn_p, attn_s, gm_p, gm_s, xp, xs, w_o_b, l, g1, b1, rwt, rb, alpha, tm_route)
        block_expert, n_used, pad_lo, pad_hi, dest = _block_plan(meta_i, counts, n_blocks)
        rows = _dispatch(pad_lo, pad_hi, n_used, dest, x1, n_blocks, tm)
        out_rows = _experts(block_expert, n_used, rows, w_gate, w_up, w_down, l, n_blocks)
        gates_t = meta_f[0:2].T
        xp = _combine(dest, x1, gates_t, out_rows, g2, b2, l, alpha, tm, 0, t_p // tm)
        xs = _combine(dest, x1, gates_t, out_rows, g2, b2, l, alpha, tm, t_p // tm, t_s // tm)

        def last_window(a):
            rows = [lax.slice(a, ((b + 1) * seq - WINDOW, 0), ((b + 1) * seq, KV_WIDTH)) for b in range(batch)]
            return jnp.stack(rows).reshape(batch, WINDOW, N_KV_HEADS, HEAD_DIM)

        kp_out.append(last_window(k32))
        vp_out.append(last_window(v32))
        k_s = k32[t_p:].reshape(dbatch, ds, KV_WIDTH)
        v_s = v32[t_p:].reshape(dbatch, ds, KV_WIDTH)
        ks_out.append(jnp.concatenate([cache_k[l][:, ds:], k_s], axis=1)
                      .reshape(dbatch, WINDOW, N_KV_HEADS, HEAD_DIM))
        vs_out.append(jnp.concatenate([cache_v[l][:, ds:], v_s], axis=1)
                      .reshape(dbatch, WINDOW, N_KV_HEADS, HEAD_DIM))
        gv_out.append(vn[t_p:].reshape(dbatch, ds, GM_GROUPS, GM_GROUP_DIM))

    return (xp.reshape(batch, seq, d), xs.reshape(dbatch, ds, d), jnp.stack(kp_out), jnp.stack(vp_out), jnp.stack(ks_out),
            jnp.stack(vs_out), jnp.stack(gv_out))
```

```python
import functools

import jax
import jax.numpy as jnp
import numpy as np
from jax import lax
from jax.experimental import pallas as pl
from jax.experimental.pallas import tpu as pltpu

F32 = jnp.float32
BF16 = jnp.bfloat16

CHUNK = 64
N_HEADS = 16
N_KV_HEADS = 4
HEAD_DIM = 64
Q_PER_KV = N_HEADS // N_KV_HEADS
ATTN_WIDTH = N_HEADS * HEAD_DIM
KV_WIDTH = N_KV_HEADS * HEAD_DIM
WINDOW = 128
ROT_DIM = HEAD_DIM // 4
ROPE_THETA = 500000.0
GM_CHUNK = 128
GM_GROUPS = 8
GM_GROUP_DIM = 128
GM_WIDTH = GM_GROUPS * GM_GROUP_DIM
N_EXPERTS = 16
N_EXPERT_GROUPS = 4
EXPERTS_PER_GROUP = N_EXPERTS // N_EXPERT_GROUPS
NORM_EPS = 1e-5
NEG_INF = -1e30
LOG2_E = 1.4426950408889634
PAST_LEN = 4096

LANES = 128
KEY_BLOCK = 256
Q_BLOCK = 128
EXPERT_ROWS = 512
VMEM_LIMIT = 56 * 1024 * 1024


def _cparams(sem):
    return pltpu.CompilerParams(dimension_semantics=sem, vmem_limit_bytes=VMEM_LIMIT)


def _resident(shape, index_map):
    return pl.BlockSpec(shape, index_map, pipeline_mode=pl.Buffered(1))


def _gelu(a):
    return 0.5 * a * (1.0 + lax.erf(a * np.float32(0.7071067811865476)))


def _pair_expand(t):
    r = pltpu.roll(t, HEAD_DIM, 1)
    lane = lax.broadcasted_iota(jnp.int32, t.shape, 1)
    lo = lane < HEAD_DIM
    return jnp.where(lo, t, r), jnp.where(lo, r, t)


def _expand_kv(a):
    a0, b0 = _pair_expand(a[:, 0:LANES])
    a1, b1 = _pair_expand(a[:, LANES:2 * LANES])
    return jnp.concatenate([a0, b0, a1, b1], axis=1)


def _pick_rows(first_ref, second_ref, n_first):
    return jnp.where(pl.program_id(0) < n_first, first_ref[...], second_ref[...])


def _split_specs(block, n_first):
    return [pl.BlockSpec(block, lambda i: (jnp.minimum(i, n_first - 1), 0)),
            pl.BlockSpec(block, lambda i: (jnp.maximum(i - n_first, 0), 0))]


def _gate_mix(w, y, rows):
    nchunk = y.shape[0] // rows
    pieces = []
    for n in range(nchunk):
        piece = y[n * rows:(n + 1) * rows, :]
        if rows < GM_CHUNK:
            piece = jnp.concatenate([piece, jnp.zeros((GM_CHUNK - rows, GM_GROUP_DIM), F32)], axis=0)
        pieces.append(piece.astype(BF16))
    mix = jnp.dot(w[0:rows, :], jnp.concatenate(pieces, axis=1), preferred_element_type=F32)
    return jnp.concatenate([mix[:, n * GM_GROUP_DIM:(n + 1) * GM_GROUP_DIM] for n in range(nchunk)], axis=0)


def _inproj_body(x_ref, w_ref, rope_ref, lng_ref, lnb_ref, ws_ref, bias_ref, ggm_ref,
                 q_ref, k_ref, v_ref, kp_ref, vp_ref, gm_ref, vn_ref, gu_s, gm_s, *, rows):
    xb = x_ref[...].astype(BF16)
    tm = xb.shape[0]
    cs = rope_ref[:, 0:LANES]
    s1 = rope_ref[:, LANES:2 * LANES]
    s2 = rope_ref[:, 2 * LANES:3 * LANES]

    def rope(a):
        n = a.shape[1] // LANES
        width = a.shape[1]
        c = jnp.concatenate([cs] * n, axis=1)
        m1 = jnp.concatenate([s1] * n, axis=1)
        m2 = jnp.concatenate([s2] * n, axis=1)
        return a * c + pltpu.roll(a, width - ROT_DIM // 2, 1) * m1 + pltpu.roll(a, ROT_DIM // 2, 1) * m2

    def proj(c0, c1):
        return jnp.dot(xb, w_ref[0, :, c0:c1], preferred_element_type=F32)

    half = ATTN_WIDTH // 2
    for h in range(2):
        q = rope(proj(h * half, (h + 1) * half)) * np.float32(HEAD_DIM ** -0.5 * LOG2_E)
        q_ref[:, h * half:(h + 1) * half] = q.astype(BF16)

    kv = proj(ATTN_WIDTH, ATTN_WIDTH + 2 * KV_WIDTH)
    k = rope(kv[:, 0:KV_WIDTH])
    v = kv[:, KV_WIDTH:2 * KV_WIDTH]
    k_ref[...] = k
    v_ref[...] = v
    kp_ref[...] = _expand_kv(k).astype(BF16)
    vp_ref[...] = _expand_kv(v).astype(BF16)

    base = ATTN_WIDTH + 2 * KV_WIDTH
    for h in range(2):
        gu_s[:, h * half:(h + 1) * half] = _gelu(proj(base + h * half, base + (h + 1) * half))

    ci = lax.broadcasted_iota(jnp.int32, (GM_CHUNK, GM_CHUNK), 0) // CHUNK
    cj = lax.broadcasted_iota(jnp.int32, (GM_CHUNK, GM_CHUNK), 1) // CHUNK
    visible = ci >= cj
    base = base + GM_WIDTH
    for h in range(2):
        gv = _gelu(proj(base + h * half, base + (h + 1) * half))
        for g in range(half // GM_GROUP_DIM):
            t = gv[:, g * GM_GROUP_DIM:(g + 1) * GM_GROUP_DIM]
            col = h * half + g * GM_GROUP_DIM
            tc = t - jnp.mean(t, axis=-1, keepdims=True)
            var = jnp.mean(tc * tc, axis=-1, keepdims=True)
            y = tc * lax.rsqrt(var + NORM_EPS) * lng_ref[:, col:col + GM_GROUP_DIM] \
                + lnb_ref[:, col:col + GM_GROUP_DIM]
            vn_ref[:, col:col + GM_GROUP_DIM] = y
            w = jnp.where(visible, ws_ref[col // GM_GROUP_DIM], np.float32(0.0)).astype(BF16)
            b = jnp.concatenate([bias_ref[0:rows, col:col + GM_GROUP_DIM]] * (tm // rows), axis=0)
            gm_s[:, col:col + GM_GROUP_DIM] = gu_s[:, col:col + GM_GROUP_DIM] * (_gate_mix(w, y, rows) + b)

    gm = gm_s[...]
    ms = jnp.mean(gm * gm, axis=-1, keepdims=True)
    gm_ref[...] = (gm * lax.rsqrt(ms + NORM_EPS) * ggm_ref[...]).astype(BF16)


def _inproj(x, w, layer, rope_tab, lng, lnb, ws, bias_full, g_gm, tm, rows):
    t_all, d = x.shape
    n_in = w.shape[2]
    row = lambda i: (i, 0)
    pick = lambda i: (layer, 0, 0)
    out_shape = (
        jax.ShapeDtypeStruct((t_all, ATTN_WIDTH), BF16),
        jax.ShapeDtypeStruct((t_all, KV_WIDTH), F32),
        jax.ShapeDtypeStruct((t_all, KV_WIDTH), F32),
        jax.ShapeDtypeStruct((t_all, 2 * KV_WIDTH), BF16),
        jax.ShapeDtypeStruct((t_all, 2 * KV_WIDTH), BF16),
        jax.ShapeDtypeStruct((t_all, GM_WIDTH), BF16),
        jax.ShapeDtypeStruct((t_all, GM_WIDTH), F32),
    )
    return pl.pallas_call(
        functools.partial(_inproj_body, rows=rows),
        grid=(t_all // tm,),
        in_specs=[
            pl.BlockSpec((tm, d), row),
            _resident((1, d, n_in), pick),
            pl.BlockSpec((tm, 3 * LANES), row),
            _resident((None, 1, GM_WIDTH), pick),
            _resident((None, 1, GM_WIDTH), pick),
            _resident((None, GM_GROUPS, GM_CHUNK, GM_CHUNK), lambda i: (layer, 0, 0, 0)),
            _resident((None, GM_CHUNK, GM_WIDTH), pick),
            _resident((None, 1, GM_WIDTH), pick),
        ],
        out_specs=(
            pl.BlockSpec((tm, ATTN_WIDTH), row),
            pl.BlockSpec((tm, KV_WIDTH), row),
            pl.BlockSpec((tm, KV_WIDTH), row),
            pl.BlockSpec((tm, 2 * KV_WIDTH), row),
            pl.BlockSpec((tm, 2 * KV_WIDTH), row),
            pl.BlockSpec((tm, GM_WIDTH), row),
            pl.BlockSpec((tm, GM_WIDTH), row),
        ),
        out_shape=out_shape,
        scratch_shapes=[pltpu.VMEM((tm, GM_WIDTH), F32), pltpu.VMEM((tm, GM_WIDTH), F32)],
        compiler_params=_cparams(("parallel",)),
        name="inproj",
    )(x, w, rope_tab, lng, lnb, ws, bias_full, g_gm)


def _attn_group(qg, kexp, vexp, bias, sinks):
    nk = kexp.shape[0]
    seg = lax.broadcasted_iota(jnp.int32, kexp.shape, 1) // HEAD_DIM
    zero = jnp.zeros_like(kexp)
    kbd = jnp.concatenate([jnp.where(seg == j, kexp, zero) for j in range(Q_PER_KV)], axis=0)
    vbd = jnp.concatenate([jnp.where(seg == j, vexp, zero) for j in range(Q_PER_KV)], axis=0)
    s = lax.dot_general(qg, kbd, (((1,), (1,)), ((), ())), preferred_element_type=F32)
    es, rs = [], []
    for j in range(Q_PER_KV):
        sj = s[:, j * nk:(j + 1) * nk] + bias
        sink = sinks[j] * np.float32(LOG2_E)
        m = jnp.maximum(jnp.max(sj, axis=-1, keepdims=True), sink)
        e = jnp.exp2(sj - m)
        den = jnp.sum(e, axis=-1, keepdims=True) + jnp.exp2(sink - m)
        es.append(e.astype(BF16))
        rs.append(1.0 / den)
    p = jnp.concatenate(es, axis=1)
    o = jnp.dot(p, vbd, preferred_element_type=F32)
    oseg = lax.broadcasted_iota(jnp.int32, o.shape, 1) // HEAD_DIM
    r = jnp.where(oseg == 0, rs[0], jnp.where(oseg == 1, rs[1], jnp.where(oseg == 2, rs[2], rs[3])))
    return o * r


def _attn_rows(q, kbuf, vbuf, bias, sink_ref, g_row):
    outs = []
    for h in range(N_KV_HEADS):
        kt = kbuf[:, h * LANES:(h + 1) * LANES]
        vt = vbuf[:, h * LANES:(h + 1) * LANES]
        kexp = jnp.concatenate([kt, kt], axis=1)
        vexp = jnp.concatenate([vt, vt], axis=1)
        sinks = [sink_ref[h * Q_PER_KV + j] for j in range(Q_PER_KV)]
        width = Q_PER_KV * HEAD_DIM
        outs.append(_attn_group(q[:, h * width:(h + 1) * width], kexp, vexp, bias, sinks))
    o = jnp.concatenate(outs, axis=1)
    ms = jnp.mean(o * o, axis=-1, keepdims=True)
    return o * lax.rsqrt(ms + NORM_EPS) * g_row


def _attn_prompt_body(sink_ref, q_ref, kc_ref, kprev_ref, vc_ref, vprev_ref, g_ref, o_ref,
                      kbuf, vbuf, *, tq):
    i = pl.program_id(1)
    kbuf[0:WINDOW, :] = kprev_ref[...]
    kbuf[WINDOW:WINDOW + tq, :] = kc_ref[...]
    vbuf[0:WINDOW, :] = vprev_ref[...]
    vbuf[WINDOW:WINDOW + tq, :] = vc_ref[...]
    rq = lax.broadcasted_iota(jnp.int32, (Q_BLOCK, KEY_BLOCK), 0) // CHUNK
    ck = lax.broadcasted_iota(jnp.int32, (Q_BLOCK, KEY_BLOCK), 1) // CHUNK
    rel = ck - rq
    band = (rel >= 0) & (rel <= WINDOW // CHUNK)
    g_row = g_ref[...]

    def sub(s, carry):
        r0 = pl.multiple_of(s * Q_BLOCK, Q_BLOCK)
        first = jnp.logical_and(i == 0, s == 0)
        kmin = jnp.where(first, WINDOW // CHUNK, 0)
        ok = jnp.logical_and(band, ck >= kmin)
        bias = jnp.where(ok, np.float32(0.0), np.float32(NEG_INF))
        q = q_ref[pl.ds(r0, Q_BLOCK), :]
        kb = kbuf[pl.ds(r0, KEY_BLOCK), :]
        vb = vbuf[pl.ds(r0, KEY_BLOCK), :]
        o_ref[pl.ds(r0, Q_BLOCK), :] = _attn_rows(q, kb, vb, bias, sink_ref, g_row).astype(BF16)
        return carry

    lax.fori_loop(0, tq // Q_BLOCK, sub, 0)


def _attn_prompt(sinks, q, kp, vp, g_attn, layer, batch, seq, tq):
    nq = seq // tq
    per = tq // WINDOW
    cur = lambda b, i: (b * nq + i, 0)
    prev = lambda b, i: (b * (seq // WINDOW) + jnp.maximum(i * per - 1, 0), 0)
    kernel = functools.partial(_attn_prompt_body, tq=tq)
    return pl.pallas_call(
        kernel,
        grid=(batch, nq),
        in_specs=[
            pl.BlockSpec(memory_space=pltpu.SMEM),
            pl.BlockSpec((tq, ATTN_WIDTH), cur),
            pl.BlockSpec((tq, 2 * KV_WIDTH), cur),
            pl.BlockSpec((WINDOW, 2 * KV_WIDTH), prev),
            pl.BlockSpec((tq, 2 * KV_WIDTH), cur),
            pl.BlockSpec((WINDOW, 2 * KV_WIDTH), prev),
            pl.BlockSpec((None, 1, ATTN_WIDTH), lambda b, i: (layer, 0, 0)),
        ],
        out_specs=pl.BlockSpec((tq, ATTN_WIDTH), cur),
        out_shape=jax.ShapeDtypeStruct((batch * seq, ATTN_WIDTH), BF16),
        scratch_shapes=[pltpu.VMEM((WINDOW + tq, 2 * KV_WIDTH), BF16),
                        pltpu.VMEM((WINDOW + tq, 2 * KV_WIDTH), BF16)],
        compiler_params=_cparams(("parallel", "parallel")),
        name="attn_prompt",
    )(sinks, q, kp, kp, vp, vp, g_attn)


def _attn_sample_body(sink_ref, q_ref, ck_ref, cv_ref, kn_ref, vn_ref, g_ref, o_ref, *, ds):
    pad = jnp.zeros((KEY_BLOCK - WINDOW - ds, KV_WIDTH), F32)
    k_all = jnp.concatenate([ck_ref[0], kn_ref[...], pad], axis=0)
    v_all = jnp.concatenate([cv_ref[0], vn_ref[...], pad], axis=0)
    kb = _expand_kv(k_all).astype(BF16)
    vb = _expand_kv(v_all).astype(BF16)
    col = lax.broadcasted_iota(jnp.int32, (ds, KEY_BLOCK), 1)
    bias = jnp.where(col < WINDOW + ds, np.float32(0.0), np.float32(NEG_INF))
    o_ref[...] = _attn_rows(q_ref[...], kb, vb, bias, sink_ref, g_ref[...]).astype(BF16)


def _attn_sample(sinks, q, k32, v32, cache_k, cache_v, g_attn, layer, row0, dbatch, ds):
    off = row0 // ds
    new = lambda b: (off + b, 0)
    kernel = functools.partial(_attn_sample_body, ds=ds)
    return pl.pallas_call(
        kernel,
        grid=(dbatch,),
        in_specs=[
            pl.BlockSpec(memory_space=pltpu.SMEM),
            pl.BlockSpec((ds, ATTN_WIDTH), new),
            pl.BlockSpec((None, 1, WINDOW, KV_WIDTH), lambda b: (layer, b, 0, 0)),
            pl.BlockSpec((None, 1, WINDOW, KV_WIDTH), lambda b: (layer, b, 0, 0)),
            pl.BlockSpec((ds, KV_WIDTH), new),
            pl.BlockSpec((ds, KV_WIDTH), new),
            pl.BlockSpec((None, 1, ATTN_WIDTH), lambda b: (layer, 0, 0)),
        ],
        out_specs=pl.BlockSpec((ds, ATTN_WIDTH), lambda b: (b, 0)),
        out_shape=jax.ShapeDtypeStruct((dbatch * ds, ATTN_WIDTH), BF16),
        compiler_params=_cparams(("parallel",)),
        name="attn_sample",
    )(sinks, q, cache_k, cache_v, k32, v32, g_attn)


def _layer_norm_rows(z, g, b):
    zc = z - jnp.mean(z, axis=-1, keepdims=True)
    var = jnp.mean(zc * zc, axis=-1, keepdims=True)
    return zc * lax.rsqrt(var + NORM_EPS) * g + b


def _tree_sum(items):
    while len(items) > 1:
        items = [a + b for a, b in zip(items[0::2], items[1::2])] + (items[-1:] if len(items) % 2 else [])
    return items[0]


def _first_max(items):
    while len(items) > 1:
        nxt = [tuple(jnp.where(a[0] >= b[0], x, y) for x, y in zip(a, b))
               for a, b in zip(items[0::2], items[1::2])]
        items = nxt + (items[-1:] if len(items) % 2 else [])
    return items[0]


def _route_rows(lg, rb):
    sc = 1.0 / (1.0 + jnp.exp(-lg))
    sel = sc + rb
    rows = [sel[e:e + 1, :] for e in range(N_EXPERTS)]
    scr = [sc[e:e + 1, :] for e in range(N_EXPERTS)]
    groups = []
    for g in range(N_EXPERT_GROUPS):
        a, b, c, d = rows[g * EXPERTS_PER_GROUP:(g + 1) * EXPERTS_PER_GROUP]
        hi1, lo1 = jnp.maximum(a, b), jnp.minimum(a, b)
        hi2, lo2 = jnp.maximum(c, d), jnp.minimum(c, d)
        top1 = jnp.maximum(hi1, hi2)
        top2 = jnp.maximum(jnp.minimum(hi1, hi2), jnp.maximum(lo1, lo2))
        members = tuple(rows[g * EXPERTS_PER_GROUP + j] for j in range(EXPERTS_PER_GROUP))
        scores = tuple(scr[g * EXPERTS_PER_GROUP + j] for j in range(EXPERTS_PER_GROUP))
        groups.append((top1 + top2, jnp.full_like(top1, g * EXPERTS_PER_GROUP)) + members + scores)
    best = _first_max(groups)
    first_id = best[1]
    cand = [(best[2 + j], best[2 + EXPERTS_PER_GROUP + j], first_id + np.float32(j))
            for j in range(EXPERTS_PER_GROUP)]
    _, g0, idx0 = _first_max(cand)
    neg = np.float32(-np.inf)
    cand2 = [(jnp.where(c[2] == idx0, neg, c[0]), c[1], c[2]) for c in cand]
    _, g1, idx1 = _first_max(cand2)
    tot = g0 + g1
    return idx0, idx1, g0 / tot, g1 / tot


def _outproj_body(ap_ref, as_ref, gp_ref, gs_ref, xp_ref, xs_ref, wo_ref, g1_ref, b1_ref, rwt_ref, rb_ref,
                  x1_ref, mi_ref, mf_ref, cnt_ref, carry, *, alpha, n_first):
    i = pl.program_id(0)
    tm = xp_ref.shape[0]

    @pl.when(i == 0)
    def _():
        carry[...] = jnp.zeros_like(carry)

    y = jnp.dot(_pick_rows(ap_ref, as_ref, n_first), wo_ref[0, 0:ATTN_WIDTH, :], preferred_element_type=F32)
    y = y + jnp.dot(_pick_rows(gp_ref, gs_ref, n_first), wo_ref[0, ATTN_WIDTH:ATTN_WIDTH + GM_WIDTH, :],
                    preferred_element_type=F32)
    x1 = _layer_norm_rows(np.float32(alpha) * _pick_rows(xp_ref, xs_ref, n_first) + y,
                          g1_ref[...], b1_ref[...])
    x1_ref[...] = x1

    lg = lax.dot_general(rwt_ref[...], x1.astype(BF16), (((1,), (1,)), ((), ())),
                         preferred_element_type=F32)
    rb = jnp.concatenate([rb_ref[...]] * (tm // LANES), axis=1)
    idx0, idx1, gate0, gate1 = _route_rows(lg, rb)

    one, zero = np.float32(1.0), np.float32(0.0)
    hit0 = [idx0 == np.float32(e) for e in range(N_EXPERTS)]
    hit1 = [idx1 == np.float32(e) for e in range(N_EXPERTS)]
    chosen = jnp.concatenate(
        [jnp.where(jnp.logical_or(hit0[e], hit1[e]), one, zero) for e in range(N_EXPERTS)], axis=0)
    s_idx = lax.broadcasted_iota(jnp.int32, (tm, tm), 0)
    t_idx = lax.broadcasted_iota(jnp.int32, (tm, tm), 1)
    before = jnp.where(s_idx < t_idx, np.float32(1.0), zero).astype(BF16)
    base = carry[...]
    rank = jnp.dot(chosen.astype(BF16), before, preferred_element_type=F32) \
        + jnp.concatenate([base] * (tm // LANES), axis=1)
    carry[...] = base + jnp.sum(chosen, axis=1, keepdims=True)
    cnt_ref[...] = carry[...]

    rank_rows = [rank[e:e + 1, :] for e in range(N_EXPERTS)]
    rank0 = _tree_sum([jnp.where(hit0[e], rank_rows[e], zero) for e in range(N_EXPERTS)])
    rank1 = _tree_sum([jnp.where(hit1[e], rank_rows[e], zero) for e in range(N_EXPERTS)])
    zr = jnp.zeros((1, tm), F32)
    mi_ref[...] = jnp.concatenate([idx0, idx1, rank0, rank1, zr, zr, zr, zr], axis=0).astype(jnp.int32)
    mf_ref[...] = jnp.concatenate([gate0, gate1, zr, zr, zr, zr, zr, zr], axis=0)


def _outproj(attn_p, attn_s, gm_p, gm_s, xp, xs, wo, layer, g1, b1, rwt, rb, alpha, tm):
    d = xp.shape[1]
    t_all = xp.shape[0] + xs.shape[0]
    n_first = xp.shape[0] // tm
    row = lambda i: (i, 0)
    col = lambda i: (0, i)
    const = lambda i: (0, 0)
    kernel = functools.partial(_outproj_body, alpha=alpha, n_first=n_first)
    return pl.pallas_call(
        kernel,
        grid=(t_all // tm,),
        in_specs=_split_specs((tm, ATTN_WIDTH), n_first) + _split_specs((tm, GM_WIDTH), n_first)
        + _split_specs((tm, d), n_first) + [
            _resident((1, ATTN_WIDTH + GM_WIDTH, d), lambda i: (layer, 0, 0)),
            _resident((None, 1, d), lambda i: (layer, 0, 0)),
            _resident((None, 1, d), lambda i: (layer, 0, 0)),
            _resident((N_EXPERTS, d), const),
            _resident((N_EXPERTS, LANES), const),
        ],
        out_specs=(
            pl.BlockSpec((tm, d), row),
            pl.BlockSpec((8, tm), col),
            pl.BlockSpec((8, tm), col),
            pl.BlockSpec((N_EXPERTS, LANES), const),
        ),
        out_shape=(
            jax.ShapeDtypeStruct((t_all, d), F32),
            jax.ShapeDtypeStruct((8, t_all), jnp.int32),
            jax.ShapeDtypeStruct((8, t_all), F32),
            jax.ShapeDtypeStruct((N_EXPERTS, LANES), F32),
        ),
        scratch_shapes=[pltpu.VMEM((N_EXPERTS, LANES), F32)],
        compiler_params=_cparams(("arbitrary",)),
        name="outproj_router",
    )(attn_p, attn_s, gm_p, gm_s, xp, xs, wo, g1, b1, rwt, rb)


def _dispatch_body(pz_ref, pe_ref, nu_ref, dest_ref, x1_ref, rows_hbm, x1s, zbuf, sem, zsem, *, tb, n_blocks):
    i = pl.program_id(0)
    n = pl.num_programs(0)
    slot = i % 2

    def wait_tile(s):
        for _ in range(2):
            pltpu.make_async_copy(x1s.at[s], rows_hbm.at[pl.ds(0, tb), :], sem.at[s]).wait()

    x1s[slot] = x1_ref[...]

    for s in range(2):
        @pl.when(slot == s)
        def _(s=s):
            for r in range(tb):
                for k in (2, 3):
                    pltpu.make_async_copy(x1s.at[s, pl.ds(r, 1), :],
                                          rows_hbm.at[pl.ds(dest_ref[k, r], 1), :], sem.at[s]).start()

    @pl.when(i > 0)
    def _():
        wait_tile(1 - slot)

    @pl.when(i == n - 1)
    def _():
        wait_tile(slot)
        zbuf[...] = jnp.zeros_like(zbuf)

        def row_copy(r):
            return pltpu.make_async_copy(zbuf.at[pl.ds(0, 1), :], rows_hbm.at[pl.ds(r, 1), :], zsem.at[0])

        def block_copy(j):
            r0 = pl.multiple_of(j * EXPERT_ROWS, EXPERT_ROWS)
            return pltpu.make_async_copy(zbuf, rows_hbm.at[pl.ds(r0, EXPERT_ROWS), :], zsem.at[0])

        for e in range(N_EXPERTS):
            lax.fori_loop(pz_ref[e], pe_ref[e], lambda r, c: (row_copy(r).start(), c)[1], 0)
        lax.fori_loop(nu_ref[0], n_blocks, lambda j, c: (block_copy(j).start(), c)[1], 0)
        for e in range(N_EXPERTS):
            lax.fori_loop(pz_ref[e], pe_ref[e], lambda r, c: (row_copy(r).wait(), c)[1], 0)
        lax.fori_loop(nu_ref[0], n_blocks, lambda j, c: (block_copy(j).wait(), c)[1], 0)


def _dispatch(pad_lo, pad_hi, n_used, dest, x1, n_blocks, tb):
    t_all, d = x1.shape
    kernel = functools.partial(_dispatch_body, tb=tb, n_blocks=n_blocks)
    grid_spec = pltpu.PrefetchScalarGridSpec(
        num_scalar_prefetch=3,
        grid=(t_all // tb,),
        in_specs=[
            pl.BlockSpec((8, tb), lambda i, pz, pe, nu: (0, i), memory_space=pltpu.SMEM),
            pl.BlockSpec((tb, d), lambda i, pz, pe, nu: (i, 0)),
        ],
        out_specs=pl.BlockSpec(memory_space=pl.ANY),
        scratch_shapes=[pltpu.VMEM((2, tb, d), F32), pltpu.VMEM((EXPERT_ROWS, d), F32),
                        pltpu.SemaphoreType.DMA((2,)), pltpu.SemaphoreType.DMA((1,))],
    )
    return pl.pallas_call(
        kernel,
        grid_spec=grid_spec,
        out_shape=jax.ShapeDtypeStruct((n_blocks * EXPERT_ROWS, d), F32),
        compiler_params=_cparams(("arbitrary",)),
        name="dispatch",
    )(pad_lo, pad_hi, n_used, dest, x1)


def _expert_body(be_ref, nu_ref, x_ref, wg_ref, wu_ref, wd_ref, o_ref, wg_s, wu_s, wd_s):
    j = pl.program_id(0)

    @pl.when(jnp.logical_or(j == 0, be_ref[j] != be_ref[jnp.maximum(j - 1, 0)]))
    def _():
        wg_s[...] = wg_ref[0, 0].astype(BF16)
        wu_s[...] = wu_ref[0, 0].astype(BF16)
        wd_s[...] = wd_ref[0, 0].astype(BF16)

    @pl.when(j < nu_ref[0])
    def _():
        xb = x_ref[...].astype(BF16)
        gate = jnp.dot(xb, wg_s[...], preferred_element_type=F32)
        up = jnp.dot(xb, wu_s[...], preferred_element_type=F32)
        h = gate * (1.0 / (1.0 + jnp.exp(-gate))) * up
        o_ref[...] = jnp.dot(h.astype(BF16), wd_s[...], preferred_element_type=F32)

    @pl.when(j >= nu_ref[0])
    def _():
        o_ref[...] = jnp.zeros_like(o_ref)


def _experts(block_expert, n_used, rows, wg, wu, wd, layer, n_blocks):
    d = rows.shape[1]
    de = wg.shape[3]
    grid_spec = pltpu.PrefetchScalarGridSpec(
        num_scalar_prefetch=2,
        grid=(n_blocks,),
        in_specs=[
            pl.BlockSpec((EXPERT_ROWS, d), lambda j, be, nu: (jnp.minimum(j, nu[0] - 1), 0)),
            pl.BlockSpec((1, 1, d, de), lambda j, be, nu: (layer, be[j], 0, 0)),
            pl.BlockSpec((1, 1, d, de), lambda j, be, nu: (layer, be[j], 0, 0)),
            pl.BlockSpec((1, 1, de, d), lambda j, be, nu: (layer, be[j], 0, 0)),
        ],
        out_specs=pl.BlockSpec((EXPERT_ROWS, d), lambda j, be, nu: (j, 0)),
        scratch_shapes=[pltpu.VMEM((d, de), BF16), pltpu.VMEM((d, de), BF16), pltpu.VMEM((de, d), BF16)],
    )
    return pl.pallas_call(
        _expert_body,
        grid_spec=grid_spec,
        out_shape=jax.ShapeDtypeStruct(rows.shape, F32),
        compiler_params=_cparams(("arbitrary",)),
        name="experts",
    )(block_expert, n_used, rows, wg, wu, wd)


def _combine_body(d0_ref, dn_ref, x1_ref, gt_ref, rows_hbm, g2_ref, b2_ref, x2_ref, rbuf, sem, *, alpha):
    i = pl.program_id(0)
    n = pl.num_programs(0)
    tm = x1_ref.shape[0]
    slot = i % 2

    def row_copy(src_row, s, k, r):
        return pltpu.make_async_copy(rows_hbm.at[pl.ds(src_row, 1), :],
                                     rbuf.at[s, k, pl.ds(r, 1), :], sem.at[s])

    @pl.when(i == 0)
    def _():
        def body(r, c):
            for k in range(2):
                row_copy(d0_ref[2 + k, r], 0, k, r).start()
            return c
        lax.fori_loop(0, tm, body, 0, unroll=8)

    def wait_slot(s):
        for k in range(2):
            pltpu.make_async_copy(rows_hbm.at[pl.ds(0, tm), :], rbuf.at[s, k], sem.at[s]).wait()

    for s in range(2):
        @pl.when(jnp.logical_and(i + 1 < n, 1 - slot == s))
        def _(s=s):
            for r in range(tm):
                for k in range(2):
                    row_copy(dn_ref[2 + k, r], s, k, r).start()

    wait_slot(slot)
    y = rbuf[slot, 0] * gt_ref[:, 0:1] + rbuf[slot, 1] * gt_ref[:, 1:2]
    x2_ref[...] = _layer_norm_rows(np.float32(alpha) * x1_ref[...] + y, g2_ref[...], b2_ref[...])


def _combine(meta_i, x1, gates_t, out_rows, g2, b2, layer, alpha, tm, tile0, n_tiles):
    d = x1.shape[1]
    kernel = functools.partial(_combine_body, alpha=alpha)
    return pl.pallas_call(
        kernel,
        grid=(n_tiles,),
        in_specs=[
            pl.BlockSpec((8, tm), lambda i: (0, tile0), memory_space=pltpu.SMEM),
            pl.BlockSpec((8, tm), lambda i: (0, tile0 + jnp.minimum(i + 1, n_tiles - 1)),
                         memory_space=pltpu.SMEM),
            pl.BlockSpec((tm, d), lambda i: (tile0 + i, 0)),
            pl.BlockSpec((tm, 2), lambda i: (tile0 + i, 0)),
            pl.BlockSpec(memory_space=pl.ANY),
            _resident((None, 1, d), lambda i: (layer, 0, 0)),
            _resident((None, 1, d), lambda i: (layer, 0, 0)),
        ],
        out_specs=pl.BlockSpec((tm, d), lambda i: (i, 0)),
        out_shape=jax.ShapeDtypeStruct((n_tiles * tm, d), F32),
        scratch_shapes=[pltpu.VMEM((2, 2, tm, d), F32), pltpu.SemaphoreType.DMA((2,))],
        compiler_params=_cparams(("arbitrary",)),
        name="combine_ln",
    )(meta_i, meta_i, x1, gates_t, out_rows, g2, b2)


def _rope_table(pos):
    half = ROT_DIM // 2
    inv_freq = jnp.power(ROPE_THETA, -2.0 * jnp.arange(half, dtype=F32) / ROT_DIM)
    ang = pos.astype(F32)[:, None] * inv_freq[None, :]
    cos, sin = jnp.cos(ang), jnp.sin(ang)
    n = pos.shape[0]
    ones = jnp.ones((n, HEAD_DIM - ROT_DIM), F32)
    zeros = jnp.zeros((n, HEAD_DIM - ROT_DIM), F32)
    zh = jnp.zeros((n, half), F32)
    c = jnp.concatenate([cos, cos, ones], axis=1)
    s1 = jnp.concatenate([-sin, zh, zeros], axis=1)
    s2 = jnp.concatenate([zh, sin, zeros], axis=1)
    rep = LANES // HEAD_DIM
    return jnp.concatenate([jnp.tile(c, (1, rep)), jnp.tile(s1, (1, rep)), jnp.tile(s2, (1, rep))], axis=1)


def _block_plan(meta_i, counts, n_blocks):
    cnt = counts[:, 0].astype(jnp.int32)
    padded = (cnt + EXPERT_ROWS - 1) // EXPERT_ROWS * EXPERT_ROWS
    pend = jnp.cumsum(padded)
    pstart = pend - padded
    n_used = pend[-1] // EXPERT_ROWS
    blk = jnp.minimum(jnp.arange(n_blocks, dtype=jnp.int32), n_used - 1)
    block_expert = jnp.sum((blk[:, None] * EXPERT_ROWS >= pend[None, :]).astype(jnp.int32), axis=1)
    experts = jnp.arange(N_EXPERTS, dtype=jnp.int32)[None, :, None]
    start_of = jnp.sum(jnp.where(meta_i[0:2, None, :] == experts, pstart[None, :, None], 0), axis=1)
    dest = jnp.concatenate([meta_i[0:2], start_of + meta_i[2:4], meta_i[4:8]], axis=0)
    return block_expert, n_used.reshape(1), pstart + cnt, pend, dest


def kernel(x_prompt, x_sample, cache_k, cache_v, w_in, w_o, attn_sinks, attn_out_g, gm_out_g,
           gm_ln_g, gm_ln_b, gm_ws, gm_bs, ln1_g, ln1_b, ln2_g, ln2_b, router_w, router_b,
           w_gate, w_up, w_down):
    depth = w_in.shape[0]
    batch, seq, d = x_prompt.shape
    dbatch, ds, _ = x_sample.shape
    alpha = (2 * depth) ** 0.25
    t_p, t_s = batch * seq, dbatch * ds
    t_all = t_p + t_s
    tm = 256 if (t_p % 256 == 0 and t_s % 256 == 0) else 128
    tm_route = 512 if (t_p % 512 == 0 and t_s % 512 == 0) else tm
    w_in_b, w_o_b = w_in.astype(BF16), w_o.astype(BF16)
    tq = 512 if seq % 512 == 0 else seq
    assert seq % GM_CHUNK == 0 and seq % tq == 0 and t_all % tm == 0 and ds <= CHUNK
    assert tm % GM_CHUNK == 0 and tm % ds == 0
    n_blocks = -(-2 * t_all // EXPERT_ROWS) + N_EXPERTS

    xp, xs = x_prompt.reshape(t_p, d), x_sample.reshape(t_s, d)
    rope_p = jnp.tile(_rope_table(jnp.arange(seq)), (batch, 1))
    rope_s = jnp.tile(_rope_table(PAST_LEN + jnp.arange(ds)), (dbatch, 1))
    rwt = router_w.T.astype(BF16)
    rb = jnp.broadcast_to(router_b.astype(F32)[:, None], (N_EXPERTS, LANES))
    cache_k = cache_k.reshape(depth, dbatch, WINDOW, KV_WIDTH)
    cache_v = cache_v.reshape(depth, dbatch, WINDOW, KV_WIDTH)

    ln_g = gm_ln_g.reshape(depth, 1, GM_WIDTH)
    ln_b = gm_ln_b.reshape(depth, 1, GM_WIDTH)
    g_attn = attn_out_g.reshape(depth, 1, ATTN_WIDTH)
    g_gm = gm_out_g.reshape(depth, 1, GM_WIDTH)
    bias_full = jnp.repeat(jnp.swapaxes(gm_bs, 1, 2), GM_GROUP_DIM, axis=2)
    g1, b1 = ln1_g.reshape(depth, 1, d), ln1_b.reshape(depth, 1, d)
    g2, b2 = ln2_g.reshape(depth, 1, d), ln2_b.reshape(depth, 1, d)

    kp_out, vp_out, ks_out, vs_out, gv_out = [], [], [], [], []
    for l in range(depth):
        q_p, k32, v32, kp, vp, gm_p, _ = _inproj(
            xp, w_in_b, l, rope_p, ln_g, ln_b, gm_ws, bias_full, g_gm, tm, GM_CHUNK)
        q_s, k_s, v_s, _, _, gm_s, vn_s = _inproj(
            xs, w_in_b, l, rope_s, ln_g, ln_b, gm_ws, bias_full, g_gm, tm, ds)
        attn_p = _attn_prompt(attn_sinks[l], q_p, kp, vp, g_attn, l, batch, seq, tq)
        attn_s = _attn_sample(attn_sinks[l], q_s, k_s, v_s, cache_k, cache_v, g_attn, l, 0, dbatch, ds)
        x1, meta_i, meta_f, counts = _outproj(
            attn_p, attn_s, gm_p, gm_s, xp, xs, w_o_b, l, g1, b1, rwt, rb, alpha, tm_route)
        block_expert, n_used, pad_lo, pad_hi, dest = _block_plan(meta_i, counts, n_blocks)
        rows = _dispatch(pad_lo, pad_hi, n_used, dest, x1, n_blocks, tm)
        out_rows = _experts(block_expert, n_used, rows, w_gate, w_up, w_down, l, n_blocks)
        gates_t = meta_f[0:2].T
        xp = _combine(dest, x1, gates_t, out_rows, g2, b2, l, alpha, tm, 0, t_p // tm)
        xs = _combine(dest, x1, gates_t, out_rows, g2, b2, l, alpha, tm, t_p // tm, t_s // tm)

        def last_window(a):
            rows = [lax.slice(a, ((b + 1) * seq - WINDOW, 0), ((b + 1) * seq, KV_WIDTH)) for b in range(batch)]
            return jnp.stack(rows).reshape(batch, WINDOW, N_KV_HEADS, HEAD_DIM)

        kp_out.append(last_window(k32))
        vp_out.append(last_window(v32))
        k_s = k_s.reshape(dbatch, ds, KV_WIDTH)
        v_s = v_s.reshape(dbatch, ds, KV_WIDTH)
        ks_out.append(jnp.concatenate([cache_k[l][:, ds:], k_s], axis=1)
                      .reshape(dbatch, WINDOW, N_KV_HEADS, HEAD_DIM))
        vs_out.append(jnp.concatenate([cache_v[l][:, ds:], v_s], axis=1)
                      .reshape(dbatch, WINDOW, N_KV_HEADS, HEAD_DIM))
        gv_out.append(vn_s.reshape(dbatch, ds, GM_GROUPS, GM_GROUP_DIM))

    return (xp.reshape(batch, seq, d), xs.reshape(dbatch, ds, d), jnp.stack(kp_out), jnp.stack(vp_out), jnp.stack(ks_out),
            jnp.stack(vs_out), jnp.stack(gv_out))
```

```python
import functools

import jax
import jax.numpy as jnp
import numpy as np
from jax import lax
from jax.experimental import pallas as pl
from jax.experimental.pallas import tpu as pltpu

F32 = jnp.float32
BF16 = jnp.bfloat16

CHUNK = 64
N_HEADS = 16
N_KV_HEADS = 4
HEAD_DIM = 64
Q_PER_KV = N_HEADS // N_KV_HEADS
ATTN_WIDTH = N_HEADS * HEAD_DIM
KV_WIDTH = N_KV_HEADS * HEAD_DIM
WINDOW = 128
ROT_DIM = HEAD_DIM // 4
ROPE_THETA = 500000.0
GM_CHUNK = 128
GM_GROUPS = 8
GM_GROUP_DIM = 128
GM_WIDTH = GM_GROUPS * GM_GROUP_DIM
N_EXPERTS = 16
N_EXPERT_GROUPS = 4
EXPERTS_PER_GROUP = N_EXPERTS // N_EXPERT_GROUPS
NORM_EPS = 1e-5
NEG_INF = -1e30
LOG2_E = 1.4426950408889634
PAST_LEN = 4096

LANES = 128
KEY_BLOCK = 256
Q_BLOCK = 128
EXPERT_ROWS = 512
VMEM_LIMIT = 56 * 1024 * 1024


def _cparams(sem):
    return pltpu.CompilerParams(dimension_semantics=sem, vmem_limit_bytes=VMEM_LIMIT)


def _resident(shape, index_map):
    return pl.BlockSpec(shape, index_map, pipeline_mode=pl.Buffered(1))


def _gelu(a):
    return 0.5 * a * (1.0 + lax.erf(a * np.float32(0.7071067811865476)))


def _pair_expand(t):
    r = pltpu.roll(t, HEAD_DIM, 1)
    lane = lax.broadcasted_iota(jnp.int32, t.shape, 1)
    lo = lane < HEAD_DIM
    return jnp.where(lo, t, r), jnp.where(lo, r, t)


def _expand_kv(a):
    a0, b0 = _pair_expand(a[:, 0:LANES])
    a1, b1 = _pair_expand(a[:, LANES:2 * LANES])
    return jnp.concatenate([a0, b0, a1, b1], axis=1)


def _pick_rows(first_ref, second_ref, n_first):
    return jnp.where(pl.program_id(0) < n_first, first_ref[...], second_ref[...])


def _split_specs(block, n_first):
    return [pl.BlockSpec(block, lambda i: (jnp.minimum(i, n_first - 1), 0)),
            pl.BlockSpec(block, lambda i: (jnp.maximum(i - n_first, 0), 0))]


def _gate_mix(w, y, rows):
    nchunk = y.shape[0] // rows
    pieces = []
    for n in range(nchunk):
        piece = y[n * rows:(n + 1) * rows, :]
        if rows < GM_CHUNK:
            piece = jnp.concatenate([piece, jnp.zeros((GM_CHUNK - rows, GM_GROUP_DIM), F32)], axis=0)
        pieces.append(piece.astype(BF16))
    mix = jnp.dot(w[0:rows, :], jnp.concatenate(pieces, axis=1), preferred_element_type=F32)
    return jnp.concatenate([mix[:, n * GM_GROUP_DIM:(n + 1) * GM_GROUP_DIM] for n in range(nchunk)], axis=0)


def _inproj_body(x_ref, w_ref, rope_ref, lng_ref, lnb_ref, ws_ref, bias_ref, ggm_ref,
                 q_ref, k_ref, v_ref, kp_ref, vp_ref, gm_ref, vn_ref, gu_s, gm_s, *, rows):
    xb = x_ref[...].astype(BF16)
    tm = xb.shape[0]
    cs = rope_ref[:, 0:LANES]
    s1 = rope_ref[:, LANES:2 * LANES]
    s2 = rope_ref[:, 2 * LANES:3 * LANES]

    def rope(a):
        n = a.shape[1] // LANES
        width = a.shape[1]
        c = jnp.concatenate([cs] * n, axis=1)
        m1 = jnp.concatenate([s1] * n, axis=1)
        m2 = jnp.concatenate([s2] * n, axis=1)
        return a * c + pltpu.roll(a, width - ROT_DIM // 2, 1) * m1 + pltpu.roll(a, ROT_DIM // 2, 1) * m2

    def proj(c0, c1):
        return jnp.dot(xb, w_ref[0, :, c0:c1], preferred_element_type=F32)

    half = ATTN_WIDTH // 2
    for h in range(2):
        q = rope(proj(h * half, (h + 1) * half)) * np.float32(HEAD_DIM ** -0.5 * LOG2_E)
        q_ref[:, h * half:(h + 1) * half] = q.astype(BF16)

    kv = proj(ATTN_WIDTH, ATTN_WIDTH + 2 * KV_WIDTH)
    k = rope(kv[:, 0:KV_WIDTH])
    v = kv[:, KV_WIDTH:2 * KV_WIDTH]
    k_ref[...] = k
    v_ref[...] = v
    kp_ref[...] = _expand_kv(k).astype(BF16)
    vp_ref[...] = _expand_kv(v).astype(BF16)

    base = ATTN_WIDTH + 2 * KV_WIDTH
    for h in range(2):
        gu_s[:, h * half:(h + 1) * half] = _gelu(proj(base + h * half, base + (h + 1) * half))

    ci = lax.broadcasted_iota(jnp.int32, (GM_CHUNK, GM_CHUNK), 0) // CHUNK
    cj = lax.broadcasted_iota(jnp.int32, (GM_CHUNK, GM_CHUNK), 1) // CHUNK
    visible = ci >= cj
    base = base + GM_WIDTH
    for h in range(2):
        gv = _gelu(proj(base + h * half, base + (h + 1) * half))
        for g in range(half // GM_GROUP_DIM):
            t = gv[:, g * GM_GROUP_DIM:(g + 1) * GM_GROUP_DIM]
            col = h * half + g * GM_GROUP_DIM
            tc = t - jnp.mean(t, axis=-1, keepdims=True)
            var = jnp.mean(tc * tc, axis=-1, keepdims=True)
            y = tc * lax.rsqrt(var + NORM_EPS) * lng_ref[:, col:col + GM_GROUP_DIM] \
                + lnb_ref[:, col:col + GM_GROUP_DIM]
            vn_ref[:, col:col + GM_GROUP_DIM] = y
            w = jnp.where(visible, ws_ref[col // GM_GROUP_DIM], np.float32(0.0)).astype(BF16)
            b = jnp.concatenate([bias_ref[0:rows, col:col + GM_GROUP_DIM]] * (tm // rows), axis=0)
            gm_s[:, col:col + GM_GROUP_DIM] = gu_s[:, col:col + GM_GROUP_DIM] * (_gate_mix(w, y, rows) + b)

    gm = gm_s[...]
    ms = jnp.mean(gm * gm, axis=-1, keepdims=True)
    gm_ref[...] = (gm * lax.rsqrt(ms + NORM_EPS) * ggm_ref[...]).astype(BF16)


def _inproj(x, w, layer, rope_tab, lng, lnb, ws, bias_full, g_gm, tm, rows):
    t_all, d = x.shape
    n_in = w.shape[2]
    row = lambda i: (i, 0)
    pick = lambda i: (layer, 0, 0)
    out_shape = (
        jax.ShapeDtypeStruct((t_all, ATTN_WIDTH), BF16),
        jax.ShapeDtypeStruct((t_all, KV_WIDTH), F32),
        jax.ShapeDtypeStruct((t_all, KV_WIDTH), F32),
        jax.ShapeDtypeStruct((t_all, 2 * KV_WIDTH), BF16),
        jax.ShapeDtypeStruct((t_all, 2 * KV_WIDTH), BF16),
        jax.ShapeDtypeStruct((t_all, GM_WIDTH), BF16),
        jax.ShapeDtypeStruct((t_all, GM_WIDTH), F32),
    )
    return pl.pallas_call(
        functools.partial(_inproj_body, rows=rows),
        grid=(t_all // tm,),
        in_specs=[
            pl.BlockSpec((tm, d), row),
            _resident((1, d, n_in), pick),
            pl.BlockSpec((tm, 3 * LANES), row),
            _resident((None, 1, GM_WIDTH), pick),
            _resident((None, 1, GM_WIDTH), pick),
            _resident((None, GM_GROUPS, GM_CHUNK, GM_CHUNK), lambda i: (layer, 0, 0, 0)),
            _resident((None, GM_CHUNK, GM_WIDTH), pick),
            _resident((None, 1, GM_WIDTH), pick),
        ],
        out_specs=(
            pl.BlockSpec((tm, ATTN_WIDTH), row),
            pl.BlockSpec((tm, KV_WIDTH), row),
            pl.BlockSpec((tm, KV_WIDTH), row),
            pl.BlockSpec((tm, 2 * KV_WIDTH), row),
            pl.BlockSpec((tm, 2 * KV_WIDTH), row),
            pl.BlockSpec((tm, GM_WIDTH), row),
            pl.BlockSpec((tm, GM_WIDTH), row),
        ),
        out_shape=out_shape,
        scratch_shapes=[pltpu.VMEM((tm, GM_WIDTH), F32), pltpu.VMEM((tm, GM_WIDTH), F32)],
        compiler_params=_cparams(("parallel",)),
        name="inproj",
    )(x, w, rope_tab, lng, lnb, ws, bias_full, g_gm)


def _attn_group(qg, kexp, vexp, bias, sinks):
    nk = kexp.shape[0]
    seg = lax.broadcasted_iota(jnp.int32, kexp.shape, 1) // HEAD_DIM
    zero = jnp.zeros_like(kexp)
    kbd = jnp.concatenate([jnp.where(seg == j, kexp, zero) for j in range(Q_PER_KV)], axis=0)
    vbd = jnp.concatenate([jnp.where(seg == j, vexp, zero) for j in range(Q_PER_KV)], axis=0)
    s = lax.dot_general(qg, kbd, (((1,), (1,)), ((), ())), preferred_element_type=F32)
    es, rs = [], []
    for j in range(Q_PER_KV):
        sj = s[:, j * nk:(j + 1) * nk] + bias
        sink = sinks[j] * np.float32(LOG2_E)
        m = jnp.maximum(jnp.max(sj, axis=-1, keepdims=True), sink)
        e = jnp.exp2(sj - m)
        den = jnp.sum(e, axis=-1, keepdims=True) + jnp.exp2(sink - m)
        es.append(e.astype(BF16))
        rs.append(1.0 / den)
    p = jnp.concatenate(es, axis=1)
    o = jnp.dot(p, vbd, preferred_element_type=F32)
    oseg = lax.broadcasted_iota(jnp.int32, o.shape, 1) // HEAD_DIM
    r = jnp.where(oseg == 0, rs[0], jnp.where(oseg == 1, rs[1], jnp.where(oseg == 2, rs[2], rs[3])))
    return o * r


def _attn_rows(q, kbuf, vbuf, bias, sink_ref, g_row):
    outs = []
    for h in range(N_KV_HEADS):
        kt = kbuf[:, h * LANES:(h + 1) * LANES]
        vt = vbuf[:, h * LANES:(h + 1) * LANES]
        kexp = jnp.concatenate([kt, kt], axis=1)
        vexp = jnp.concatenate([vt, vt], axis=1)
        sinks = [sink_ref[h * Q_PER_KV + j] for j in range(Q_PER_KV)]
        width = Q_PER_KV * HEAD_DIM
        outs.append(_attn_group(q[:, h * width:(h + 1) * width], kexp, vexp, bias, sinks))
    o = jnp.concatenate(outs, axis=1)
    ms = jnp.mean(o * o, axis=-1, keepdims=True)
    return o * lax.rsqrt(ms + NORM_EPS) * g_row


def _attn_prompt_body(sink_ref, q_ref, kc_ref, kprev_ref, vc_ref, vprev_ref, g_ref, o_ref,
                      kbuf, vbuf, *, tq):
    i = pl.program_id(1)
    kbuf[0:WINDOW, :] = kprev_ref[...]
    kbuf[WINDOW:WINDOW + tq, :] = kc_ref[...]
    vbuf[0:WINDOW, :] = vprev_ref[...]
    vbuf[WINDOW:WINDOW + tq, :] = vc_ref[...]
    rq = lax.broadcasted_iota(jnp.int32, (Q_BLOCK, KEY_BLOCK), 0) // CHUNK
    ck = lax.broadcasted_iota(jnp.int32, (Q_BLOCK, KEY_BLOCK), 1) // CHUNK
    rel = ck - rq
    band = (rel >= 0) & (rel <= WINDOW // CHUNK)
    g_row = g_ref[...]

    def sub(s, carry):
        r0 = pl.multiple_of(s * Q_BLOCK, Q_BLOCK)
        first = jnp.logical_and(i == 0, s == 0)
        kmin = jnp.where(first, WINDOW // CHUNK, 0)
        ok = jnp.logical_and(band, ck >= kmin)
        bias = jnp.where(ok, np.float32(0.0), np.float32(NEG_INF))
        q = q_ref[pl.ds(r0, Q_BLOCK), :]
        kb = kbuf[pl.ds(r0, KEY_BLOCK), :]
        vb = vbuf[pl.ds(r0, KEY_BLOCK), :]
        o_ref[pl.ds(r0, Q_BLOCK), :] = _attn_rows(q, kb, vb, bias, sink_ref, g_row).astype(BF16)
        return carry

    lax.fori_loop(0, tq // Q_BLOCK, sub, 0)


def _attn_prompt(sinks, q, kp, vp, g_attn, layer, batch, seq, tq):
    nq = seq // tq
    per = tq // WINDOW
    cur = lambda b, i: (b * nq + i, 0)
    prev = lambda b, i: (b * (seq // WINDOW) + jnp.maximum(i * per - 1, 0), 0)
    kernel = functools.partial(_attn_prompt_body, tq=tq)
    return pl.pallas_call(
        kernel,
        grid=(batch, nq),
        in_specs=[
            pl.BlockSpec(memory_space=pltpu.SMEM),
            pl.BlockSpec((tq, ATTN_WIDTH), cur),
            pl.BlockSpec((tq, 2 * KV_WIDTH), cur),
            pl.BlockSpec((WINDOW, 2 * KV_WIDTH), prev),
            pl.BlockSpec((tq, 2 * KV_WIDTH), cur),
            pl.BlockSpec((WINDOW, 2 * KV_WIDTH), prev),
            pl.BlockSpec((None, 1, ATTN_WIDTH), lambda b, i: (layer, 0, 0)),
        ],
        out_specs=pl.BlockSpec((tq, ATTN_WIDTH), cur),
        out_shape=jax.ShapeDtypeStruct((batch * seq, ATTN_WIDTH), BF16),
        scratch_shapes=[pltpu.VMEM((WINDOW + tq, 2 * KV_WIDTH), BF16),
                        pltpu.VMEM((WINDOW + tq, 2 * KV_WIDTH), BF16)],
        compiler_params=_cparams(("parallel", "parallel")),
        name="attn_prompt",
    )(sinks, q, kp, kp, vp, vp, g_attn)


def _attn_sample_body(sink_ref, q_ref, ck_ref, cv_ref, kn_ref, vn_ref, g_ref, o_ref, *, ds):
    pad = jnp.zeros((KEY_BLOCK - WINDOW - ds, KV_WIDTH), F32)
    k_all = jnp.concatenate([ck_ref[0], kn_ref[...], pad], axis=0)
    v_all = jnp.concatenate([cv_ref[0], vn_ref[...], pad], axis=0)
    kb = _expand_kv(k_all).astype(BF16)
    vb = _expand_kv(v_all).astype(BF16)
    col = lax.broadcasted_iota(jnp.int32, (ds, KEY_BLOCK), 1)
    bias = jnp.where(col < WINDOW + ds, np.float32(0.0), np.float32(NEG_INF))
    o_ref[...] = _attn_rows(q_ref[...], kb, vb, bias, sink_ref, g_ref[...]).astype(BF16)


def _attn_sample(sinks, q, k32, v32, cache_k, cache_v, g_attn, layer, row0, dbatch, ds):
    off = row0 // ds
    new = lambda b: (off + b, 0)
    kernel = functools.partial(_attn_sample_body, ds=ds)
    return pl.pallas_call(
        kernel,
        grid=(dbatch,),
        in_specs=[
            pl.BlockSpec(memory_space=pltpu.SMEM),
            pl.BlockSpec((ds, ATTN_WIDTH), new),
            pl.BlockSpec((None, 1, WINDOW, KV_WIDTH), lambda b: (layer, b, 0, 0)),
            pl.BlockSpec((None, 1, WINDOW, KV_WIDTH), lambda b: (layer, b, 0, 0)),
            pl.BlockSpec((ds, KV_WIDTH), new),
            pl.BlockSpec((ds, KV_WIDTH), new),
            pl.BlockSpec((None, 1, ATTN_WIDTH), lambda b: (layer, 0, 0)),
        ],
        out_specs=pl.BlockSpec((ds, ATTN_WIDTH), lambda b: (b, 0)),
        out_shape=jax.ShapeDtypeStruct((dbatch * ds, ATTN_WIDTH), BF16),
        compiler_params=_cparams(("parallel",)),
        name="attn_sample",
    )(sinks, q, cache_k, cache_v, k32, v32, g_attn)


def _layer_norm_rows(z, g, b):
    zc = z - jnp.mean(z, axis=-1, keepdims=True)
    var = jnp.mean(zc * zc, axis=-1, keepdims=True)
    return zc * lax.rsqrt(var + NORM_EPS) * g + b


def _tree_sum(items):
    while len(items) > 1:
        items = [a + b for a, b in zip(items[0::2], items[1::2])] + (items[-1:] if len(items) % 2 else [])
    return items[0]


def _first_max(items):
    while len(items) > 1:
        nxt = [tuple(jnp.where(a[0] >= b[0], x, y) for x, y in zip(a, b))
               for a, b in zip(items[0::2], items[1::2])]
        items = nxt + (items[-1:] if len(items) % 2 else [])
    return items[0]


def _route_rows(lg, rb):
    sc = 1.0 / (1.0 + jnp.exp(-lg))
    sel = sc + rb
    rows = [sel[e:e + 1, :] for e in range(N_EXPERTS)]
    scr = [sc[e:e + 1, :] for e in range(N_EXPERTS)]
    groups = []
    for g in range(N_EXPERT_GROUPS):
        a, b, c, d = rows[g * EXPERTS_PER_GROUP:(g + 1) * EXPERTS_PER_GROUP]
        hi1, lo1 = jnp.maximum(a, b), jnp.minimum(a, b)
        hi2, lo2 = jnp.maximum(c, d), jnp.minimum(c, d)
        top1 = jnp.maximum(hi1, hi2)
        top2 = jnp.maximum(jnp.minimum(hi1, hi2), jnp.maximum(lo1, lo2))
        members = tuple(rows[g * EXPERTS_PER_GROUP + j] for j in range(EXPERTS_PER_GROUP))
        scores = tuple(scr[g * EXPERTS_PER_GROUP + j] for j in range(EXPERTS_PER_GROUP))
        groups.append((top1 + top2, jnp.full_like(top1, g * EXPERTS_PER_GROUP)) + members + scores)
    best = _first_max(groups)
    first_id = best[1]
    cand = [(best[2 + j], best[2 + EXPERTS_PER_GROUP + j], first_id + np.float32(j))
            for j in range(EXPERTS_PER_GROUP)]
    _, g0, idx0 = _first_max(cand)
    neg = np.float32(-np.inf)
    cand2 = [(jnp.where(c[2] == idx0, neg, c[0]), c[1], c[2]) for c in cand]
    _, g1, idx1 = _first_max(cand2)
    tot = g0 + g1
    return idx0, idx1, g0 / tot, g1 / tot


def _pack_halves(xb):
    c = xb.shape[1] // 2
    lo = pltpu.bitcast(xb[:, 0:c].astype(F32), jnp.uint32)
    hi = pltpu.bitcast(xb[:, c:2 * c].astype(F32), jnp.uint32)
    return (hi & np.uint32(0xFFFF0000)) | (lo >> np.uint32(16))


def _unpack_halves(w):
    lo = pltpu.bitcast(w << np.uint32(16), F32)
    hi = pltpu.bitcast(w & np.uint32(0xFFFF0000), F32)
    return lo.astype(BF16), hi.astype(BF16)


def _outproj_body(ap_ref, as_ref, gp_ref, gs_ref, xp_ref, xs_ref, wo_ref, g1_ref, b1_ref, rwt_ref, rb_ref,
                  x1_ref, x1p_ref, mi_ref, mf_ref, cnt_ref, carry, *, alpha, n_first):
    i = pl.program_id(0)
    tm = xp_ref.shape[0]

    @pl.when(i == 0)
    def _():
        carry[...] = jnp.zeros_like(carry)

    y = jnp.dot(_pick_rows(ap_ref, as_ref, n_first), wo_ref[0, 0:ATTN_WIDTH, :], preferred_element_type=F32)
    y = y + jnp.dot(_pick_rows(gp_ref, gs_ref, n_first), wo_ref[0, ATTN_WIDTH:ATTN_WIDTH + GM_WIDTH, :],
                    preferred_element_type=F32)
    x1 = _layer_norm_rows(np.float32(alpha) * _pick_rows(xp_ref, xs_ref, n_first) + y,
                          g1_ref[...], b1_ref[...])
    x1_ref[...] = x1
    x1b = x1.astype(BF16)
    x1p_ref[...] = _pack_halves(x1b)

    lg = lax.dot_general(rwt_ref[...], x1b, (((1,), (1,)), ((), ())),
                         preferred_element_type=F32)
    rb = jnp.concatenate([rb_ref[...]] * (tm // LANES), axis=1)
    idx0, idx1, gate0, gate1 = _route_rows(lg, rb)

    one, zero = np.float32(1.0), np.float32(0.0)
    hit0 = [idx0 == np.float32(e) for e in range(N_EXPERTS)]
    hit1 = [idx1 == np.float32(e) for e in range(N_EXPERTS)]
    chosen = jnp.concatenate(
        [jnp.where(jnp.logical_or(hit0[e], hit1[e]), one, zero) for e in range(N_EXPERTS)], axis=0)
    s_idx = lax.broadcasted_iota(jnp.int32, (tm, tm), 0)
    t_idx = lax.broadcasted_iota(jnp.int32, (tm, tm), 1)
    before = jnp.where(s_idx < t_idx, np.float32(1.0), zero).astype(BF16)
    base = carry[...]
    rank = jnp.dot(chosen.astype(BF16), before, preferred_element_type=F32) \
        + jnp.concatenate([base] * (tm // LANES), axis=1)
    carry[...] = base + jnp.sum(chosen, axis=1, keepdims=True)
    cnt_ref[...] = carry[...]

    rank_rows = [rank[e:e + 1, :] for e in range(N_EXPERTS)]
    rank0 = _tree_sum([jnp.where(hit0[e], rank_rows[e], zero) for e in range(N_EXPERTS)])
    rank1 = _tree_sum([jnp.where(hit1[e], rank_rows[e], zero) for e in range(N_EXPERTS)])
    zr = jnp.zeros((1, tm), F32)
    mi_ref[...] = jnp.concatenate([idx0, idx1, rank0, rank1, zr, zr, zr, zr], axis=0).astype(jnp.int32)
    mf_ref[...] = jnp.concatenate([gate0, gate1, zr, zr, zr, zr, zr, zr], axis=0)


def _outproj(attn_p, attn_s, gm_p, gm_s, xp, xs, wo, layer, g1, b1, rwt, rb, alpha, tm):
    d = xp.shape[1]
    t_all = xp.shape[0] + xs.shape[0]
    n_first = xp.shape[0] // tm
    row = lambda i: (i, 0)
    col = lambda i: (0, i)
    const = lambda i: (0, 0)
    kernel = functools.partial(_outproj_body, alpha=alpha, n_first=n_first)
    return pl.pallas_call(
        kernel,
        grid=(t_all // tm,),
        in_specs=_split_specs((tm, ATTN_WIDTH), n_first) + _split_specs((tm, GM_WIDTH), n_first)
        + _split_specs((tm, d), n_first) + [
            _resident((1, ATTN_WIDTH + GM_WIDTH, d), lambda i: (layer, 0, 0)),
            _resident((None, 1, d), lambda i: (layer, 0, 0)),
            _resident((None, 1, d), lambda i: (layer, 0, 0)),
            _resident((N_EXPERTS, d), const),
            _resident((N_EXPERTS, LANES), const),
        ],
        out_specs=(
            pl.BlockSpec((tm, d), row),
            pl.BlockSpec((tm, d // 2), row),
            pl.BlockSpec((8, tm), col),
            pl.BlockSpec((8, tm), col),
            pl.BlockSpec((N_EXPERTS, LANES), const),
        ),
        out_shape=(
            jax.ShapeDtypeStruct((t_all, d), F32),
            jax.ShapeDtypeStruct((t_all, d // 2), jnp.uint32),
            jax.ShapeDtypeStruct((8, t_all), jnp.int32),
            jax.ShapeDtypeStruct((8, t_all), F32),
            jax.ShapeDtypeStruct((N_EXPERTS, LANES), F32),
        ),
        scratch_shapes=[pltpu.VMEM((N_EXPERTS, LANES), F32)],
        compiler_params=_cparams(("arbitrary",)),
        name="outproj_router",
    )(attn_p, attn_s, gm_p, gm_s, xp, xs, wo, g1, b1, rwt, rb)


def _dispatch_body(pz_ref, pe_ref, nu_ref, dest_ref, x1_ref, rows_hbm, x1s, zbuf, sem, zsem, *, tb, n_blocks):
    i = pl.program_id(0)
    n = pl.num_programs(0)
    slot = i % 2

    def wait_tile(s):
        for _ in range(2):
            pltpu.make_async_copy(x1s.at[s], rows_hbm.at[pl.ds(0, tb), :], sem.at[s]).wait()

    x1s[slot] = x1_ref[...]

    for s in range(2):
        @pl.when(slot == s)
        def _(s=s):
            for r in range(tb):
                for k in (2, 3):
                    pltpu.make_async_copy(x1s.at[s, pl.ds(r, 1), :],
                                          rows_hbm.at[pl.ds(dest_ref[k, r], 1), :], sem.at[s]).start()

    @pl.when(i > 0)
    def _():
        wait_tile(1 - slot)

    @pl.when(i == n - 1)
    def _():
        wait_tile(slot)
        zbuf[...] = jnp.zeros_like(zbuf)

        def row_copy(r):
            return pltpu.make_async_copy(zbuf.at[pl.ds(0, 1), :], rows_hbm.at[pl.ds(r, 1), :], zsem.at[0])

        def block_copy(j):
            r0 = pl.multiple_of(j * EXPERT_ROWS, EXPERT_ROWS)
            return pltpu.make_async_copy(zbuf, rows_hbm.at[pl.ds(r0, EXPERT_ROWS), :], zsem.at[0])

        for e in range(N_EXPERTS):
            lax.fori_loop(pz_ref[e], pe_ref[e], lambda r, c: (row_copy(r).start(), c)[1], 0)
        lax.fori_loop(nu_ref[0], n_blocks, lambda j, c: (block_copy(j).start(), c)[1], 0)
        for e in range(N_EXPERTS):
            lax.fori_loop(pz_ref[e], pe_ref[e], lambda r, c: (row_copy(r).wait(), c)[1], 0)
        lax.fori_loop(nu_ref[0], n_blocks, lambda j, c: (block_copy(j).wait(), c)[1], 0)


def _dispatch(pad_lo, pad_hi, n_used, dest, x1, n_blocks, tb):
    t_all, d = x1.shape
    dt = x1.dtype
    kernel = functools.partial(_dispatch_body, tb=tb, n_blocks=n_blocks)
    grid_spec = pltpu.PrefetchScalarGridSpec(
        num_scalar_prefetch=3,
        grid=(t_all // tb,),
        in_specs=[
            pl.BlockSpec((8, tb), lambda i, pz, pe, nu: (0, i), memory_space=pltpu.SMEM),
            pl.BlockSpec((tb, d), lambda i, pz, pe, nu: (i, 0)),
        ],
        out_specs=pl.BlockSpec(memory_space=pl.ANY),
        scratch_shapes=[pltpu.VMEM((2, tb, d), dt), pltpu.VMEM((EXPERT_ROWS, d), dt),
                        pltpu.SemaphoreType.DMA((2,)), pltpu.SemaphoreType.DMA((1,))],
    )
    return pl.pallas_call(
        kernel,
        grid_spec=grid_spec,
        out_shape=jax.ShapeDtypeStruct((n_blocks * EXPERT_ROWS, d), dt),
        compiler_params=_cparams(("arbitrary",)),
        name="dispatch",
    )(pad_lo, pad_hi, n_used, dest, x1)


def _expert_body(be_ref, nu_ref, x_ref, wg_ref, wu_ref, wd_ref, o_ref, wg_s, wu_s, wd_s):
    j = pl.program_id(0)

    @pl.when(jnp.logical_or(j == 0, be_ref[j] != be_ref[jnp.maximum(j - 1, 0)]))
    def _():
        wg_s[...] = wg_ref[0, 0].astype(BF16)
        wu_s[...] = wu_ref[0, 0].astype(BF16)
        wd_s[...] = wd_ref[0, 0].astype(BF16)

    @pl.when(j < nu_ref[0])
    def _():
        lo, hi = _unpack_halves(x_ref[...])
        c = lo.shape[1]
        gate = jnp.dot(lo, wg_s[0:c, :], preferred_element_type=F32) \
            + jnp.dot(hi, wg_s[c:2 * c, :], preferred_element_type=F32)
        up = jnp.dot(lo, wu_s[0:c, :], preferred_element_type=F32) \
            + jnp.dot(hi, wu_s[c:2 * c, :], preferred_element_type=F32)
        h = gate * (1.0 / (1.0 + jnp.exp(-gate))) * up
        o_ref[...] = jnp.dot(h.astype(BF16), wd_s[...], preferred_element_type=F32)

    @pl.when(j >= nu_ref[0])
    def _():
        o_ref[...] = jnp.zeros_like(o_ref)


def _experts(block_expert, n_used, rows, wg, wu, wd, layer, n_blocks):
    d = wg.shape[2]
    de = wg.shape[3]
    grid_spec = pltpu.PrefetchScalarGridSpec(
        num_scalar_prefetch=2,
        grid=(n_blocks,),
        in_specs=[
            pl.BlockSpec((EXPERT_ROWS, d // 2), lambda j, be, nu: (jnp.minimum(j, nu[0] - 1), 0)),
            pl.BlockSpec((1, 1, d, de), lambda j, be, nu: (layer, be[j], 0, 0)),
            pl.BlockSpec((1, 1, d, de), lambda j, be, nu: (layer, be[j], 0, 0)),
            pl.BlockSpec((1, 1, de, d), lambda j, be, nu: (layer, be[j], 0, 0)),
        ],
        out_specs=pl.BlockSpec((EXPERT_ROWS, d), lambda j, be, nu: (j, 0)),
        scratch_shapes=[pltpu.VMEM((d, de), BF16), pltpu.VMEM((d, de), BF16), pltpu.VMEM((de, d), BF16)],
    )
    return pl.pallas_call(
        _expert_body,
        grid_spec=grid_spec,
        out_shape=jax.ShapeDtypeStruct((rows.shape[0], d), F32),
        compiler_params=_cparams(("arbitrary",)),
        name="experts",
    )(block_expert, n_used, rows, wg, wu, wd)


def _combine_body(d0_ref, dn_ref, x1_ref, gt_ref, rows_hbm, g2_ref, b2_ref, x2_ref, rbuf, sem, *, alpha):
    i = pl.program_id(0)
    n = pl.num_programs(0)
    tm = x1_ref.shape[0]
    slot = i % 2

    def row_copy(src_row, s, k, r):
        return pltpu.make_async_copy(rows_hbm.at[pl.ds(src_row, 1), :],
                                     rbuf.at[s, k, pl.ds(r, 1), :], sem.at[s])

    @pl.when(i == 0)
    def _():
        def body(r, c):
            for k in range(2):
                row_copy(d0_ref[2 + k, r], 0, k, r).start()
            return c
        lax.fori_loop(0, tm, body, 0, unroll=8)

    def wait_slot(s):
        for k in range(2):
            pltpu.make_async_copy(rows_hbm.at[pl.ds(0, tm), :], rbuf.at[s, k], sem.at[s]).wait()

    for s in range(2):
        @pl.when(jnp.logical_and(i + 1 < n, 1 - slot == s))
        def _(s=s):
            for r in range(tm):
                for k in range(2):
                    row_copy(dn_ref[2 + k, r], s, k, r).start()

    wait_slot(slot)
    y = rbuf[slot, 0] * gt_ref[:, 0:1] + rbuf[slot, 1] * gt_ref[:, 1:2]
    x2_ref[...] = _layer_norm_rows(np.float32(alpha) * x1_ref[...] + y, g2_ref[...], b2_ref[...])


def _combine(meta_i, x1, gates_t, out_rows, g2, b2, layer, alpha, tm, tile0, n_tiles):
    d = x1.shape[1]
    kernel = functools.partial(_combine_body, alpha=alpha)
    return pl.pallas_call(
        kernel,
        grid=(n_tiles,),
        in_specs=[
            pl.BlockSpec((8, tm), lambda i: (0, tile0), memory_space=pltpu.SMEM),
            pl.BlockSpec((8, tm), lambda i: (0, tile0 + jnp.minimum(i + 1, n_tiles - 1)),
                         memory_space=pltpu.SMEM),
            pl.BlockSpec((tm, d), lambda i: (tile0 + i, 0)),
            pl.BlockSpec((tm, 2), lambda i: (tile0 + i, 0)),
            pl.BlockSpec(memory_space=pl.ANY),
            _resident((None, 1, d), lambda i: (layer, 0, 0)),
            _resident((None, 1, d), lambda i: (layer, 0, 0)),
        ],
        out_specs=pl.BlockSpec((tm, d), lambda i: (i, 0)),
        out_shape=jax.ShapeDtypeStruct((n_tiles * tm, d), F32),
        scratch_shapes=[pltpu.VMEM((2, 2, tm, d), F32), pltpu.SemaphoreType.DMA((2,))],
        compiler_params=_cparams(("arbitrary",)),
        name="combine_ln",
    )(meta_i, meta_i, x1, gates_t, out_rows, g2, b2)


def _rope_table(pos):
    half = ROT_DIM // 2
    inv_freq = jnp.power(ROPE_THETA, -2.0 * jnp.arange(half, dtype=F32) / ROT_DIM)
    ang = pos.astype(F32)[:, None] * inv_freq[None, :]
    cos, sin = jnp.cos(ang), jnp.sin(ang)
    n = pos.shape[0]
    ones = jnp.ones((n, HEAD_DIM - ROT_DIM), F32)
    zeros = jnp.zeros((n, HEAD_DIM - ROT_DIM), F32)
    zh = jnp.zeros((n, half), F32)
    c = jnp.concatenate([cos, cos, ones], axis=1)
    s1 = jnp.concatenate([-sin, zh, zeros], axis=1)
    s2 = jnp.concatenate([zh, sin, zeros], axis=1)
    rep = LANES // HEAD_DIM
    return jnp.concatenate([jnp.tile(c, (1, rep)), jnp.tile(s1, (1, rep)), jnp.tile(s2, (1, rep))], axis=1)


def _block_plan(meta_i, counts, n_blocks):
    cnt = counts[:, 0].astype(jnp.int32)
    padded = (cnt + EXPERT_ROWS - 1) // EXPERT_ROWS * EXPERT_ROWS
    pend = jnp.cumsum(padded)
    pstart = pend - padded
    n_used = pend[-1] // EXPERT_ROWS
    blk = jnp.minimum(jnp.arange(n_blocks, dtype=jnp.int32), n_used - 1)
    block_expert = jnp.sum((blk[:, None] * EXPERT_ROWS >= pend[None, :]).astype(jnp.int32), axis=1)
    experts = jnp.arange(N_EXPERTS, dtype=jnp.int32)[None, :, None]
    start_of = jnp.sum(jnp.where(meta_i[0:2, None, :] == experts, pstart[None, :, None], 0), axis=1)
    dest = jnp.concatenate([meta_i[0:2], start_of + meta_i[2:4], meta_i[4:8]], axis=0)
    return block_expert, n_used.reshape(1), pstart + cnt, pend, dest


def kernel(x_prompt, x_sample, cache_k, cache_v, w_in, w_o, attn_sinks, attn_out_g, gm_out_g,
           gm_ln_g, gm_ln_b, gm_ws, gm_bs, ln1_g, ln1_b, ln2_g, ln2_b, router_w, router_b,
           w_gate, w_up, w_down):
    depth = w_in.shape[0]
    batch, seq, d = x_prompt.shape
    dbatch, ds, _ = x_sample.shape
    alpha = (2 * depth) ** 0.25
    t_p, t_s = batch * seq, dbatch * ds
    t_all = t_p + t_s
    tm = 256 if (t_p % 256 == 0 and t_s % 256 == 0) else 128
    tm_route = 512 if (t_p % 512 == 0 and t_s % 512 == 0) else tm
    w_in_b, w_o_b = w_in.astype(BF16), w_o.astype(BF16)
    tq = 512 if seq % 512 == 0 else seq
    assert seq % GM_CHUNK == 0 and seq % tq == 0 and t_all % tm == 0 and ds <= CHUNK
    assert tm % GM_CHUNK == 0 and tm % ds == 0
    n_blocks = -(-2 * t_all // EXPERT_ROWS) + N_EXPERTS

    xp, xs = x_prompt.reshape(t_p, d), x_sample.reshape(t_s, d)
    rope_p = jnp.tile(_rope_table(jnp.arange(seq)), (batch, 1))
    rope_s = jnp.tile(_rope_table(PAST_LEN + jnp.arange(ds)), (dbatch, 1))
    rwt = router_w.T.astype(BF16)
    rb = jnp.broadcast_to(router_b.astype(F32)[:, None], (N_EXPERTS, LANES))
    cache_k = cache_k.reshape(depth, dbatch, WINDOW, KV_WIDTH)
    cache_v = cache_v.reshape(depth, dbatch, WINDOW, KV_WIDTH)

    ln_g = gm_ln_g.reshape(depth, 1, GM_WIDTH)
    ln_b = gm_ln_b.reshape(depth, 1, GM_WIDTH)
    g_attn = attn_out_g.reshape(depth, 1, ATTN_WIDTH)
    g_gm = gm_out_g.reshape(depth, 1, GM_WIDTH)
    bias_full = jnp.repeat(jnp.swapaxes(gm_bs, 1, 2), GM_GROUP_DIM, axis=2)
    g1, b1 = ln1_g.reshape(depth, 1, d), ln1_b.reshape(depth, 1, d)
    g2, b2 = ln2_g.reshape(depth, 1, d), ln2_b.reshape(depth, 1, d)

    kp_out, vp_out, ks_out, vs_out, gv_out = [], [], [], [], []
    for l in range(depth):
        q_p, k32, v32, kp, vp, gm_p, _ = _inproj(
            xp, w_in_b, l, rope_p, ln_g, ln_b, gm_ws, bias_full, g_gm, tm, GM_CHUNK)
        q_s, k_s, v_s, _, _, gm_s, vn_s = _inproj(
            xs, w_in_b, l, rope_s, ln_g, ln_b, gm_ws, bias_full, g_gm, tm, ds)
        attn_p = _attn_prompt(attn_sinks[l], q_p, kp, vp, g_attn, l, batch, seq, tq)
        attn_s = _attn_sample(attn_sinks[l], q_s, k_s, v_s, cache_k, cache_v, g_attn, l, 0, dbatch, ds)
        x1, x1_packed, meta_i, meta_f, counts = _outproj(
            attn_p, attn_s, gm_p, gm_s, xp, xs, w_o_b, l, g1, b1, rwt, rb, alpha, tm_route)
        block_expert, n_used, pad_lo, pad_hi, dest = _block_plan(meta_i, counts, n_blocks)
        rows = _dispatch(pad_lo, pad_hi, n_used, dest, x1_packed, n_blocks, tm)
        out_rows = _experts(block_expert, n_used, rows, w_gate, w_up, w_down, l, n_blocks)
        gates_t = meta_f[0:2].T
        xp = _combine(dest, x1, gates_t, out_rows, g2, b2, l, alpha, tm, 0, t_p // tm)
        xs = _combine(dest, x1, gates_t, out_rows, g2, b2, l, alpha, tm, t_p // tm, t_s // tm)

        def last_window(a):
            rows = [lax.slice(a, ((b + 1) * seq - WINDOW, 0), ((b + 1) * seq, KV_WIDTH)) for b in range(batch)]
            return jnp.stack(rows).reshape(batch, WINDOW, N_KV_HEADS, HEAD_DIM)

        kp_out.append(last_window(k32))
        vp_out.append(last_window(v32))
        k_s = k_s.reshape(dbatch, ds, KV_WIDTH)
        v_s = v_s.reshape(dbatch, ds, KV_WIDTH)
        ks_out.append(jnp.concatenate([cache_k[l][:, ds:], k_s], axis=1)
                      .reshape(dbatch, WINDOW, N_KV_HEADS, HEAD_DIM))
        vs_out.append(jnp.concatenate([cache_v[l][:, ds:], v_s], axis=1)
                      .reshape(dbatch, WINDOW, N_KV_HEADS, HEAD_DIM))
        gv_out.append(vn_s.reshape(dbatch, ds, GM_GROUPS, GM_GROUP_DIM))

    return (xp.reshape(batch, seq, d), xs.reshape(dbatch, ds, d), jnp.stack(kp_out), jnp.stack(vp_out), jnp.stack(ks_out),
            jnp.stack(vs_out), jnp.stack(gv_out))
```

```python
import functools

import jax
import jax.numpy as jnp
import numpy as np
from jax import lax
from jax.experimental import pallas as pl
from jax.experimental.pallas import tpu as pltpu

F32 = jnp.float32
BF16 = jnp.bfloat16

CHUNK = 64
N_HEADS = 16
N_KV_HEADS = 4
HEAD_DIM = 64
Q_PER_KV = N_HEADS // N_KV_HEADS
ATTN_WIDTH = N_HEADS * HEAD_DIM
KV_WIDTH = N_KV_HEADS * HEAD_DIM
WINDOW = 128
ROT_DIM = HEAD_DIM // 4
ROPE_THETA = 500000.0
GM_CHUNK = 128
GM_GROUPS = 8
GM_GROUP_DIM = 128
GM_WIDTH = GM_GROUPS * GM_GROUP_DIM
N_EXPERTS = 16
N_EXPERT_GROUPS = 4
EXPERTS_PER_GROUP = N_EXPERTS // N_EXPERT_GROUPS
NORM_EPS = 1e-5
NEG_INF = -1e30
LOG2_E = 1.4426950408889634
PAST_LEN = 4096

LANES = 128
SUBLANES = 8
KEY_BLOCK = 256
Q_BLOCK = 128
EXPERT_ROWS = 512
VMEM_LIMIT = 56 * 1024 * 1024


def _cparams(sem):
    return pltpu.CompilerParams(dimension_semantics=sem, vmem_limit_bytes=VMEM_LIMIT)


def _resident(shape, index_map):
    return pl.BlockSpec(shape, index_map, pipeline_mode=pl.Buffered(1))


def _gelu(a):
    return 0.5 * a * (1.0 + lax.erf(a * np.float32(0.7071067811865476)))


def _pair_expand(t):
    r = pltpu.roll(t, HEAD_DIM, 1)
    lane = lax.broadcasted_iota(jnp.int32, t.shape, 1)
    lo = lane < HEAD_DIM
    return jnp.where(lo, t, r), jnp.where(lo, r, t)


def _expand_kv(a):
    a0, b0 = _pair_expand(a[:, 0:LANES])
    a1, b1 = _pair_expand(a[:, LANES:2 * LANES])
    return jnp.concatenate([a0, b0, a1, b1], axis=1)


def _pick_rows(first_ref, second_ref, n_first):
    return jnp.where(pl.program_id(0) < n_first, first_ref[...], second_ref[...])


def _split_specs(block, n_first):
    return [pl.BlockSpec(block, lambda i: (jnp.minimum(i, n_first - 1), 0)),
            pl.BlockSpec(block, lambda i: (jnp.maximum(i - n_first, 0), 0))]


def _gate_mix(w, y, rows):
    nchunk = y.shape[0] // rows
    pieces = []
    for n in range(nchunk):
        piece = y[n * rows:(n + 1) * rows, :]
        if rows < GM_CHUNK:
            piece = jnp.concatenate([piece, jnp.zeros((GM_CHUNK - rows, GM_GROUP_DIM), F32)], axis=0)
        pieces.append(piece.astype(BF16))
    mix = jnp.dot(w[0:rows, :], jnp.concatenate(pieces, axis=1), preferred_element_type=F32)
    return jnp.concatenate([mix[:, n * GM_GROUP_DIM:(n + 1) * GM_GROUP_DIM] for n in range(nchunk)], axis=0)


def _inproj_body(x_ref, w_ref, rope_ref, lng_ref, lnb_ref, ws_ref, bias_ref, ggm_ref,
                 q_ref, k_ref, v_ref, kp_ref, vp_ref, gm_ref, vn_ref, gu_s, gm_s, *, rows):
    xb = x_ref[...].astype(BF16)
    tm = xb.shape[0]
    cs = rope_ref[:, 0:LANES]
    s1 = rope_ref[:, LANES:2 * LANES]
    s2 = rope_ref[:, 2 * LANES:3 * LANES]

    def rope(a):
        n = a.shape[1] // LANES
        width = a.shape[1]
        c = jnp.concatenate([cs] * n, axis=1)
        m1 = jnp.concatenate([s1] * n, axis=1)
        m2 = jnp.concatenate([s2] * n, axis=1)
        return a * c + pltpu.roll(a, width - ROT_DIM // 2, 1) * m1 + pltpu.roll(a, ROT_DIM // 2, 1) * m2

    def proj(c0, c1):
        return jnp.dot(xb, w_ref[0, :, c0:c1], preferred_element_type=F32)

    half = ATTN_WIDTH // 2
    for h in range(2):
        q = rope(proj(h * half, (h + 1) * half)) * np.float32(HEAD_DIM ** -0.5 * LOG2_E)
        q_ref[:, h * half:(h + 1) * half] = q.astype(BF16)

    kv = proj(ATTN_WIDTH, ATTN_WIDTH + 2 * KV_WIDTH)
    k = rope(kv[:, 0:KV_WIDTH])
    v = kv[:, KV_WIDTH:2 * KV_WIDTH]
    k_ref[...] = k
    v_ref[...] = v
    kp_ref[...] = _expand_kv(k).astype(BF16)
    vp_ref[...] = _expand_kv(v).astype(BF16)

    base = ATTN_WIDTH + 2 * KV_WIDTH
    for h in range(2):
        gu_s[:, h * half:(h + 1) * half] = _gelu(proj(base + h * half, base + (h + 1) * half))

    ci = lax.broadcasted_iota(jnp.int32, (GM_CHUNK, GM_CHUNK), 0) // CHUNK
    cj = lax.broadcasted_iota(jnp.int32, (GM_CHUNK, GM_CHUNK), 1) // CHUNK
    visible = ci >= cj
    base = base + GM_WIDTH
    for h in range(2):
        gv = _gelu(proj(base + h * half, base + (h + 1) * half))
        for g in range(half // GM_GROUP_DIM):
            t = gv[:, g * GM_GROUP_DIM:(g + 1) * GM_GROUP_DIM]
            col = h * half + g * GM_GROUP_DIM
            tc = t - jnp.mean(t, axis=-1, keepdims=True)
            var = jnp.mean(tc * tc, axis=-1, keepdims=True)
            y = tc * lax.rsqrt(var + NORM_EPS) * lng_ref[:, col:col + GM_GROUP_DIM] \
                + lnb_ref[:, col:col + GM_GROUP_DIM]
            vn_ref[:, col:col + GM_GROUP_DIM] = y
            w = jnp.where(visible, ws_ref[col // GM_GROUP_DIM], np.float32(0.0)).astype(BF16)
            b = jnp.concatenate([bias_ref[0:rows, col:col + GM_GROUP_DIM]] * (tm // rows), axis=0)
            gm_s[:, col:col + GM_GROUP_DIM] = gu_s[:, col:col + GM_GROUP_DIM] * (_gate_mix(w, y, rows) + b)

    gm = gm_s[...]
    ms = jnp.mean(gm * gm, axis=-1, keepdims=True)
    gm_ref[...] = (gm * lax.rsqrt(ms + NORM_EPS) * ggm_ref[...]).astype(BF16)


def _inproj(x, w, layer, rope_tab, lng, lnb, ws, bias_full, g_gm, tm, rows):
    t_all, d = x.shape
    n_in = w.shape[2]
    row = lambda i: (i, 0)
    pick = lambda i: (layer, 0, 0)
    out_shape = (
        jax.ShapeDtypeStruct((t_all, ATTN_WIDTH), BF16),
        jax.ShapeDtypeStruct((t_all, KV_WIDTH), F32),
        jax.ShapeDtypeStruct((t_all, KV_WIDTH), F32),
        jax.ShapeDtypeStruct((t_all, 2 * KV_WIDTH), BF16),
        jax.ShapeDtypeStruct((t_all, 2 * KV_WIDTH), BF16),
        jax.ShapeDtypeStruct((t_all, GM_WIDTH), BF16),
        jax.ShapeDtypeStruct((t_all, GM_WIDTH), F32),
    )
    return pl.pallas_call(
        functools.partial(_inproj_body, rows=rows),
        grid=(t_all // tm,),
        in_specs=[
            pl.BlockSpec((tm, d), row),
            _resident((1, d, n_in), pick),
            pl.BlockSpec((tm, 3 * LANES), row),
            _resident((None, 1, GM_WIDTH), pick),
            _resident((None, 1, GM_WIDTH), pick),
            _resident((None, GM_GROUPS, GM_CHUNK, GM_CHUNK), lambda i: (layer, 0, 0, 0)),
            _resident((None, GM_CHUNK, GM_WIDTH), pick),
            _resident((None, 1, GM_WIDTH), pick),
        ],
        out_specs=(
            pl.BlockSpec((tm, ATTN_WIDTH), row),
            pl.BlockSpec((tm, KV_WIDTH), row),
            pl.BlockSpec((tm, KV_WIDTH), row),
            pl.BlockSpec((tm, 2 * KV_WIDTH), row),
            pl.BlockSpec((tm, 2 * KV_WIDTH), row),
            pl.BlockSpec((tm, GM_WIDTH), row),
            pl.BlockSpec((tm, GM_WIDTH), row),
        ),
        out_shape=out_shape,
        scratch_shapes=[pltpu.VMEM((tm, GM_WIDTH), F32), pltpu.VMEM((tm, GM_WIDTH), F32)],
        compiler_params=_cparams(("parallel",)),
        name="inproj",
    )(x, w, rope_tab, lng, lnb, ws, bias_full, g_gm)


def _attn_group(qg, kexp, vexp, bias, sinks):
    nk = kexp.shape[0]
    seg = lax.broadcasted_iota(jnp.int32, kexp.shape, 1) // HEAD_DIM
    zero = jnp.zeros_like(kexp)
    kbd = jnp.concatenate([jnp.where(seg == j, kexp, zero) for j in range(Q_PER_KV)], axis=0)
    vbd = jnp.concatenate([jnp.where(seg == j, vexp, zero) for j in range(Q_PER_KV)], axis=0)
    s = lax.dot_general(qg, kbd, (((1,), (1,)), ((), ())), preferred_element_type=F32)
    es, rs = [], []
    for j in range(Q_PER_KV):
        sj = s[:, j * nk:(j + 1) * nk] + bias
        sink = sinks[j] * np.float32(LOG2_E)
        m = jnp.maximum(jnp.max(sj, axis=-1, keepdims=True), sink)
        e = jnp.exp2(sj - m)
        den = jnp.sum(e, axis=-1, keepdims=True) + jnp.exp2(sink - m)
        es.append(e.astype(BF16))
        rs.append(1.0 / den)
    p = jnp.concatenate(es, axis=1)
    o = jnp.dot(p, vbd, preferred_element_type=F32)
    oseg = lax.broadcasted_iota(jnp.int32, o.shape, 1) // HEAD_DIM
    r = jnp.where(oseg == 0, rs[0], jnp.where(oseg == 1, rs[1], jnp.where(oseg == 2, rs[2], rs[3])))
    return o * r


def _attn_rows(q, kbuf, vbuf, bias, sink_ref, g_row):
    outs = []
    for h in range(N_KV_HEADS):
        kt = kbuf[:, h * LANES:(h + 1) * LANES]
        vt = vbuf[:, h * LANES:(h + 1) * LANES]
        kexp = jnp.concatenate([kt, kt], axis=1)
        vexp = jnp.concatenate([vt, vt], axis=1)
        sinks = [sink_ref[h * Q_PER_KV + j] for j in range(Q_PER_KV)]
        width = Q_PER_KV * HEAD_DIM
        outs.append(_attn_group(q[:, h * width:(h + 1) * width], kexp, vexp, bias, sinks))
    o = jnp.concatenate(outs, axis=1)
    ms = jnp.mean(o * o, axis=-1, keepdims=True)
    return o * lax.rsqrt(ms + NORM_EPS) * g_row


def _attn_prompt_body(sink_ref, q_ref, kc_ref, kprev_ref, vc_ref, vprev_ref, g_ref, o_ref,
                      kbuf, vbuf, *, tq):
    i = pl.program_id(1)
    kbuf[0:WINDOW, :] = kprev_ref[...]
    kbuf[WINDOW:WINDOW + tq, :] = kc_ref[...]
    vbuf[0:WINDOW, :] = vprev_ref[...]
    vbuf[WINDOW:WINDOW + tq, :] = vc_ref[...]
    rq = lax.broadcasted_iota(jnp.int32, (Q_BLOCK, KEY_BLOCK), 0) // CHUNK
    ck = lax.broadcasted_iota(jnp.int32, (Q_BLOCK, KEY_BLOCK), 1) // CHUNK
    rel = ck - rq
    band = (rel >= 0) & (rel <= WINDOW // CHUNK)
    g_row = g_ref[...]

    def sub(s, carry):
        r0 = pl.multiple_of(s * Q_BLOCK, Q_BLOCK)
        first = jnp.logical_and(i == 0, s == 0)
        kmin = jnp.where(first, WINDOW // CHUNK, 0)
        ok = jnp.logical_and(band, ck >= kmin)
        bias = jnp.where(ok, np.float32(0.0), np.float32(NEG_INF))
        q = q_ref[pl.ds(r0, Q_BLOCK), :]
        kb = kbuf[pl.ds(r0, KEY_BLOCK), :]
        vb = vbuf[pl.ds(r0, KEY_BLOCK), :]
        o_ref[pl.ds(r0, Q_BLOCK), :] = _attn_rows(q, kb, vb, bias, sink_ref, g_row).astype(BF16)
        return carry

    lax.fori_loop(0, tq // Q_BLOCK, sub, 0)


def _attn_prompt(sinks, q, kp, vp, g_attn, layer, batch, seq, tq):
    nq = seq // tq
    per = tq // WINDOW
    cur = lambda b, i: (b * nq + i, 0)
    prev = lambda b, i: (b * (seq // WINDOW) + jnp.maximum(i * per - 1, 0), 0)
    kernel = functools.partial(_attn_prompt_body, tq=tq)
    return pl.pallas_call(
        kernel,
        grid=(batch, nq),
        in_specs=[
            pl.BlockSpec(memory_space=pltpu.SMEM),
            pl.BlockSpec((tq, ATTN_WIDTH), cur),
            pl.BlockSpec((tq, 2 * KV_WIDTH), cur),
            pl.BlockSpec((WINDOW, 2 * KV_WIDTH), prev),
            pl.BlockSpec((tq, 2 * KV_WIDTH), cur),
            pl.BlockSpec((WINDOW, 2 * KV_WIDTH), prev),
            pl.BlockSpec((None, 1, ATTN_WIDTH), lambda b, i: (layer, 0, 0)),
        ],
        out_specs=pl.BlockSpec((tq, ATTN_WIDTH), cur),
        out_shape=jax.ShapeDtypeStruct((batch * seq, ATTN_WIDTH), BF16),
        scratch_shapes=[pltpu.VMEM((WINDOW + tq, 2 * KV_WIDTH), BF16),
                        pltpu.VMEM((WINDOW + tq, 2 * KV_WIDTH), BF16)],
        compiler_params=_cparams(("parallel", "parallel")),
        name="attn_prompt",
    )(sinks, q, kp, kp, vp, vp, g_attn)


def _attn_sample_body(sink_ref, q_ref, ck_ref, cv_ref, kn_ref, vn_ref, g_ref, o_ref, *, ds):
    pad = jnp.zeros((KEY_BLOCK - WINDOW - ds, KV_WIDTH), F32)
    k_all = jnp.concatenate([ck_ref[0], kn_ref[...], pad], axis=0)
    v_all = jnp.concatenate([cv_ref[0], vn_ref[...], pad], axis=0)
    kb = _expand_kv(k_all).astype(BF16)
    vb = _expand_kv(v_all).astype(BF16)
    col = lax.broadcasted_iota(jnp.int32, (ds, KEY_BLOCK), 1)
    bias = jnp.where(col < WINDOW + ds, np.float32(0.0), np.float32(NEG_INF))
    o_ref[...] = _attn_rows(q_ref[...], kb, vb, bias, sink_ref, g_ref[...]).astype(BF16)


def _attn_sample(sinks, q, k32, v32, cache_k, cache_v, g_attn, layer, row0, dbatch, ds):
    off = row0 // ds
    new = lambda b: (off + b, 0)
    kernel = functools.partial(_attn_sample_body, ds=ds)
    return pl.pallas_call(
        kernel,
        grid=(dbatch,),
        in_specs=[
            pl.BlockSpec(memory_space=pltpu.SMEM),
            pl.BlockSpec((ds, ATTN_WIDTH), new),
            pl.BlockSpec((None, 1, WINDOW, KV_WIDTH), lambda b: (layer, b, 0, 0)),
            pl.BlockSpec((None, 1, WINDOW, KV_WIDTH), lambda b: (layer, b, 0, 0)),
            pl.BlockSpec((ds, KV_WIDTH), new),
            pl.BlockSpec((ds, KV_WIDTH), new),
            pl.BlockSpec((None, 1, ATTN_WIDTH), lambda b: (layer, 0, 0)),
        ],
        out_specs=pl.BlockSpec((ds, ATTN_WIDTH), lambda b: (b, 0)),
        out_shape=jax.ShapeDtypeStruct((dbatch * ds, ATTN_WIDTH), BF16),
        compiler_params=_cparams(("parallel",)),
        name="attn_sample",
    )(sinks, q, cache_k, cache_v, k32, v32, g_attn)


def _layer_norm_rows(z, g, b):
    zc = z - jnp.mean(z, axis=-1, keepdims=True)
    var = jnp.mean(zc * zc, axis=-1, keepdims=True)
    return zc * lax.rsqrt(var + NORM_EPS) * g + b


def _tree_sum(items):
    while len(items) > 1:
        items = [a + b for a, b in zip(items[0::2], items[1::2])] + (items[-1:] if len(items) % 2 else [])
    return items[0]


def _first_max(items):
    while len(items) > 1:
        nxt = [tuple(jnp.where(a[0] >= b[0], x, y) for x, y in zip(a, b))
               for a, b in zip(items[0::2], items[1::2])]
        items = nxt + (items[-1:] if len(items) % 2 else [])
    return items[0]


def _route_rows(lg, rb):
    sc = 1.0 / (1.0 + jnp.exp(-lg))
    sel = sc + rb
    rows = [sel[e:e + 1, :] for e in range(N_EXPERTS)]
    scr = [sc[e:e + 1, :] for e in range(N_EXPERTS)]
    groups = []
    for g in range(N_EXPERT_GROUPS):
        a, b, c, d = rows[g * EXPERTS_PER_GROUP:(g + 1) * EXPERTS_PER_GROUP]
        hi1, lo1 = jnp.maximum(a, b), jnp.minimum(a, b)
        hi2, lo2 = jnp.maximum(c, d), jnp.minimum(c, d)
        top1 = jnp.maximum(hi1, hi2)
        top2 = jnp.maximum(jnp.minimum(hi1, hi2), jnp.maximum(lo1, lo2))
        members = tuple(rows[g * EXPERTS_PER_GROUP + j] for j in range(EXPERTS_PER_GROUP))
        scores = tuple(scr[g * EXPERTS_PER_GROUP + j] for j in range(EXPERTS_PER_GROUP))
        groups.append((top1 + top2, jnp.full_like(top1, g * EXPERTS_PER_GROUP)) + members + scores)
    best = _first_max(groups)
    first_id = best[1]
    cand = [(best[2 + j], best[2 + EXPERTS_PER_GROUP + j], first_id + np.float32(j))
            for j in range(EXPERTS_PER_GROUP)]
    _, g0, idx0 = _first_max(cand)
    neg = np.float32(-np.inf)
    cand2 = [(jnp.where(c[2] == idx0, neg, c[0]), c[1], c[2]) for c in cand]
    _, g1, idx1 = _first_max(cand2)
    tot = g0 + g1
    return idx0, idx1, g0 / tot, g1 / tot


def _pack_halves(xb):
    c = xb.shape[1] // 2
    lo = pltpu.bitcast(xb[:, 0:c].astype(F32), jnp.uint32)
    hi = pltpu.bitcast(xb[:, c:2 * c].astype(F32), jnp.uint32)
    return (hi & np.uint32(0xFFFF0000)) | (lo >> np.uint32(16))


def _unpack_halves(w):
    lo = pltpu.bitcast(w << np.uint32(16), F32)
    hi = pltpu.bitcast(w & np.uint32(0xFFFF0000), F32)
    return lo.astype(BF16), hi.astype(BF16)


def _outproj_body(ap_ref, as_ref, gp_ref, gs_ref, xp_ref, xs_ref, wo_ref, g1_ref, b1_ref, rwt_ref, rb_ref,
                  x1_ref, x1p_ref, mi_ref, mf_ref, cnt_ref, carry, *, alpha, n_first):
    i = pl.program_id(0)
    tm = xp_ref.shape[0]

    @pl.when(i == 0)
    def _():
        carry[...] = jnp.zeros_like(carry)

    y = jnp.dot(_pick_rows(ap_ref, as_ref, n_first), wo_ref[0, 0:ATTN_WIDTH, :], preferred_element_type=F32)
    y = y + jnp.dot(_pick_rows(gp_ref, gs_ref, n_first), wo_ref[0, ATTN_WIDTH:ATTN_WIDTH + GM_WIDTH, :],
                    preferred_element_type=F32)
    x1 = _layer_norm_rows(np.float32(alpha) * _pick_rows(xp_ref, xs_ref, n_first) + y,
                          g1_ref[...], b1_ref[...])
    x1_ref[...] = x1
    x1b = x1.astype(BF16)
    x1p_ref[...] = _pack_halves(x1b)

    lg = lax.dot_general(rwt_ref[...], x1b, (((1,), (1,)), ((), ())),
                         preferred_element_type=F32)
    rb = jnp.concatenate([rb_ref[...]] * (tm // LANES), axis=1)
    idx0, idx1, gate0, gate1 = _route_rows(lg, rb)

    one, zero = np.float32(1.0), np.float32(0.0)
    hit0 = [idx0 == np.float32(e) for e in range(N_EXPERTS)]
    hit1 = [idx1 == np.float32(e) for e in range(N_EXPERTS)]
    chosen = jnp.concatenate(
        [jnp.where(jnp.logical_or(hit0[e], hit1[e]), one, zero) for e in range(N_EXPERTS)], axis=0)
    s_idx = lax.broadcasted_iota(jnp.int32, (tm, tm), 0)
    t_idx = lax.broadcasted_iota(jnp.int32, (tm, tm), 1)
    before = jnp.where(s_idx < t_idx, np.float32(1.0), zero).astype(BF16)
    base = carry[...]
    rank = jnp.dot(chosen.astype(BF16), before, preferred_element_type=F32) \
        + jnp.concatenate([base] * (tm // LANES), axis=1)
    carry[...] = base + jnp.sum(chosen, axis=1, keepdims=True)
    cnt_ref[...] = carry[...]

    rank_rows = [rank[e:e + 1, :] for e in range(N_EXPERTS)]
    rank0 = _tree_sum([jnp.where(hit0[e], rank_rows[e], zero) for e in range(N_EXPERTS)])
    rank1 = _tree_sum([jnp.where(hit1[e], rank_rows[e], zero) for e in range(N_EXPERTS)])
    zr = jnp.zeros((1, tm), F32)
    mi_ref[...] = jnp.concatenate([idx0, idx1, rank0, rank1, zr, zr, zr, zr], axis=0).astype(jnp.int32)
    mf_ref[...] = jnp.concatenate([gate0, gate1, zr, zr, zr, zr, zr, zr], axis=0)


def _outproj(attn_p, attn_s, gm_p, gm_s, xp, xs, wo, layer, g1, b1, rwt, rb, alpha, tm):
    d = xp.shape[1]
    t_all = xp.shape[0] + xs.shape[0]
    n_first = xp.shape[0] // tm
    row = lambda i: (i, 0)
    col = lambda i: (0, i)
    const = lambda i: (0, 0)
    kernel = functools.partial(_outproj_body, alpha=alpha, n_first=n_first)
    return pl.pallas_call(
        kernel,
        grid=(t_all // tm,),
        in_specs=_split_specs((tm, ATTN_WIDTH), n_first) + _split_specs((tm, GM_WIDTH), n_first)
        + _split_specs((tm, d), n_first) + [
            _resident((1, ATTN_WIDTH + GM_WIDTH, d), lambda i: (layer, 0, 0)),
            _resident((None, 1, d), lambda i: (layer, 0, 0)),
            _resident((None, 1, d), lambda i: (layer, 0, 0)),
            _resident((N_EXPERTS, d), const),
            _resident((N_EXPERTS, LANES), const),
        ],
        out_specs=(
            pl.BlockSpec((tm, d), row),
            pl.BlockSpec((tm, d // 2), row),
            pl.BlockSpec((8, tm), col),
            pl.BlockSpec((8, tm), col),
            pl.BlockSpec((N_EXPERTS, LANES), const),
        ),
        out_shape=(
            jax.ShapeDtypeStruct((t_all, d), F32),
            jax.ShapeDtypeStruct((t_all, d // 2), jnp.uint32),
            jax.ShapeDtypeStruct((8, t_all), jnp.int32),
            jax.ShapeDtypeStruct((8, t_all), F32),
            jax.ShapeDtypeStruct((N_EXPERTS, LANES), F32),
        ),
        scratch_shapes=[pltpu.VMEM((N_EXPERTS, LANES), F32)],
        compiler_params=_cparams(("arbitrary",)),
        name="outproj_router",
    )(attn_p, attn_s, gm_p, gm_s, xp, xs, wo, g1, b1, rwt, rb)


def _dispatch_body(pz_ref, pe_ref, nu_ref, dest_ref, x1_ref, rows_hbm, x1s, zbuf, sem, zsem, *, tb, n_blocks):
    i = pl.program_id(0)
    n = pl.num_programs(0)
    slot = i % 2

    def wait_tile(s):
        for _ in range(2):
            pltpu.make_async_copy(x1s.at[s], rows_hbm.at[pl.ds(0, tb), :], sem.at[s]).wait()

    x1s[slot] = x1_ref[...]

    for s in range(2):
        @pl.when(slot == s)
        def _(s=s):
            for r in range(tb):
                for k in (2, 3):
                    pltpu.make_async_copy(x1s.at[s, pl.ds(r, 1), :],
                                          rows_hbm.at[pl.ds(dest_ref[k, r], 1), :], sem.at[s]).start()

    @pl.when(i > 0)
    def _():
        wait_tile(1 - slot)

    @pl.when(i == n - 1)
    def _():
        wait_tile(slot)
        zbuf[...] = jnp.zeros_like(zbuf)

        def row_copy(r):
            return pltpu.make_async_copy(zbuf.at[pl.ds(0, 1), :], rows_hbm.at[pl.ds(r, 1), :], zsem.at[0])

        def block_copy(j):
            r0 = pl.multiple_of(j * EXPERT_ROWS, EXPERT_ROWS)
            return pltpu.make_async_copy(zbuf, rows_hbm.at[pl.ds(r0, EXPERT_ROWS), :], zsem.at[0])

        def span_copy(r0, size):
            r0 = pl.multiple_of(r0, SUBLANES)
            return pltpu.make_async_copy(zbuf.at[pl.ds(0, size), :], rows_hbm.at[pl.ds(r0, size), :], zsem.at[0])

        def fill_tail(e, act):
            lo, hi = pz_ref[e], pe_ref[e]
            mid = jnp.minimum((lo + (SUBLANES - 1)) // SUBLANES * SUBLANES, hi)
            lax.fori_loop(lo, mid, lambda r, c: (act(row_copy(r)), c)[1], 0)
            groups = (hi - mid) // SUBLANES
            at = mid
            for bit in range((EXPERT_ROWS // SUBLANES).bit_length()):
                size = SUBLANES << bit
                take = (groups >> bit) & 1

                @pl.when(take == 1)
                def _(at=at, size=size):
                    act(span_copy(at, size))

                at = at + take * size

        for act in (lambda c: c.start(), lambda c: c.wait()):
            for e in range(N_EXPERTS):
                fill_tail(e, act)
            lax.fori_loop(nu_ref[0], n_blocks, lambda j, c, act=act: (act(block_copy(j)), c)[1], 0)


def _dispatch(pad_lo, pad_hi, n_used, dest, x1, n_blocks, tb):
    t_all, d = x1.shape
    dt = x1.dtype
    kernel = functools.partial(_dispatch_body, tb=tb, n_blocks=n_blocks)
    grid_spec = pltpu.PrefetchScalarGridSpec(
        num_scalar_prefetch=3,
        grid=(t_all // tb,),
        in_specs=[
            pl.BlockSpec((8, tb), lambda i, pz, pe, nu: (0, i), memory_space=pltpu.SMEM),
            pl.BlockSpec((tb, d), lambda i, pz, pe, nu: (i, 0)),
        ],
        out_specs=pl.BlockSpec(memory_space=pl.ANY),
        scratch_shapes=[pltpu.VMEM((2, tb, d), dt), pltpu.VMEM((EXPERT_ROWS, d), dt),
                        pltpu.SemaphoreType.DMA((2,)), pltpu.SemaphoreType.DMA((1,))],
    )
    return pl.pallas_call(
        kernel,
        grid_spec=grid_spec,
        out_shape=jax.ShapeDtypeStruct((n_blocks * EXPERT_ROWS, d), dt),
        compiler_params=_cparams(("arbitrary",)),
        name="dispatch",
    )(pad_lo, pad_hi, n_used, dest, x1)


def _expert_body(be_ref, nu_ref, x_ref, wg_ref, wu_ref, wd_ref, o_ref, wg_s, wu_s, wd_s):
    j = pl.program_id(0)

    @pl.when(jnp.logical_or(j == 0, be_ref[j] != be_ref[jnp.maximum(j - 1, 0)]))
    def _():
        wg_s[...] = wg_ref[0, 0].astype(BF16)
        wu_s[...] = wu_ref[0, 0].astype(BF16)
        wd_s[...] = wd_ref[0, 0].astype(BF16)

    @pl.when(j < nu_ref[0])
    def _():
        lo, hi = _unpack_halves(x_ref[...])
        c = lo.shape[1]
        gate = jnp.dot(lo, wg_s[0:c, :], preferred_element_type=F32) \
            + jnp.dot(hi, wg_s[c:2 * c, :], preferred_element_type=F32)
        up = jnp.dot(lo, wu_s[0:c, :], preferred_element_type=F32) \
            + jnp.dot(hi, wu_s[c:2 * c, :], preferred_element_type=F32)
        h = gate * (1.0 / (1.0 + jnp.exp(-gate))) * up
        o_ref[...] = jnp.dot(h.astype(BF16), wd_s[...], preferred_element_type=F32)

    @pl.when(j >= nu_ref[0])
    def _():
        o_ref[...] = jnp.zeros_like(o_ref)


def _experts(block_expert, n_used, rows, wg, wu, wd, layer, n_blocks):
    d = wg.shape[2]
    de = wg.shape[3]
    grid_spec = pltpu.PrefetchScalarGridSpec(
        num_scalar_prefetch=2,
        grid=(n_blocks,),
        in_specs=[
            pl.BlockSpec((EXPERT_ROWS, d // 2), lambda j, be, nu: (jnp.minimum(j, nu[0] - 1), 0)),
            pl.BlockSpec((1, 1, d, de), lambda j, be, nu: (layer, be[j], 0, 0)),
            pl.BlockSpec((1, 1, d, de), lambda j, be, nu: (layer, be[j], 0, 0)),
            pl.BlockSpec((1, 1, de, d), lambda j, be, nu: (layer, be[j], 0, 0)),
        ],
        out_specs=pl.BlockSpec((EXPERT_ROWS, d), lambda j, be, nu: (j, 0)),
        scratch_shapes=[pltpu.VMEM((d, de), BF16), pltpu.VMEM((d, de), BF16), pltpu.VMEM((de, d), BF16)],
    )
    return pl.pallas_call(
        _expert_body,
        grid_spec=grid_spec,
        out_shape=jax.ShapeDtypeStruct((rows.shape[0], d), F32),
        compiler_params=_cparams(("arbitrary",)),
        name="experts",
    )(block_expert, n_used, rows, wg, wu, wd)


def _combine_body(d0_ref, dn_ref, x1_ref, gt_ref, rows_hbm, g2_ref, b2_ref, x2_ref, rbuf, sem, *, alpha):
    i = pl.program_id(0)
    n = pl.num_programs(0)
    tm = x1_ref.shape[0]
    slot = i % 2

    def row_copy(src_row, s, k, r):
        return pltpu.make_async_copy(rows_hbm.at[pl.ds(src_row, 1), :],
                                     rbuf.at[s, k, pl.ds(r, 1), :], sem.at[s])

    @pl.when(i == 0)
    def _():
        def body(r, c):
            for k in range(2):
                row_copy(d0_ref[2 + k, r], 0, k, r).start()
            return c
        lax.fori_loop(0, tm, body, 0, unroll=8)

    def wait_slot(s):
        for k in range(2):
            pltpu.make_async_copy(rows_hbm.at[pl.ds(0, tm), :], rbuf.at[s, k], sem.at[s]).wait()

    for s in range(2):
        @pl.when(jnp.logical_and(i + 1 < n, 1 - slot == s))
        def _(s=s):
            for r in range(tm):
                for k in range(2):
                    row_copy(dn_ref[2 + k, r], s, k, r).start()

    wait_slot(slot)
    y = rbuf[slot, 0] * gt_ref[:, 0:1] + rbuf[slot, 1] * gt_ref[:, 1:2]
    x2_ref[...] = _layer_norm_rows(np.float32(alpha) * x1_ref[...] + y, g2_ref[...], b2_ref[...])


def _combine(meta_i, x1, gates_t, out_rows, g2, b2, layer, alpha, tm, tile0, n_tiles):
    d = x1.shape[1]
    kernel = functools.partial(_combine_body, alpha=alpha)
    return pl.pallas_call(
        kernel,
        grid=(n_tiles,),
        in_specs=[
            pl.BlockSpec((8, tm), lambda i: (0, tile0), memory_space=pltpu.SMEM),
            pl.BlockSpec((8, tm), lambda i: (0, tile0 + jnp.minimum(i + 1, n_tiles - 1)),
                         memory_space=pltpu.SMEM),
            pl.BlockSpec((tm, d), lambda i: (tile0 + i, 0)),
            pl.BlockSpec((tm, 2), lambda i: (tile0 + i, 0)),
            pl.BlockSpec(memory_space=pl.ANY),
            _resident((None, 1, d), lambda i: (layer, 0, 0)),
            _resident((None, 1, d), lambda i: (layer, 0, 0)),
        ],
        out_specs=pl.BlockSpec((tm, d), lambda i: (i, 0)),
        out_shape=jax.ShapeDtypeStruct((n_tiles * tm, d), F32),
        scratch_shapes=[pltpu.VMEM((2, 2, tm, d), F32), pltpu.SemaphoreType.DMA((2,))],
        compiler_params=_cparams(("arbitrary",)),
        name="combine_ln",
    )(meta_i, meta_i, x1, gates_t, out_rows, g2, b2)


def _rope_table(pos):
    half = ROT_DIM // 2
    inv_freq = jnp.power(ROPE_THETA, -2.0 * jnp.arange(half, dtype=F32) / ROT_DIM)
    ang = pos.astype(F32)[:, None] * inv_freq[None, :]
    cos, sin = jnp.cos(ang), jnp.sin(ang)
    n = pos.shape[0]
    ones = jnp.ones((n, HEAD_DIM - ROT_DIM), F32)
    zeros = jnp.zeros((n, HEAD_DIM - ROT_DIM), F32)
    zh = jnp.zeros((n, half), F32)
    c = jnp.concatenate([cos, cos, ones], axis=1)
    s1 = jnp.concatenate([-sin, zh, zeros], axis=1)
    s2 = jnp.concatenate([zh, sin, zeros], axis=1)
    rep = LANES // HEAD_DIM
    return jnp.concatenate([jnp.tile(c, (1, rep)), jnp.tile(s1, (1, rep)), jnp.tile(s2, (1, rep))], axis=1)


def _block_plan(meta_i, counts, n_blocks):
    cnt = counts[:, 0].astype(jnp.int32)
    padded = (cnt + EXPERT_ROWS - 1) // EXPERT_ROWS * EXPERT_ROWS
    pend = jnp.cumsum(padded)
    pstart = pend - padded
    n_used = pend[-1] // EXPERT_ROWS
    blk = jnp.minimum(jnp.arange(n_blocks, dtype=jnp.int32), n_used - 1)
    block_expert = jnp.sum((blk[:, None] * EXPERT_ROWS >= pend[None, :]).astype(jnp.int32), axis=1)
    experts = jnp.arange(N_EXPERTS, dtype=jnp.int32)[None, :, None]
    start_of = jnp.sum(jnp.where(meta_i[0:2, None, :] == experts, pstart[None, :, None], 0), axis=1)
    dest = jnp.concatenate([meta_i[0:2], start_of + meta_i[2:4], meta_i[4:8]], axis=0)
    return block_expert, n_used.reshape(1), pstart + cnt, pend, dest


def kernel(x_prompt, x_sample, cache_k, cache_v, w_in, w_o, attn_sinks, attn_out_g, gm_out_g,
           gm_ln_g, gm_ln_b, gm_ws, gm_bs, ln1_g, ln1_b, ln2_g, ln2_b, router_w, router_b,
           w_gate, w_up, w_down):
    depth = w_in.shape[0]
    batch, seq, d = x_prompt.shape
    dbatch, ds, _ = x_sample.shape
    alpha = (2 * depth) ** 0.25
    t_p, t_s = batch * seq, dbatch * ds
    t_all = t_p + t_s
    tm = 256 if (t_p % 256 == 0 and t_s % 256 == 0) else 128
    tm_route = 512 if (t_p % 512 == 0 and t_s % 512 == 0) else tm
    w_in_b, w_o_b = w_in.astype(BF16), w_o.astype(BF16)
    tq = 512 if seq % 512 == 0 else seq
    assert seq % GM_CHUNK == 0 and seq % tq == 0 and t_all % tm == 0 and ds <= CHUNK
    assert tm % GM_CHUNK == 0 and tm % ds == 0
    n_blocks = -(-2 * t_all // EXPERT_ROWS) + N_EXPERTS

    xp, xs = x_prompt.reshape(t_p, d), x_sample.reshape(t_s, d)
    rope_p = jnp.tile(_rope_table(jnp.arange(seq)), (batch, 1))
    rope_s = jnp.tile(_rope_table(PAST_LEN + jnp.arange(ds)), (dbatch, 1))
    rwt = router_w.T.astype(BF16)
    rb = jnp.broadcast_to(router_b.astype(F32)[:, None], (N_EXPERTS, LANES))
    cache_k = cache_k.reshape(depth, dbatch, WINDOW, KV_WIDTH)
    cache_v = cache_v.reshape(depth, dbatch, WINDOW, KV_WIDTH)

    ln_g = gm_ln_g.reshape(depth, 1, GM_WIDTH)
    ln_b = gm_ln_b.reshape(depth, 1, GM_WIDTH)
    g_attn = attn_out_g.reshape(depth, 1, ATTN_WIDTH)
    g_gm = gm_out_g.reshape(depth, 1, GM_WIDTH)
    bias_full = jnp.repeat(jnp.swapaxes(gm_bs, 1, 2), GM_GROUP_DIM, axis=2)
    g1, b1 = ln1_g.reshape(depth, 1, d), ln1_b.reshape(depth, 1, d)
    g2, b2 = ln2_g.reshape(depth, 1, d), ln2_b.reshape(depth, 1, d)

    kp_out, vp_out, ks_out, vs_out, gv_out = [], [], [], [], []
    for l in range(depth):
        q_p, k32, v32, kp, vp, gm_p, _ = _inproj(
            xp, w_in_b, l, rope_p, ln_g, ln_b, gm_ws, bias_full, g_gm, tm, GM_CHUNK)
        q_s, k_s, v_s, _, _, gm_s, vn_s = _inproj(
            xs, w_in_b, l, rope_s, ln_g, ln_b, gm_ws, bias_full, g_gm, tm, ds)
        attn_p = _attn_prompt(attn_sinks[l], q_p, kp, vp, g_attn, l, batch, seq, tq)
        attn_s = _attn_sample(attn_sinks[l], q_s, k_s, v_s, cache_k, cache_v, g_attn, l, 0, dbatch, ds)
        x1, x1_packed, meta_i, meta_f, counts = _outproj(
            attn_p, attn_s, gm_p, gm_s, xp, xs, w_o_b, l, g1, b1, rwt, rb, alpha, tm_route)
        block_expert, n_used, pad_lo, pad_hi, dest = _block_plan(meta_i, counts, n_blocks)
        rows = _dispatch(pad_lo, pad_hi, n_used, dest, x1_packed, n_blocks, tm)
        out_rows = _experts(block_expert, n_used, rows, w_gate, w_up, w_down, l, n_blocks)
        gates_t = meta_f[0:2].T
        xp = _combine(dest, x1, gates_t, out_rows, g2, b2, l, alpha, tm, 0, t_p // tm)
        xs = _combine(dest, x1, gates_t, out_rows, g2, b2, l, alpha, tm, t_p // tm, t_s // tm)

        def last_window(a):
            rows = [lax.slice(a, ((b + 1) * seq - WINDOW, 0), ((b + 1) * seq, KV_WIDTH)) for b in range(batch)]
            return jnp.stack(rows).reshape(batch, WINDOW, N_KV_HEADS, HEAD_DIM)

        kp_out.append(last_window(k32))
        vp_out.append(last_window(v32))
        k_s = k_s.reshape(dbatch, ds, KV_WIDTH)
        v_s = v_s.reshape(dbatch, ds, KV_WIDTH)
        ks_out.append(jnp.concatenate([cache_k[l][:, ds:], k_s], axis=1)
                      .reshape(dbatch, WINDOW, N_KV_HEADS, HEAD_DIM))
        vs_out.append(jnp.concatenate([cache_v[l][:, ds:], v_s], axis=1)
                      .reshape(dbatch, WINDOW, N_KV_HEADS, HEAD_DIM))
        gv_out.append(vn_s.reshape(dbatch, ds, GM_GROUPS, GM_GROUP_DIM))

    return (xp.reshape(batch, seq, d), xs.reshape(dbatch, ds, d), jnp.stack(kp_out), jnp.stack(vp_out), jnp.stack(ks_out),
            jnp.stack(vs_out), jnp.stack(gv_out))
```

```python
import functools

import jax
import jax.numpy as jnp
import numpy as np
from jax import lax
from jax.experimental import pallas as pl
from jax.experimental.pallas import tpu as pltpu

F32 = jnp.float32
BF16 = jnp.bfloat16

CHUNK = 64
N_HEADS = 16
N_KV_HEADS = 4
HEAD_DIM = 64
Q_PER_KV = N_HEADS // N_KV_HEADS
ATTN_WIDTH = N_HEADS * HEAD_DIM
KV_WIDTH = N_KV_HEADS * HEAD_DIM
WINDOW = 128
ROT_DIM = HEAD_DIM // 4
ROPE_THETA = 500000.0
GM_CHUNK = 128
GM_GROUPS = 8
GM_GROUP_DIM = 128
GM_WIDTH = GM_GROUPS * GM_GROUP_DIM
N_EXPERTS = 16
N_EXPERT_GROUPS = 4
EXPERTS_PER_GROUP = N_EXPERTS // N_EXPERT_GROUPS
NORM_EPS = 1e-5
NEG_INF = -1e30
LOG2_E = 1.4426950408889634
PAST_LEN = 4096

LANES = 128
SUBLANES = 8
KEY_BLOCK = 256
Q_BLOCK = 128
EXPERT_ROWS = 512
VMEM_LIMIT = 56 * 1024 * 1024


def _cparams(sem):
    return pltpu.CompilerParams(dimension_semantics=sem, vmem_limit_bytes=VMEM_LIMIT)


def _resident(shape, index_map):
    return pl.BlockSpec(shape, index_map, pipeline_mode=pl.Buffered(1))


def _gelu(a):
    return 0.5 * a * (1.0 + lax.erf(a * np.float32(0.7071067811865476)))


def _pair_expand(t):
    r = pltpu.roll(t, HEAD_DIM, 1)
    lane = lax.broadcasted_iota(jnp.int32, t.shape, 1)
    lo = lane < HEAD_DIM
    return jnp.where(lo, t, r), jnp.where(lo, r, t)


def _expand_kv(a):
    a0, b0 = _pair_expand(a[:, 0:LANES])
    a1, b1 = _pair_expand(a[:, LANES:2 * LANES])
    return jnp.concatenate([a0, b0, a1, b1], axis=1)


def _pick_rows(first_ref, second_ref, n_first):
    return jnp.where(pl.program_id(0) < n_first, first_ref[...], second_ref[...])


def _split_specs(block, n_first):
    return [pl.BlockSpec(block, lambda i: (jnp.minimum(i, n_first - 1), 0)),
            pl.BlockSpec(block, lambda i: (jnp.maximum(i - n_first, 0), 0))]


def _gate_mix(w, y, rows):
    nchunk = y.shape[0] // rows
    pieces = []
    for n in range(nchunk):
        piece = y[n * rows:(n + 1) * rows, :]
        if rows < GM_CHUNK:
            piece = jnp.concatenate([piece, jnp.zeros((GM_CHUNK - rows, GM_GROUP_DIM), F32)], axis=0)
        pieces.append(piece.astype(BF16))
    mix = jnp.dot(w[0:rows, :], jnp.concatenate(pieces, axis=1), preferred_element_type=F32)
    return jnp.concatenate([mix[:, n * GM_GROUP_DIM:(n + 1) * GM_GROUP_DIM] for n in range(nchunk)], axis=0)


def _inproj_body(x_ref, w_ref, rope_ref, lng_ref, lnb_ref, ws_ref, bias_ref, ggm_ref,
                 q_ref, k_ref, v_ref, kp_ref, vp_ref, gm_ref, vn_ref, gu_s, gm_s, *, rows):
    xb = x_ref[...].astype(BF16)
    tm = xb.shape[0]
    cs = rope_ref[:, 0:LANES]
    s1 = rope_ref[:, LANES:2 * LANES]
    s2 = rope_ref[:, 2 * LANES:3 * LANES]

    def rope(a):
        n = a.shape[1] // LANES
        width = a.shape[1]
        c = jnp.concatenate([cs] * n, axis=1)
        m1 = jnp.concatenate([s1] * n, axis=1)
        m2 = jnp.concatenate([s2] * n, axis=1)
        return a * c + pltpu.roll(a, width - ROT_DIM // 2, 1) * m1 + pltpu.roll(a, ROT_DIM // 2, 1) * m2

    def proj(c0, c1):
        return jnp.dot(xb, w_ref[0, :, c0:c1], preferred_element_type=F32)

    half = ATTN_WIDTH // 2
    for h in range(2):
        q = rope(proj(h * half, (h + 1) * half)) * np.float32(HEAD_DIM ** -0.5 * LOG2_E)
        q_ref[:, h * half:(h + 1) * half] = q.astype(BF16)

    kv = proj(ATTN_WIDTH, ATTN_WIDTH + 2 * KV_WIDTH)
    k = rope(kv[:, 0:KV_WIDTH])
    v = kv[:, KV_WIDTH:2 * KV_WIDTH]
    k_ref[...] = k
    v_ref[...] = v
    kp_ref[...] = _expand_kv(k).astype(BF16)
    vp_ref[...] = _expand_kv(v).astype(BF16)

    base = ATTN_WIDTH + 2 * KV_WIDTH
    for h in range(2):
        gu_s[:, h * half:(h + 1) * half] = _gelu(proj(base + h * half, base + (h + 1) * half))

    ci = lax.broadcasted_iota(jnp.int32, (GM_CHUNK, GM_CHUNK), 0) // CHUNK
    cj = lax.broadcasted_iota(jnp.int32, (GM_CHUNK, GM_CHUNK), 1) // CHUNK
    visible = ci >= cj
    base = base + GM_WIDTH
    for h in range(2):
        gv = _gelu(proj(base + h * half, base + (h + 1) * half))
        for g in range(half // GM_GROUP_DIM):
            t = gv[:, g * GM_GROUP_DIM:(g + 1) * GM_GROUP_DIM]
            col = h * half + g * GM_GROUP_DIM
            tc = t - jnp.mean(t, axis=-1, keepdims=True)
            var = jnp.mean(tc * tc, axis=-1, keepdims=True)
            y = tc * lax.rsqrt(var + NORM_EPS) * lng_ref[:, col:col + GM_GROUP_DIM] \
                + lnb_ref[:, col:col + GM_GROUP_DIM]
            vn_ref[:, col:col + GM_GROUP_DIM] = y
            w = jnp.where(visible, ws_ref[col // GM_GROUP_DIM], np.float32(0.0)).astype(BF16)
            b = jnp.concatenate([bias_ref[0:rows, col:col + GM_GROUP_DIM]] * (tm // rows), axis=0)
            gm_s[:, col:col + GM_GROUP_DIM] = gu_s[:, col:col + GM_GROUP_DIM] * (_gate_mix(w, y, rows) + b)

    gm = gm_s[...]
    ms = jnp.mean(gm * gm, axis=-1, keepdims=True)
    gm_ref[...] = (gm * lax.rsqrt(ms + NORM_EPS) * ggm_ref[...]).astype(BF16)


def _inproj(x, w, layer, rope_tab, lng, lnb, ws, bias_full, g_gm, tm, rows):
    t_all, d = x.shape
    n_in = w.shape[2]
    row = lambda i: (i, 0)
    pick = lambda i: (layer, 0, 0)
    out_shape = (
        jax.ShapeDtypeStruct((t_all, ATTN_WIDTH), BF16),
        jax.ShapeDtypeStruct((t_all, KV_WIDTH), F32),
        jax.ShapeDtypeStruct((t_all, KV_WIDTH), F32),
        jax.ShapeDtypeStruct((t_all, 2 * KV_WIDTH), BF16),
        jax.ShapeDtypeStruct((t_all, 2 * KV_WIDTH), BF16),
        jax.ShapeDtypeStruct((t_all, GM_WIDTH), BF16),
        jax.ShapeDtypeStruct((t_all, GM_WIDTH), F32),
    )
    return pl.pallas_call(
        functools.partial(_inproj_body, rows=rows),
        grid=(t_all // tm,),
        in_specs=[
            pl.BlockSpec((tm, d), row),
            _resident((1, d, n_in), pick),
            pl.BlockSpec((tm, 3 * LANES), row),
            _resident((None, 1, GM_WIDTH), pick),
            _resident((None, 1, GM_WIDTH), pick),
            _resident((None, GM_GROUPS, GM_CHUNK, GM_CHUNK), lambda i: (layer, 0, 0, 0)),
            _resident((None, GM_CHUNK, GM_WIDTH), pick),
            _resident((None, 1, GM_WIDTH), pick),
        ],
        out_specs=(
            pl.BlockSpec((tm, ATTN_WIDTH), row),
            pl.BlockSpec((tm, KV_WIDTH), row),
            pl.BlockSpec((tm, KV_WIDTH), row),
            pl.BlockSpec((tm, 2 * KV_WIDTH), row),
            pl.BlockSpec((tm, 2 * KV_WIDTH), row),
            pl.BlockSpec((tm, GM_WIDTH), row),
            pl.BlockSpec((tm, GM_WIDTH), row),
        ),
        out_shape=out_shape,
        scratch_shapes=[pltpu.VMEM((tm, GM_WIDTH), F32), pltpu.VMEM((tm, GM_WIDTH), F32)],
        compiler_params=_cparams(("parallel",)),
        name="inproj",
    )(x, w, rope_tab, lng, lnb, ws, bias_full, g_gm)


def _attn_group(qg, kexp, vexp, bias, sinks):
    nk = kexp.shape[0]
    seg = lax.broadcasted_iota(jnp.int32, kexp.shape, 1) // HEAD_DIM
    zero = jnp.zeros_like(kexp)
    kbd = jnp.concatenate([jnp.where(seg == j, kexp, zero) for j in range(Q_PER_KV)], axis=0)
    vbd = jnp.concatenate([jnp.where(seg == j, vexp, zero) for j in range(Q_PER_KV)], axis=0)
    s = lax.dot_general(qg, kbd, (((1,), (1,)), ((), ())), preferred_element_type=F32)
    es, rs = [], []
    for j in range(Q_PER_KV):
        sj = s[:, j * nk:(j + 1) * nk] + bias
        sink = sinks[j] * np.float32(LOG2_E)
        m = jnp.maximum(jnp.max(sj, axis=-1, keepdims=True), sink)
        e = jnp.exp2(sj - m)
        den = jnp.sum(e, axis=-1, keepdims=True) + jnp.exp2(sink - m)
        es.append(e.astype(BF16))
        rs.append(1.0 / den)
    p = jnp.concatenate(es, axis=1)
    o = jnp.dot(p, vbd, preferred_element_type=F32)
    oseg = lax.broadcasted_iota(jnp.int32, o.shape, 1) // HEAD_DIM
    r = jnp.where(oseg == 0, rs[0], jnp.where(oseg == 1, rs[1], jnp.where(oseg == 2, rs[2], rs[3])))
    return o * r


def _attn_rows(q, kbuf, vbuf, bias, sink_ref, g_row):
    outs = []
    for h in range(N_KV_HEADS):
        kt = kbuf[:, h * LANES:(h + 1) * LANES]
        vt = vbuf[:, h * LANES:(h + 1) * LANES]
        kexp = jnp.concatenate([kt, kt], axis=1)
        vexp = jnp.concatenate([vt, vt], axis=1)
        sinks = [sink_ref[h * Q_PER_KV + j] for j in range(Q_PER_KV)]
        width = Q_PER_KV * HEAD_DIM
        outs.append(_attn_group(q[:, h * width:(h + 1) * width], kexp, vexp, bias, sinks))
    o = jnp.concatenate(outs, axis=1)
    ms = jnp.mean(o * o, axis=-1, keepdims=True)
    return o * lax.rsqrt(ms + NORM_EPS) * g_row


def _attn_prompt_body(sink_ref, q_ref, kc_ref, kprev_ref, vc_ref, vprev_ref, g_ref, o_ref,
                      kbuf, vbuf, *, tq):
    i = pl.program_id(1)
    kbuf[0:WINDOW, :] = kprev_ref[...]
    kbuf[WINDOW:WINDOW + tq, :] = kc_ref[...]
    vbuf[0:WINDOW, :] = vprev_ref[...]
    vbuf[WINDOW:WINDOW + tq, :] = vc_ref[...]
    rq = lax.broadcasted_iota(jnp.int32, (Q_BLOCK, KEY_BLOCK), 0) // CHUNK
    ck = lax.broadcasted_iota(jnp.int32, (Q_BLOCK, KEY_BLOCK), 1) // CHUNK
    rel = ck - rq
    band = (rel >= 0) & (rel <= WINDOW // CHUNK)
    g_row = g_ref[...]

    def sub(s, carry):
        r0 = pl.multiple_of(s * Q_BLOCK, Q_BLOCK)
        first = jnp.logical_and(i == 0, s == 0)
        kmin = jnp.where(first, WINDOW // CHUNK, 0)
        ok = jnp.logical_and(band, ck >= kmin)
        bias = jnp.where(ok, np.float32(0.0), np.float32(NEG_INF))
        q = q_ref[pl.ds(r0, Q_BLOCK), :]
        kb = kbuf[pl.ds(r0, KEY_BLOCK), :]
        vb = vbuf[pl.ds(r0, KEY_BLOCK), :]
        o_ref[pl.ds(r0, Q_BLOCK), :] = _attn_rows(q, kb, vb, bias, sink_ref, g_row).astype(BF16)
        return carry

    lax.fori_loop(0, tq // Q_BLOCK, sub, 0)


def _attn_prompt(sinks, q, kp, vp, g_attn, layer, batch, seq, tq):
    nq = seq // tq
    per = tq // WINDOW
    cur = lambda b, i: (b * nq + i, 0)
    prev = lambda b, i: (b * (seq // WINDOW) + jnp.maximum(i * per - 1, 0), 0)
    kernel = functools.partial(_attn_prompt_body, tq=tq)
    return pl.pallas_call(
        kernel,
        grid=(batch, nq),
        in_specs=[
            pl.BlockSpec(memory_space=pltpu.SMEM),
            pl.BlockSpec((tq, ATTN_WIDTH), cur),
            pl.BlockSpec((tq, 2 * KV_WIDTH), cur),
            pl.BlockSpec((WINDOW, 2 * KV_WIDTH), prev),
            pl.BlockSpec((tq, 2 * KV_WIDTH), cur),
            pl.BlockSpec((WINDOW, 2 * KV_WIDTH), prev),
            pl.BlockSpec((None, 1, ATTN_WIDTH), lambda b, i: (layer, 0, 0)),
        ],
        out_specs=pl.BlockSpec((tq, ATTN_WIDTH), cur),
        out_shape=jax.ShapeDtypeStruct((batch * seq, ATTN_WIDTH), BF16),
        scratch_shapes=[pltpu.VMEM((WINDOW + tq, 2 * KV_WIDTH), BF16),
                        pltpu.VMEM((WINDOW + tq, 2 * KV_WIDTH), BF16)],
        compiler_params=_cparams(("parallel", "parallel")),
        name="attn_prompt",
    )(sinks, q, kp, kp, vp, vp, g_attn)


def _attn_sample_body(sink_ref, q_ref, ck_ref, cv_ref, kn_ref, vn_ref, g_ref, o_ref, *, ds):
    pad = jnp.zeros((KEY_BLOCK - WINDOW - ds, KV_WIDTH), F32)
    k_all = jnp.concatenate([ck_ref[0], kn_ref[...], pad], axis=0)
    v_all = jnp.concatenate([cv_ref[0], vn_ref[...], pad], axis=0)
    kb = _expand_kv(k_all).astype(BF16)
    vb = _expand_kv(v_all).astype(BF16)
    col = lax.broadcasted_iota(jnp.int32, (ds, KEY_BLOCK), 1)
    bias = jnp.where(col < WINDOW + ds, np.float32(0.0), np.float32(NEG_INF))
    o_ref[...] = _attn_rows(q_ref[...], kb, vb, bias, sink_ref, g_ref[...]).astype(BF16)


def _attn_sample(sinks, q, k32, v32, cache_k, cache_v, g_attn, layer, row0, dbatch, ds):
    off = row0 // ds
    new = lambda b: (off + b, 0)
    kernel = functools.partial(_attn_sample_body, ds=ds)
    return pl.pallas_call(
        kernel,
        grid=(dbatch,),
        in_specs=[
            pl.BlockSpec(memory_space=pltpu.SMEM),
            pl.BlockSpec((ds, ATTN_WIDTH), new),
            pl.BlockSpec((None, 1, WINDOW, KV_WIDTH), lambda b: (layer, b, 0, 0)),
            pl.BlockSpec((None, 1, WINDOW, KV_WIDTH), lambda b: (layer, b, 0, 0)),
            pl.BlockSpec((ds, KV_WIDTH), new),
            pl.BlockSpec((ds, KV_WIDTH), new),
            pl.BlockSpec((None, 1, ATTN_WIDTH), lambda b: (layer, 0, 0)),
        ],
        out_specs=pl.BlockSpec((ds, ATTN_WIDTH), lambda b: (b, 0)),
        out_shape=jax.ShapeDtypeStruct((dbatch * ds, ATTN_WIDTH), BF16),
        compiler_params=_cparams(("parallel",)),
        name="attn_sample",
    )(sinks, q, cache_k, cache_v, k32, v32, g_attn)


def _layer_norm_rows(z, g, b):
    zc = z - jnp.mean(z, axis=-1, keepdims=True)
    var = jnp.mean(zc * zc, axis=-1, keepdims=True)
    return zc * lax.rsqrt(var + NORM_EPS) * g + b


def _tree_sum(items):
    while len(items) > 1:
        items = [a + b for a, b in zip(items[0::2], items[1::2])] + (items[-1:] if len(items) % 2 else [])
    return items[0]


def _first_max(items):
    while len(items) > 1:
        nxt = [tuple(jnp.where(a[0] >= b[0], x, y) for x, y in zip(a, b))
               for a, b in zip(items[0::2], items[1::2])]
        items = nxt + (items[-1:] if len(items) % 2 else [])
    return items[0]


def _route_rows(lg, rb):
    sc = 1.0 / (1.0 + jnp.exp(-lg))
    sel = sc + rb
    rows = [sel[e:e + 1, :] for e in range(N_EXPERTS)]
    scr = [sc[e:e + 1, :] for e in range(N_EXPERTS)]
    groups = []
    for g in range(N_EXPERT_GROUPS):
        a, b, c, d = rows[g * EXPERTS_PER_GROUP:(g + 1) * EXPERTS_PER_GROUP]
        hi1, lo1 = jnp.maximum(a, b), jnp.minimum(a, b)
        hi2, lo2 = jnp.maximum(c, d), jnp.minimum(c, d)
        top1 = jnp.maximum(hi1, hi2)
        top2 = jnp.maximum(jnp.minimum(hi1, hi2), jnp.maximum(lo1, lo2))
        members = tuple(rows[g * EXPERTS_PER_GROUP + j] for j in range(EXPERTS_PER_GROUP))
        scores = tuple(scr[g * EXPERTS_PER_GROUP + j] for j in range(EXPERTS_PER_GROUP))
        groups.append((top1 + top2, jnp.full_like(top1, g * EXPERTS_PER_GROUP)) + members + scores)
    best = _first_max(groups)
    first_id = best[1]
    cand = [(best[2 + j], best[2 + EXPERTS_PER_GROUP + j], first_id + np.float32(j))
            for j in range(EXPERTS_PER_GROUP)]
    _, g0, idx0 = _first_max(cand)
    neg = np.float32(-np.inf)
    cand2 = [(jnp.where(c[2] == idx0, neg, c[0]), c[1], c[2]) for c in cand]
    _, g1, idx1 = _first_max(cand2)
    tot = g0 + g1
    return idx0, idx1, g0 / tot, g1 / tot


def _outproj_body(ap_ref, as_ref, gp_ref, gs_ref, xp_ref, xs_ref, wo_ref, g1_ref, b1_ref, rwt_ref, rb_ref,
                  x1_ref, mi_ref, mf_ref, cnt_ref, carry, *, alpha, n_first):
    i = pl.program_id(0)
    tm = xp_ref.shape[0]

    @pl.when(i == 0)
    def _():
        carry[...] = jnp.zeros_like(carry)

    y = jnp.dot(_pick_rows(ap_ref, as_ref, n_first), wo_ref[0, 0:ATTN_WIDTH, :], preferred_element_type=F32)
    y = y + jnp.dot(_pick_rows(gp_ref, gs_ref, n_first), wo_ref[0, ATTN_WIDTH:ATTN_WIDTH + GM_WIDTH, :],
                    preferred_element_type=F32)
    x1 = _layer_norm_rows(np.float32(alpha) * _pick_rows(xp_ref, xs_ref, n_first) + y,
                          g1_ref[...], b1_ref[...])
    x1_ref[...] = x1

    lg = lax.dot_general(rwt_ref[...], x1.astype(BF16), (((1,), (1,)), ((), ())),
                         preferred_element_type=F32)
    rb = jnp.concatenate([rb_ref[...]] * (tm // LANES), axis=1)
    idx0, idx1, gate0, gate1 = _route_rows(lg, rb)

    one, zero = np.float32(1.0), np.float32(0.0)
    hit0 = [idx0 == np.float32(e) for e in range(N_EXPERTS)]
    hit1 = [idx1 == np.float32(e) for e in range(N_EXPERTS)]
    chosen = jnp.concatenate(
        [jnp.where(jnp.logical_or(hit0[e], hit1[e]), one, zero) for e in range(N_EXPERTS)], axis=0)
    s_idx = lax.broadcasted_iota(jnp.int32, (tm, tm), 0)
    t_idx = lax.broadcasted_iota(jnp.int32, (tm, tm), 1)
    before = jnp.where(s_idx < t_idx, np.float32(1.0), zero).astype(BF16)
    base = carry[...]
    rank = jnp.dot(chosen.astype(BF16), before, preferred_element_type=F32) \
        + jnp.concatenate([base] * (tm // LANES), axis=1)
    carry[...] = base + jnp.sum(chosen, axis=1, keepdims=True)
    cnt_ref[...] = carry[...]

    rank_rows = [rank[e:e + 1, :] for e in range(N_EXPERTS)]
    rank0 = _tree_sum([jnp.where(hit0[e], rank_rows[e], zero) for e in range(N_EXPERTS)])
    rank1 = _tree_sum([jnp.where(hit1[e], rank_rows[e], zero) for e in range(N_EXPERTS)])
    zr = jnp.zeros((1, tm), F32)
    mi_ref[...] = jnp.concatenate([idx0, idx1, rank0, rank1, zr, zr, zr, zr], axis=0).astype(jnp.int32)
    mf_ref[...] = jnp.concatenate([gate0, gate1, zr, zr, zr, zr, zr, zr], axis=0)


def _outproj(attn_p, attn_s, gm_p, gm_s, xp, xs, wo, layer, g1, b1, rwt, rb, alpha, tm):
    d = xp.shape[1]
    t_all = xp.shape[0] + xs.shape[0]
    n_first = xp.shape[0] // tm
    row = lambda i: (i, 0)
    col = lambda i: (0, i)
    const = lambda i: (0, 0)
    kernel = functools.partial(_outproj_body, alpha=alpha, n_first=n_first)
    return pl.pallas_call(
        kernel,
        grid=(t_all // tm,),
        in_specs=_split_specs((tm, ATTN_WIDTH), n_first) + _split_specs((tm, GM_WIDTH), n_first)
        + _split_specs((tm, d), n_first) + [
            _resident((1, ATTN_WIDTH + GM_WIDTH, d), lambda i: (layer, 0, 0)),
            _resident((None, 1, d), lambda i: (layer, 0, 0)),
            _resident((None, 1, d), lambda i: (layer, 0, 0)),
            _resident((N_EXPERTS, d), const),
            _resident((N_EXPERTS, LANES), const),
        ],
        out_specs=(
            pl.BlockSpec((tm, d), row),
            pl.BlockSpec((8, tm), col),
            pl.BlockSpec((8, tm), col),
            pl.BlockSpec((N_EXPERTS, LANES), const),
        ),
        out_shape=(
            jax.ShapeDtypeStruct((t_all, d), F32),
            jax.ShapeDtypeStruct((8, t_all), jnp.int32),
            jax.ShapeDtypeStruct((8, t_all), F32),
            jax.ShapeDtypeStruct((N_EXPERTS, LANES), F32),
        ),
        scratch_shapes=[pltpu.VMEM((N_EXPERTS, LANES), F32)],
        compiler_params=_cparams(("arbitrary",)),
        name="outproj_router",
    )(attn_p, attn_s, gm_p, gm_s, xp, xs, wo, g1, b1, rwt, rb)


def _dispatch_body(pz_ref, pe_ref, nu_ref, dest_ref, x1_ref, rows_hbm, x1s, zbuf, sem, zsem, *, tb, n_blocks):
    i = pl.program_id(0)
    n = pl.num_programs(0)
    slot = i % 2

    def wait_tile(s):
        for _ in range(2):
            pltpu.make_async_copy(x1s.at[s], rows_hbm.at[pl.ds(0, tb), :], sem.at[s]).wait()

    x1s[slot] = x1_ref[...]

    for s in range(2):
        @pl.when(slot == s)
        def _(s=s):
            for r in range(tb):
                for k in (2, 3):
                    pltpu.make_async_copy(x1s.at[s, pl.ds(r, 1), :],
                                          rows_hbm.at[pl.ds(dest_ref[k, r], 1), :], sem.at[s]).start()

    @pl.when(i > 0)
    def _():
        wait_tile(1 - slot)

    @pl.when(i == n - 1)
    def _():
        wait_tile(slot)
        zbuf[...] = jnp.zeros_like(zbuf)

        def row_copy(r):
            return pltpu.make_async_copy(zbuf.at[pl.ds(0, 1), :], rows_hbm.at[pl.ds(r, 1), :], zsem.at[0])

        def block_copy(j):
            r0 = pl.multiple_of(j * EXPERT_ROWS, EXPERT_ROWS)
            return pltpu.make_async_copy(zbuf, rows_hbm.at[pl.ds(r0, EXPERT_ROWS), :], zsem.at[0])

        def span_copy(r0, size):
            r0 = pl.multiple_of(r0, SUBLANES)
            return pltpu.make_async_copy(zbuf.at[pl.ds(0, size), :], rows_hbm.at[pl.ds(r0, size), :], zsem.at[0])

        def fill_tail(e, act):
            lo, hi = pz_ref[e], pe_ref[e]
            mid = jnp.minimum((lo + (SUBLANES - 1)) // SUBLANES * SUBLANES, hi)
            lax.fori_loop(lo, mid, lambda r, c: (act(row_copy(r)), c)[1], 0)
            groups = (hi - mid) // SUBLANES
            at = mid
            for bit in range((EXPERT_ROWS // SUBLANES).bit_length()):
                size = SUBLANES << bit
                take = (groups >> bit) & 1

                @pl.when(take == 1)
                def _(at=at, size=size):
                    act(span_copy(at, size))

                at = at + take * size

        for act in (lambda c: c.start(), lambda c: c.wait()):
            for e in range(N_EXPERTS):
                fill_tail(e, act)
            lax.fori_loop(nu_ref[0], n_blocks, lambda j, c, act=act: (act(block_copy(j)), c)[1], 0)


def _dispatch(pad_lo, pad_hi, n_used, dest, x1, n_blocks, tb):
    t_all, d = x1.shape
    dt = x1.dtype
    kernel = functools.partial(_dispatch_body, tb=tb, n_blocks=n_blocks)
    grid_spec = pltpu.PrefetchScalarGridSpec(
        num_scalar_prefetch=3,
        grid=(t_all // tb,),
        in_specs=[
            pl.BlockSpec((8, tb), lambda i, pz, pe, nu: (0, i), memory_space=pltpu.SMEM),
            pl.BlockSpec((tb, d), lambda i, pz, pe, nu: (i, 0)),
        ],
        out_specs=pl.BlockSpec(memory_space=pl.ANY),
        scratch_shapes=[pltpu.VMEM((2, tb, d), dt), pltpu.VMEM((EXPERT_ROWS, d), dt),
                        pltpu.SemaphoreType.DMA((2,)), pltpu.SemaphoreType.DMA((1,))],
    )
    return pl.pallas_call(
        kernel,
        grid_spec=grid_spec,
        out_shape=jax.ShapeDtypeStruct((n_blocks * EXPERT_ROWS, d), dt),
        compiler_params=_cparams(("arbitrary",)),
        name="dispatch",
    )(pad_lo, pad_hi, n_used, dest, x1)


def _expert_body(be_ref, nu_ref, x_ref, wg_ref, wu_ref, wd_ref, o_ref, wg_s, wu_s, wd_s):
    j = pl.program_id(0)

    @pl.when(jnp.logical_or(j == 0, be_ref[j] != be_ref[jnp.maximum(j - 1, 0)]))
    def _():
        wg_s[...] = wg_ref[0, 0].astype(BF16)
        wu_s[...] = wu_ref[0, 0].astype(BF16)
        wd_s[...] = wd_ref[0, 0].astype(BF16)

    @pl.when(j < nu_ref[0])
    def _():
        xb = x_ref[...].astype(BF16)
        gate = jnp.dot(xb, wg_s[...], preferred_element_type=F32)
        up = jnp.dot(xb, wu_s[...], preferred_element_type=F32)
        h = gate * (1.0 / (1.0 + jnp.exp(-gate))) * up
        o_ref[...] = jnp.dot(h.astype(BF16), wd_s[...], preferred_element_type=F32)

    @pl.when(j >= nu_ref[0])
    def _():
        o_ref[...] = jnp.zeros_like(o_ref)


def _experts(block_expert, n_used, rows, wg, wu, wd, layer, n_blocks):
    d = wg.shape[2]
    de = wg.shape[3]
    grid_spec = pltpu.PrefetchScalarGridSpec(
        num_scalar_prefetch=2,
        grid=(n_blocks,),
        in_specs=[
            pl.BlockSpec((EXPERT_ROWS, d), lambda j, be, nu: (jnp.minimum(j, nu[0] - 1), 0)),
            pl.BlockSpec((1, 1, d, de), lambda j, be, nu: (layer, be[j], 0, 0)),
            pl.BlockSpec((1, 1, d, de), lambda j, be, nu: (layer, be[j], 0, 0)),
            pl.BlockSpec((1, 1, de, d), lambda j, be, nu: (layer, be[j], 0, 0)),
        ],
        out_specs=pl.BlockSpec((EXPERT_ROWS, d), lambda j, be, nu: (j, 0)),
        scratch_shapes=[pltpu.VMEM((d, de), BF16), pltpu.VMEM((d, de), BF16), pltpu.VMEM((de, d), BF16)],
    )
    return pl.pallas_call(
        _expert_body,
        grid_spec=grid_spec,
        out_shape=jax.ShapeDtypeStruct((rows.shape[0], d), F32),
        compiler_params=_cparams(("arbitrary",)),
        name="experts",
    )(block_expert, n_used, rows, wg, wu, wd)


def _combine_body(d0_ref, dn_ref, x1_ref, gt_ref, rows_hbm, g2_ref, b2_ref, x2_ref, rbuf, sem, *, alpha):
    i = pl.program_id(0)
    n = pl.num_programs(0)
    tm = x1_ref.shape[0]
    slot = i % 2

    def row_copy(src_row, s, k, r):
        return pltpu.make_async_copy(rows_hbm.at[pl.ds(src_row, 1), :],
                                     rbuf.at[s, k, pl.ds(r, 1), :], sem.at[s])

    @pl.when(i == 0)
    def _():
        def body(r, c):
            for k in range(2):
                row_copy(d0_ref[2 + k, r], 0, k, r).start()
            return c
        lax.fori_loop(0, tm, body, 0, unroll=8)

    def wait_slot(s):
        for k in range(2):
            pltpu.make_async_copy(rows_hbm.at[pl.ds(0, tm), :], rbuf.at[s, k], sem.at[s]).wait()

    for s in range(2):
        @pl.when(jnp.logical_and(i + 1 < n, 1 - slot == s))
        def _(s=s):
            for r in range(tm):
                for k in range(2):
                    row_copy(dn_ref[2 + k, r], s, k, r).start()

    wait_slot(slot)
    y = rbuf[slot, 0] * gt_ref[:, 0:1] + rbuf[slot, 1] * gt_ref[:, 1:2]
    x2_ref[...] = _layer_norm_rows(np.float32(alpha) * x1_ref[...] + y, g2_ref[...], b2_ref[...])


def _combine(meta_i, x1, gates_t, out_rows, g2, b2, layer, alpha, tm, tile0, n_tiles):
    d = x1.shape[1]
    kernel = functools.partial(_combine_body, alpha=alpha)
    return pl.pallas_call(
        kernel,
        grid=(n_tiles,),
        in_specs=[
            pl.BlockSpec((8, tm), lambda i: (0, tile0), memory_space=pltpu.SMEM),
            pl.BlockSpec((8, tm), lambda i: (0, tile0 + jnp.minimum(i + 1, n_tiles - 1)),
                         memory_space=pltpu.SMEM),
            pl.BlockSpec((tm, d), lambda i: (tile0 + i, 0)),
            pl.BlockSpec((tm, 2), lambda i: (tile0 + i, 0)),
            pl.BlockSpec(memory_space=pl.ANY),
            _resident((None, 1, d), lambda i: (layer, 0, 0)),
            _resident((None, 1, d), lambda i: (layer, 0, 0)),
        ],
        out_specs=pl.BlockSpec((tm, d), lambda i: (i, 0)),
        out_shape=jax.ShapeDtypeStruct((n_tiles * tm, d), F32),
        scratch_shapes=[pltpu.VMEM((2, 2, tm, d), F32), pltpu.SemaphoreType.DMA((2,))],
        compiler_params=_cparams(("arbitrary",)),
        name="combine_ln",
    )(meta_i, meta_i, x1, gates_t, out_rows, g2, b2)


def _rope_table(pos):
    half = ROT_DIM // 2
    inv_freq = jnp.power(ROPE_THETA, -2.0 * jnp.arange(half, dtype=F32) / ROT_DIM)
    ang = pos.astype(F32)[:, None] * inv_freq[None, :]
    cos, sin = jnp.cos(ang), jnp.sin(ang)
    n = pos.shape[0]
    ones = jnp.ones((n, HEAD_DIM - ROT_DIM), F32)
    zeros = jnp.zeros((n, HEAD_DIM - ROT_DIM), F32)
    zh = jnp.zeros((n, half), F32)
    c = jnp.concatenate([cos, cos, ones], axis=1)
    s1 = jnp.concatenate([-sin, zh, zeros], axis=1)
    s2 = jnp.concatenate([zh, sin, zeros], axis=1)
    rep = LANES // HEAD_DIM
    return jnp.concatenate([jnp.tile(c, (1, rep)), jnp.tile(s1, (1, rep)), jnp.tile(s2, (1, rep))], axis=1)


def _block_plan(meta_i, counts, n_blocks):
    cnt = counts[:, 0].astype(jnp.int32)
    padded = (cnt + EXPERT_ROWS - 1) // EXPERT_ROWS * EXPERT_ROWS
    pend = jnp.cumsum(padded)
    pstart = pend - padded
    n_used = pend[-1] // EXPERT_ROWS
    blk = jnp.minimum(jnp.arange(n_blocks, dtype=jnp.int32), n_used - 1)
    block_expert = jnp.sum((blk[:, None] * EXPERT_ROWS >= pend[None, :]).astype(jnp.int32), axis=1)
    experts = jnp.arange(N_EXPERTS, dtype=jnp.int32)[None, :, None]
    start_of = jnp.sum(jnp.where(meta_i[0:2, None, :] == experts, pstart[None, :, None], 0), axis=1)
    dest = jnp.concatenate([meta_i[0:2], start_of + meta_i[2:4], meta_i[4:8]], axis=0)
    return block_expert, n_used.reshape(1), pstart + cnt, pend, dest


def kernel(x_prompt, x_sample, cache_k, cache_v, w_in, w_o, attn_sinks, attn_out_g, gm_out_g,
           gm_ln_g, gm_ln_b, gm_ws, gm_bs, ln1_g, ln1_b, ln2_g, ln2_b, router_w, router_b,
           w_gate, w_up, w_down):
    depth = w_in.shape[0]
    batch, seq, d = x_prompt.shape
    dbatch, ds, _ = x_sample.shape
    alpha = (2 * depth) ** 0.25
    t_p, t_s = batch * seq, dbatch * ds
    t_all = t_p + t_s
    tm = 256 if (t_p % 256 == 0 and t_s % 256 == 0) else 128
    tm_route = 512 if (t_p % 512 == 0 and t_s % 512 == 0) else tm
    w_in_b, w_o_b = w_in.astype(BF16), w_o.astype(BF16)
    tq = 512 if seq % 512 == 0 else seq
    assert seq % GM_CHUNK == 0 and seq % tq == 0 and t_all % tm == 0 and ds <= CHUNK
    assert tm % GM_CHUNK == 0 and tm % ds == 0
    n_blocks = -(-2 * t_all // EXPERT_ROWS) + N_EXPERTS

    xp, xs = x_prompt.reshape(t_p, d), x_sample.reshape(t_s, d)
    rope_p = jnp.tile(_rope_table(jnp.arange(seq)), (batch, 1))
    rope_s = jnp.tile(_rope_table(PAST_LEN + jnp.arange(ds)), (dbatch, 1))
    rwt = router_w.T.astype(BF16)
    rb = jnp.broadcast_to(router_b.astype(F32)[:, None], (N_EXPERTS, LANES))
    cache_k = cache_k.reshape(depth, dbatch, WINDOW, KV_WIDTH)
    cache_v = cache_v.reshape(depth, dbatch, WINDOW, KV_WIDTH)

    ln_g = gm_ln_g.reshape(depth, 1, GM_WIDTH)
    ln_b = gm_ln_b.reshape(depth, 1, GM_WIDTH)
    g_attn = attn_out_g.reshape(depth, 1, ATTN_WIDTH)
    g_gm = gm_out_g.reshape(depth, 1, GM_WIDTH)
    bias_full = jnp.repeat(jnp.swapaxes(gm_bs, 1, 2), GM_GROUP_DIM, axis=2)
    g1, b1 = ln1_g.reshape(depth, 1, d), ln1_b.reshape(depth, 1, d)
    g2, b2 = ln2_g.reshape(depth, 1, d), ln2_b.reshape(depth, 1, d)

    kp_out, vp_out, ks_out, vs_out, gv_out = [], [], [], [], []
    for l in range(depth):
        q_p, k32, v32, kp, vp, gm_p, _ = _inproj(
            xp, w_in_b, l, rope_p, ln_g, ln_b, gm_ws, bias_full, g_gm, tm_route, GM_CHUNK)
        q_s, k_s, v_s, _, _, gm_s, vn_s = _inproj(
            xs, w_in_b, l, rope_s, ln_g, ln_b, gm_ws, bias_full, g_gm, tm_route, ds)
        attn_p = _attn_prompt(attn_sinks[l], q_p, kp, vp, g_attn, l, batch, seq, tq)
        attn_s = _attn_sample(attn_sinks[l], q_s, k_s, v_s, cache_k, cache_v, g_attn, l, 0, dbatch, ds)
        x1, meta_i, meta_f, counts = _outproj(
            attn_p, attn_s, gm_p, gm_s, xp, xs, w_o_b, l, g1, b1, rwt, rb, alpha, tm_route)
        block_expert, n_used, pad_lo, pad_hi, dest = _block_plan(meta_i, counts, n_blocks)
        rows = _dispatch(pad_lo, pad_hi, n_used, dest, x1, n_blocks, tm)
        out_rows = _experts(block_expert, n_used, rows, w_gate, w_up, w_down, l, n_blocks)
        gates_t = meta_f[0:2].T
        xp = _combine(dest, x1, gates_t, out_rows, g2, b2, l, alpha, tm, 0, t_p // tm)
        xs = _combine(dest, x1, gates_t, out_rows, g2, b2, l, alpha, tm, t_p // tm, t_s // tm)

        def last_window(a):
            rows = [lax.slice(a, ((b + 1) * seq - WINDOW, 0), ((b + 1) * seq, KV_WIDTH)) for b in range(batch)]
            return jnp.stack(rows).reshape(batch, WINDOW, N_KV_HEADS, HEAD_DIM)

        kp_out.append(last_window(k32))
        vp_out.append(last_window(v32))
        k_s = k_s.reshape(dbatch, ds, KV_WIDTH)
        v_s = v_s.reshape(dbatch, ds, KV_WIDTH)
        ks_out.append(jnp.concatenate([cache_k[l][:, ds:], k_s], axis=1)
                      .reshape(dbatch, WINDOW, N_KV_HEADS, HEAD_DIM))
        vs_out.append(jnp.concatenate([cache_v[l][:, ds:], v_s], axis=1)
                      .reshape(dbatch, WINDOW, N_KV_HEADS, HEAD_DIM))
        gv_out.append(vn_s.reshape(dbatch, ds, GM_GROUPS, GM_GROUP_DIM))

    return (xp.reshape(batch, seq, d), xs.reshape(dbatch, ds, d), jnp.stack(kp_out), jnp.stack(vp_out), jnp.stack(ks_out),
            jnp.stack(vs_out), jnp.stack(gv_out))
```
